```python
import math
import jax
import jax.numpy as jnp
from jax import lax
import numpy as np

D_MODEL = 2048
BATCH = 2
SEQ = 4096
DEPTH = 2

GRID_W = 64
CTX_LEN = 256
N_BRANCH = 4
BRANCH_W = D_MODEL // 4
S5_GROUP_CH = 16
S5_GROUPS = BRANCH_W // S5_GROUP_CH
S5_STATE = 64
GQA_HEAD_DIM = 128
GQA_HEADS = BRANCH_W // GQA_HEAD_DIM
GQA_KV_HEADS = GQA_HEADS // 2
RET_HEADS = 4
RET_V_DIM = BRANCH_W // RET_HEADS
RET_QK_DIM = RET_V_DIM // 2
RET_CHUNK = 128
MLA_HEADS = 4
MLA_Q_LORA = BRANCH_W
MLA_KV_LORA = BRANCH_W // 2
MLA_NOPE_DIM = 128
MLA_ROPE_DIM = 64
MLA_V_DIM = BRANCH_W // MLA_HEADS
N_EXPERTS = 16
EXPERT_FF = D_MODEL // 2
EC_CAPACITY_FACTOR = 2
Q_BLOCK = 128
ROPE_BASE = 10000.0
NORM_EPS = 1e-6
DEEPNORM_ALPHA = (2 * DEPTH) ** 0.25
DEEPNORM_BETA = (8 * DEPTH) ** -0.25
IN_WIDTHS = (
    BRANCH_W,
    GQA_HEADS * GQA_HEAD_DIM, GQA_KV_HEADS * GQA_HEAD_DIM, GQA_KV_HEADS * GQA_HEAD_DIM,
    RET_HEADS * RET_QK_DIM, RET_HEADS * RET_QK_DIM, RET_HEADS * RET_V_DIM, RET_HEADS * RET_V_DIM,
    MLA_Q_LORA, MLA_KV_LORA, MLA_ROPE_DIM,
)
IN_TOTAL = sum(IN_WIDTHS)
IN_OFFSETS = tuple(int(v) for v in np.cumsum(IN_WIDTHS)[:-1])
F32 = jnp.float32

kernel_name = 'hybrid_s5_gqa_retention_mla_ecmoe_dit'


def standardize(x):
    xf = x.astype(F32)
    xc = xf - jnp.mean(xf, -1, keepdims=True)
    return xc * lax.rsqrt(jnp.mean(xc * xc, -1, keepdims=True) + NORM_EPS)


def layer_norm(x, g, b):
    return (standardize(x) * g + b).astype(x.dtype)


def modulate(x, shift, scale):
    return (standardize(x) * (1.0 + scale) + shift).astype(x.dtype)


def rms_norm(x, g):
    xf = x.astype(F32)
    return (xf * lax.rsqrt(jnp.mean(xf * xf, -1, keepdims=True) + NORM_EPS) * g).astype(x.dtype)


def rope_1d(x, pos):
    n = x.shape[-1] // 2
    inv = ROPE_BASE ** (-jnp.arange(n, dtype=F32) / n)
    ang = pos.astype(F32)[:, None] * inv[None, :]
    cos, sin = jnp.cos(ang), jnp.sin(ang)
    xf = x.astype(F32)
    x1, x2 = xf[..., :n], xf[..., n:]
    return jnp.concatenate([x1 * cos - x2 * sin, x1 * sin + x2 * cos], -1)


def axial_rope(x, row, col):
    half = x.shape[-1] // 2
    return jnp.concatenate([rope_1d(x[..., :half], row), rope_1d(x[..., half:], col)], -1).astype(x.dtype)


def split_heads(t, n_heads):
    b, n, _ = t.shape
    return t.reshape(b, n, n_heads, -1).transpose(0, 2, 1, 3)


def merge_heads(t):
    b, h, n, d = t.shape
    return t.transpose(0, 2, 1, 3).reshape(b, n, h * d)


def block_attention(q, k, v, scale):
    b, hk, g, n, dh = q.shape
    nb = n // Q_BLOCK
    qb = jnp.moveaxis(q.reshape(b, hk, g, nb, Q_BLOCK, dh), 3, 0)

    def one_block(qi):
        s = jnp.einsum('bhgqd,bhkd->bhgqk', qi, k).astype(F32) * scale
        p = jax.nn.softmax(s, axis=-1)
        return jnp.einsum('bhgqk,bhkd->bhgqd', p.astype(v.dtype), v)

    o = lax.map(one_block, qb)
    return jnp.moveaxis(o, 0, 3).reshape(b, hk, g, n, v.shape[-1])


def s5_discretise(a_re, a_im, log_dt, b_re, b_im):
    dt = jnp.exp(log_dt.astype(F32))[:, None]
    a_re = a_re.astype(F32)
    a_im = a_im.astype(F32)
    mag = jnp.exp(a_re * dt)
    ab_re, ab_im = mag * jnp.cos(a_im * dt), mag * jnp.sin(a_im * dt)
    den = a_re * a_re + a_im * a_im
    num_re, num_im = ab_re - 1.0, ab_im
    coef_re = (num_re * a_re + num_im * a_im) / den
    coef_im = (num_im * a_re - num_re * a_im) / den
    b_re = b_re.astype(F32)
    b_im = b_im.astype(F32)
    bb_re = coef_re[..., None] * b_re - coef_im[..., None] * b_im
    bb_im = coef_re[..., None] * b_im + coef_im[..., None] * b_re
    return ab_re, ab_im, bb_re, bb_im


def complex_affine_combine(left, right):
    la_re, la_im, lb_re, lb_im = left
    ra_re, ra_im, rb_re, rb_im = right
    return (la_re * ra_re - la_im * ra_im,
            la_re * ra_im + la_im * ra_re,
            ra_re * lb_re - ra_im * lb_im + rb_re,
            ra_re * lb_im + ra_im * lb_re + rb_im)


def s5_scan(u, ab_re, ab_im, bb_re, bb_im, h0_re, h0_im, reverse):
    if reverse:
        u = jnp.flip(u, 1)
    uf = u.astype(F32)
    b_re = jnp.einsum('bngi,gpi->bngp', uf, bb_re)
    b_im = jnp.einsum('bngi,gpi->bngp', uf, bb_im)
    b_re = b_re.at[:, 0].add(ab_re * h0_re - ab_im * h0_im)
    b_im = b_im.at[:, 0].add(ab_re * h0_im + ab_im * h0_re)
    a_re = jnp.broadcast_to(ab_re, b_re.shape)
    a_im = jnp.broadcast_to(ab_im, b_im.shape)
    _, _, h_re, h_im = lax.associative_scan(complex_affine_combine, (a_re, a_im, b_re, b_im), axis=1)
    if reverse:
        h_re, h_im = jnp.flip(h_re, 1), jnp.flip(h_im, 1)
    return h_re, h_im


def s5_readout(h_re, h_im, u, lp):
    b, n, w = u.shape
    y = (jnp.einsum('bngp,gip->bngi', h_re, lp['s5_c_re'].astype(F32))
         - jnp.einsum('bngp,gip->bngi', h_im, lp['s5_c_im'].astype(F32))).reshape(b, n, w)
    y = (y + lp['s5_d'].astype(F32) * u.astype(F32)).astype(u.dtype)
    z = jax.nn.gelu(y)
    return z * jax.nn.sigmoid(z @ lp['s5_w_glu'])


def s5_branch(u, uc, lp, need_ctx):
    b, n, _ = u.shape
    bc, m, _ = uc.shape
    ug = u.reshape(b, n, S5_GROUPS, S5_GROUP_CH)
    ucg = uc.reshape(bc, m, S5_GROUPS, S5_GROUP_CH)
    zero = jnp.zeros((bc, S5_GROUPS, S5_STATE), F32)
    lat_re = lat_im = ctx_re = ctx_im = 0.0
    for sfx, reverse in (('f', False), ('b', True)):
        ab_re, ab_im, bb_re, bb_im = s5_discretise(lp['s5_a_re_' + sfx], lp['s5_a_im_' + sfx],
                                                   lp['s5_log_dt_' + sfx], lp['s5_b_re'], lp['s5_b_im'])
        hc_re, hc_im = s5_scan(ucg, ab_re, ab_im, bb_re, bb_im, zero, zero, reverse)
        end = 0 if reverse else -1
        hl_re, hl_im = s5_scan(ug, ab_re, ab_im, bb_re, bb_im, hc_re[:, end], hc_im[:, end], reverse)
        lat_re = lat_re + hl_re
        lat_im = lat_im + hl_im
        if need_ctx:
            ctx_re = ctx_re + hc_re
            ctx_im = ctx_im + hc_im
    y = s5_readout(lat_re, lat_im, u, lp)
    yc = s5_readout(ctx_re, ctx_im, uc, lp) if need_ctx else None
    return y, yc


def gqa_branch(pl, pc, lp, row, col, need_ctx):
    q, k, v = pl
    qc, kc, vc = pc
    n_grp = GQA_HEADS // GQA_KV_HEADS
    scale = GQA_HEAD_DIM ** -0.5

    def attend(qh, kh, vh):
        b, _, n, d = qh.shape
        o = block_attention(qh.reshape(b, GQA_KV_HEADS, n_grp, n, d), kh, vh, scale)
        return merge_heads(o.reshape(b, GQA_HEADS, n, vh.shape[-1]))

    qn, kn = lp['gqa_q_norm'], lp['gqa_k_norm']
    k_ctx = rms_norm(split_heads(kc, GQA_KV_HEADS), kn)
    v_ctx = split_heads(vc, GQA_KV_HEADS)
    q_lat = axial_rope(rms_norm(split_heads(q, GQA_HEADS), qn), row, col)
    k_lat = axial_rope(rms_norm(split_heads(k, GQA_KV_HEADS), kn), row, col)
    v_lat = split_heads(v, GQA_KV_HEADS)
    y = attend(q_lat, jnp.concatenate([k_lat, k_ctx], 2), jnp.concatenate([v_lat, v_ctx], 2))
    yc = attend(rms_norm(split_heads(qc, GQA_HEADS), qn), k_ctx, v_ctx) if need_ctx else None
    return y, yc


def retention_chunkwise(q, k, v, log_gamma, s0, strict):
    b, h, n, _ = q.shape
    dv = v.shape[-1]
    nc = n // RET_CHUNK
    pos = jnp.arange(RET_CHUNK, dtype=F32)
    diff = pos[:, None] - pos[None, :]
    mask = (diff > 0) if strict else (diff >= 0)
    lg = log_gamma[:, None, None]
    decay_in = jnp.where(mask, jnp.exp(lg * jnp.where(mask, diff, 0.0)), 0.0)
    decay_q = jnp.exp(log_gamma[:, None] * (pos + 1.0))[..., None]
    decay_k = jnp.exp(log_gamma[:, None] * (RET_CHUNK - 1.0 - pos))[..., None]
    decay_c = jnp.exp(lg * RET_CHUNK)

    def chunks(t):
        return jnp.moveaxis(t.astype(F32).reshape(b, h, nc, RET_CHUNK, t.shape[-1]), 2, 0)

    def step(s, qkv):
        qi, ki, vi = qkv
        scores = jnp.einsum('bhqd,bhkd->bhqk', qi, ki) * decay_in
        o = (jnp.einsum('bhqk,bhkv->bhqv', scores, vi)
             + jnp.einsum('bhqd,bhdv->bhqv', qi * decay_q, s))
        s_new = s * decay_c + jnp.einsum('bhkd,bhkv->bhdv', ki * decay_k, vi)
        return s_new, o

    s_fin, o = lax.scan(step, s0, (chunks(q), chunks(k), chunks(v)))
    return jnp.moveaxis(o, 0, 2).reshape(b, h, n, dv), s_fin


def retention_state(k, v, log_gamma):
    n = k.shape[2]
    w = jnp.exp(log_gamma[:, None] * (n - 1.0 - jnp.arange(n, dtype=F32)))
    return jnp.einsum('bhnd,hn,bhnv->bhdv', k.astype(F32), w, v.astype(F32))


def retention_readout(o, g, gain):
    oc = o - jnp.mean(o, -1, keepdims=True)
    on = oc * lax.rsqrt(jnp.mean(oc * oc, -1, keepdims=True) + NORM_EPS)
    return (merge_heads(on) * gain * jax.nn.silu(g.astype(F32))).astype(g.dtype)


def flip_seq(t):
    return jnp.flip(t, 2)


def retention_branch(pl, pc, lp, row, col, need_ctx):
    q, k, v, g = pl
    qc, kc, vc, gc = pc
    sk = RET_QK_DIM ** -0.5
    lg_f = -jnp.exp(lp['ret_decay_f'].astype(F32))
    lg_b = -jnp.exp(lp['ret_decay_b'].astype(F32))
    q_lat = axial_rope(split_heads(q, RET_HEADS), row, col)
    k_lat = axial_rope(split_heads(k, RET_HEADS), row, col) * sk
    v_lat = split_heads(v, RET_HEADS)
    k_ctx = split_heads(kc, RET_HEADS) * sk
    v_ctx = split_heads(vc, RET_HEADS)
    if need_ctx:
        q_ctx = split_heads(qc, RET_HEADS)
        zero = jnp.zeros((qc.shape[0], RET_HEADS, RET_QK_DIM, RET_V_DIM), F32)
        oc_f, s_f = retention_chunkwise(q_ctx, k_ctx, v_ctx, lg_f, zero, False)
        oc_b, s_b = retention_chunkwise(flip_seq(q_ctx), flip_seq(k_ctx), flip_seq(v_ctx), lg_b, zero, True)
        yc = retention_readout(oc_f + flip_seq(oc_b), gc, lp['ret_norm'])
    else:
        s_f = retention_state(k_ctx, v_ctx, lg_f)
        s_b = retention_state(flip_seq(k_ctx), flip_seq(v_ctx), lg_b)
        yc = None
    o_f, _ = retention_chunkwise(q_lat, k_lat, v_lat, lg_f, s_f, False)
    o_b, _ = retention_chunkwise(flip_seq(q_lat), flip_seq(k_lat), flip_seq(v_lat), lg_b, s_b, True)
    y = retention_readout(o_f + flip_seq(o_b), g, lp['ret_norm'])
    return y, yc


def mla_branch(pl, pc, lp, row, col, need_ctx):
    cq, ckv, kr = pl
    cqc, ckvc, krc = pc
    scale = (MLA_NOPE_DIM + MLA_ROPE_DIM) ** -0.5

    def queries(c_q, rotate):
        qh = split_heads(rms_norm(c_q, lp['mla_q_norm']) @ lp['mla_w_uq'], MLA_HEADS)
        q_nope, q_rope = qh[..., :MLA_NOPE_DIM], qh[..., MLA_NOPE_DIM:]
        if rotate:
            q_rope = axial_rope(q_rope, row, col)
        return jnp.concatenate([q_nope, q_rope], -1)[:, :, None]

    def keys_values(c_kv, k_r, rotate):
        kvh = split_heads(rms_norm(c_kv, lp['mla_kv_norm']) @ lp['mla_w_ukv'], MLA_HEADS)
        k_nope, v = kvh[..., :MLA_NOPE_DIM], kvh[..., MLA_NOPE_DIM:]
        k_rope = k_r[:, None]
        if rotate:
            k_rope = axial_rope(k_rope, row, col)
        k_rope = jnp.broadcast_to(k_rope, k_nope.shape[:-1] + (MLA_ROPE_DIM,))
        return jnp.concatenate([k_nope, k_rope], -1), v

    k_ctx, v_ctx = keys_values(ckvc, krc, False)
    k_lat, v_lat = keys_values(ckv, kr, True)
    y = merge_heads(block_attention(queries(cq, True), jnp.concatenate([k_lat, k_ctx], 2),
                                    jnp.concatenate([v_lat, v_ctx], 2), scale)[:, :, 0])
    yc = merge_heads(block_attention(queries(cqc, False), k_ctx, v_ctx, scale)[:, :, 0]) if need_ctx else None
    return y, yc


def merge_branches(h, outs, lp):
    o = jnp.stack(outs, 0)
    proj = jnp.einsum('kbnw,kwd->bnkd', o, lp['w_branch'])
    gates = jax.nn.sigmoid(h @ lp['w_gate'] + lp['b_gate']).reshape(proj.shape)
    return jnp.sum(gates * proj, axis=2) @ lp['w_out']


def token_mixer(h, hc, row, col, lp, need_ctx):
    pl = jnp.split(h @ lp['w_in'], IN_OFFSETS, axis=-1)
    pc = jnp.split(hc @ lp['w_in'], IN_OFFSETS, axis=-1)
    o_s5, oc_s5 = s5_branch(pl[0], pc[0], lp, need_ctx)
    o_gqa, oc_gqa = gqa_branch(pl[1:4], pc[1:4], lp, row, col, need_ctx)
    o_ret, oc_ret = retention_branch(pl[4:8], pc[4:8], lp, row, col, need_ctx)
    o_mla, oc_mla = mla_branch(pl[8:11], pc[8:11], lp, row, col, need_ctx)
    y = merge_branches(h, (o_s5, o_gqa, o_ret, o_mla), lp)
    yc = merge_branches(hc, (oc_s5, oc_gqa, oc_ret, oc_mla), lp) if need_ctx else None
    return y, yc


def expert_choice_ffn(h, lp):
    b, n, d = h.shape
    cap = EC_CAPACITY_FACTOR * n // N_EXPERTS
    aff = jax.nn.softmax((h @ lp['router_w']).astype(F32), axis=-1)
    gate, idx = lax.top_k(jnp.swapaxes(aff, 1, 2), cap)
    xs = jax.vmap(lambda hb, ib: hb[ib])(h, idx)
    a = jnp.einsum('becd,edf->becf', xs, lp['moe_w_gate'])
    u = jnp.einsum('becd,edf->becf', xs, lp['moe_w_up'])
    y = jnp.einsum('becf,efd->becd', jax.nn.silu(a) * u, lp['moe_w_down']) * gate[..., None].astype(h.dtype)
    return jax.vmap(lambda yb, ib: jnp.zeros((n, d), yb.dtype).at[ib.reshape(-1)].add(yb.reshape(-1, d)))(y, idx)


def setup_inputs(seed: int = 0) -> dict:
    key = jax.random.key(seed)
    keys = iter(jax.random.split(key, 48))
    L, D = DEPTH, D_MODEL
    G, P = S5_GROUPS, S5_STATE

    def nrm(shape, std):
        return std * jax.random.normal(next(keys), shape, F32)

    s5_n = jnp.arange(P, dtype=F32)
    ret_h = jnp.arange(RET_HEADS, dtype=F32)
    ret_decay0 = jnp.log(-jnp.log(1.0 - 2.0 ** (-5.0 - ret_h)))
    dt_lo, dt_hi = math.log(1e-3), math.log(1e-1)
    qk_mla = MLA_NOPE_DIM + MLA_ROPE_DIM
    return {
        'x': nrm((BATCH, SEQ, D), 1.0),
        'c': nrm((BATCH, D), 1.0),
        'ctx': nrm((BATCH, CTX_LEN, D), 1.0),
        'c_ctx': nrm((D,), 1.0),
        'ada_w': nrm((L, D, 6 * D), D ** -0.5),
        'ada_b': nrm((L, 6 * D), 0.02),
        'w_in': nrm((L, D, IN_TOTAL), D ** -0.5),
        's5_a_re_f': -0.5 + nrm((L, G, P), 0.01),
        's5_a_im_f': jnp.pi * s5_n + nrm((L, G, P), 0.01),
        's5_log_dt_f': jax.random.uniform(next(keys), (L, G), F32, dt_lo, dt_hi),
        's5_a_re_b': -0.5 + nrm((L, G, P), 0.01),
        's5_a_im_b': jnp.pi * s5_n + nrm((L, G, P), 0.01),
        's5_log_dt_b': jax.random.uniform(next(keys), (L, G), F32, dt_lo, dt_hi),
        's5_b_re': nrm((L, G, P, S5_GROUP_CH), (2 * S5_GROUP_CH) ** -0.5),
        's5_b_im': nrm((L, G, P, S5_GROUP_CH), (2 * S5_GROUP_CH) ** -0.5),
        's5_c_re': nrm((L, G, S5_GROUP_CH, P), P ** -0.5),
        's5_c_im': nrm((L, G, S5_GROUP_CH, P), P ** -0.5),
        's5_d': nrm((L, BRANCH_W), 1.0),
        's5_w_glu': nrm((L, BRANCH_W, BRANCH_W), BRANCH_W ** -0.5),
        'gqa_q_norm': 1.0 + nrm((L, GQA_HEAD_DIM), 0.02),
        'gqa_k_norm': 1.0 + nrm((L, GQA_HEAD_DIM), 0.02),
        'ret_decay_f': ret_decay0 + nrm((L, RET_HEADS), 0.05),
        'ret_decay_b': ret_decay0 + nrm((L, RET_HEADS), 0.05),
        'ret_norm': 1.0 + nrm((L, RET_HEADS * RET_V_DIM), 0.02),
        'mla_q_norm': 1.0 + nrm((L, MLA_Q_LORA), 0.02),
        'mla_kv_norm': 1.0 + nrm((L, MLA_KV_LORA), 0.02),
        'mla_w_uq': nrm((L, MLA_Q_LORA, MLA_HEADS * qk_mla), MLA_Q_LORA ** -0.5),
        'mla_w_ukv': nrm((L, MLA_KV_LORA, MLA_HEADS * (MLA_NOPE_DIM + MLA_V_DIM)), MLA_KV_LORA ** -0.5),
        'w_branch': nrm((L, N_BRANCH, BRANCH_W, D), BRANCH_W ** -0.5),
        'w_gate': nrm((L, D, N_BRANCH * D), D ** -0.5),
        'b_gate': nrm((L, N_BRANCH * D), 0.02),
        'w_out': nrm((L, D, D), DEEPNORM_BETA * D ** -0.5),
        'ln1_g': 1.0 + nrm((L, D), 0.02),
        'ln1_b': nrm((L, D), 0.02),
        'router_w': nrm((L, D, N_EXPERTS), D ** -0.5),
        'moe_w_gate': nrm((L, N_EXPERTS, D, EXPERT_FF), D ** -0.5),
        'moe_w_up': nrm((L, N_EXPERTS, D, EXPERT_FF), D ** -0.5),
        'moe_w_down': nrm((L, N_EXPERTS, EXPERT_FF, D), DEEPNORM_BETA * EXPERT_FF ** -0.5),
        'ln2_g': 1.0 + nrm((L, D), 0.02),
        'ln2_b': nrm((L, D), 0.02),
    }


def reference(x, c, ctx, c_ctx, ada_w, ada_b, w_in,
              s5_a_re_f, s5_a_im_f, s5_log_dt_f, s5_a_re_b, s5_a_im_b, s5_log_dt_b,
              s5_b_re, s5_b_im, s5_c_re, s5_c_im, s5_d, s5_w_glu,
              gqa_q_norm, gqa_k_norm, ret_decay_f, ret_decay_b, ret_norm,
              mla_q_norm, mla_kv_norm, mla_w_uq, mla_w_ukv,
              w_branch, w_gate, b_gate, w_out, ln1_g, ln1_b,
              router_w, moe_w_gate, moe_w_up, moe_w_down, ln2_g, ln2_b):
    n_lat = x.shape[1]
    rows = n_lat // GRID_W
    row = jnp.repeat(jnp.arange(rows, dtype=jnp.int32), GRID_W)
    col = jnp.tile(jnp.arange(GRID_W, dtype=jnp.int32), rows)
    cx = ctx
    for l in range(DEPTH):
        need_ctx = l < DEPTH - 1
        lp = {
            'w_in': w_in[l],
            's5_a_re_f': s5_a_re_f[l], 's5_a_im_f': s5_a_im_f[l], 's5_log_dt_f': s5_log_dt_f[l],
            's5_a_re_b': s5_a_re_b[l], 's5_a_im_b': s5_a_im_b[l], 's5_log_dt_b': s5_log_dt_b[l],
            's5_b_re': s5_b_re[l], 's5_b_im': s5_b_im[l], 's5_c_re': s5_c_re[l], 's5_c_im': s5_c_im[l],
            's5_d': s5_d[l], 's5_w_glu': s5_w_glu[l],
            'gqa_q_norm': gqa_q_norm[l], 'gqa_k_norm': gqa_k_norm[l],
            'ret_decay_f': ret_decay_f[l], 'ret_decay_b': ret_decay_b[l], 'ret_norm': ret_norm[l],
            'mla_q_norm': mla_q_norm[l], 'mla_kv_norm': mla_kv_norm[l],
            'mla_w_uq': mla_w_uq[l], 'mla_w_ukv': mla_w_ukv[l],
            'w_branch': w_branch[l], 'w_gate': w_gate[l], 'b_gate': b_gate[l], 'w_out': w_out[l],
            'router_w': router_w[l], 'moe_w_gate': moe_w_gate[l], 'moe_w_up': moe_w_up[l],
            'moe_w_down': moe_w_down[l],
        }
        mod = jax.nn.silu(c) @ ada_w[l] + ada_b[l]
        mod_c = jax.nn.silu(c_ctx) @ ada_w[l] + ada_b[l]
        sh1, sc1, g1, sh2, sc2, g2 = jnp.split(mod[:, None, :], 6, axis=-1)
        shc1, scc1, gc1, shc2, scc2, gc2 = jnp.split(mod_c, 6)
        h = modulate(x, sh1, sc1)
        hc = modulate(cx, shc1, scc1)
        y, yc = token_mixer(h, hc, row, col, lp, need_ctx)
        x = layer_norm(DEEPNORM_ALPHA * x + g1 * y, ln1_g[l], ln1_b[l])
        h = modulate(x, sh2, sc2)
        x = layer_norm(DEEPNORM_ALPHA * x + g2 * expert_choice_ffn(h, lp), ln2_g[l], ln2_b[l])
        if need_ctx:
            cx = layer_norm(DEEPNORM_ALPHA * cx + gc1 * yc, ln1_g[l], ln1_b[l])
            hc = modulate(cx, shc2, scc2)
            cx = layer_norm(DEEPNORM_ALPHA * cx + gc2 * expert_choice_ffn(hc, lp), ln2_g[l], ln2_b[l])
    return x
```

```python
import functools
import math

import numpy as np
import jax
import jax.numpy as jnp
from jax import lax
from jax.experimental import pallas as pl
from jax.experimental.pallas import tpu as pltpu

F32 = jnp.float32
BF16 = jnp.bfloat16

D_MODEL = 2048
BATCH = 2
SEQ = 4096
DEPTH = 2
GRID_W = 64
CTX_LEN = 256
S_ALL = SEQ + CTX_LEN
N_BRANCH = 4
BRANCH_W = D_MODEL // 4
S5_GROUP_CH = 16
S5_GROUPS = BRANCH_W // S5_GROUP_CH
S5_STATE = 64
S5_MODES = S5_GROUPS * S5_STATE
GQA_HEAD_DIM = 128
GQA_HEADS = 4
GQA_KV_HEADS = 2
RET_HEADS = 4
RET_V_DIM = 128
RET_QK_DIM = 64
MLA_HEADS = 4
MLA_Q_LORA = 512
MLA_KV_LORA = 256
MLA_NOPE_DIM = 128
MLA_ROPE_DIM = 64
MLA_V_DIM = 128
MLA_QK_PAD = 256
N_EXPERTS = 16
EXPERT_FF = D_MODEL // 2
EC_CAPACITY_FACTOR = 2
ROPE_BASE = 10000.0
NORM_EPS = 1e-6
DEEPNORM_ALPHA = (2 * DEPTH) ** 0.25
IN_WIDTHS = (512, 512, 256, 256, 256, 256, 512, 512, 512, 256, 64)
IN_TOTAL = sum(IN_WIDTHS)
IN_PAD = 4096
(OFF_U, OFF_GQ, OFF_GK, OFF_GV, OFF_RQ, OFF_RK, OFF_RV, OFF_RG,
 OFF_CQ, OFF_CKV, OFF_KR) = (int(v) for v in np.concatenate([[0], np.cumsum(IN_WIDTHS)[:-1]]))

ROW_TILE = 256
BLOCKS_PER_SAMPLE = S_ALL // ROW_TILE
LAT_BLOCKS = SEQ // ROW_TILE
S5_SEG = 8
S5_STEPS = ROW_TILE // S5_SEG
RET_CHUNK = 256
VMEM_LIMIT = 56 * 1024 * 1024


def _cp(sem, vmem=VMEM_LIMIT):
    return pltpu.CompilerParams(dimension_semantics=sem, vmem_limit_bytes=vmem)


def _standardize(x):
    xc = x - jnp.mean(x, -1, keepdims=True)
    return xc * lax.rsqrt(jnp.mean(xc * xc, -1, keepdims=True) + NORM_EPS)


def _sigmoid(x):
    return 1.0 / (1.0 + jnp.exp(-x))


def _ada_kernel(c_ref, w_ref, b_ref, o_ref):
    c = c_ref[...]
    cs = (c * _sigmoid(c)).astype(BF16)
    o_ref[0] = jnp.dot(cs, w_ref[0].astype(BF16), preferred_element_type=F32) + b_ref[0]


def ada_modulation(cc, ada_w, ada_b, tn=1024):
    L, D, N = ada_w.shape
    return pl.pallas_call(
        _ada_kernel,
        grid=(L, N // tn),
        in_specs=[pl.BlockSpec((16, D), lambda l, n: (0, 0)),
                  pl.BlockSpec((1, D, tn), lambda l, n: (l, 0, n)),
                  pl.BlockSpec((1, 1, tn), lambda l, n: (l, 0, n))],
        out_specs=pl.BlockSpec((1, 16, tn), lambda l, n: (l, 0, n)),
        out_shape=jax.ShapeDtypeStruct((L, 16, N), F32),
        compiler_params=_cp(("arbitrary", "arbitrary")),
        name="ada_modulation",
    )(cc, ada_w, ada_b.reshape(L, 1, N))


def _modulate_kernel(x_ref, sh_ref, sc_ref, o_ref):
    o_ref[...] = (_standardize(x_ref[...]) * (1.0 + sc_ref[0]) + sh_ref[0]).astype(o_ref.dtype)


def modulate_rows(x, modblk, shift_part, scale_part):
    R, D = x.shape
    return pl.pallas_call(
        _modulate_kernel,
        grid=(R // ROW_TILE,),
        in_specs=[pl.BlockSpec((ROW_TILE, D), lambda i: (i, 0)),
                  pl.BlockSpec((1, 1, D), lambda i: (i, 0, shift_part)),
                  pl.BlockSpec((1, 1, D), lambda i: (i, 0, scale_part))],
        out_specs=pl.BlockSpec((ROW_TILE, D), lambda i: (i, 0)),
        out_shape=jax.ShapeDtypeStruct((R, D), BF16),
        compiler_params=_cp(("parallel",)),
        name="modulate_rows",
    )(x, modblk, modblk)


def _mm_kernel(x_ref, w_ref, o_ref):
    o_ref[...] = jnp.dot(x_ref[...], w_ref[...], preferred_element_type=F32).astype(o_ref.dtype)


def matmul_bf16(x, w, tn, out_dtype=F32, tm=ROW_TILE, name="matmul_bf16"):
    R, K = x.shape
    N = w.shape[1]
    return pl.pallas_call(
        _mm_kernel,
        grid=(N // tn, R // tm),
        in_specs=[pl.BlockSpec((tm, K), lambda n, m: (m, 0)),
                  pl.BlockSpec((K, tn), lambda n, m: (0, n))],
        out_specs=pl.BlockSpec((tm, tn), lambda n, m: (m, n)),
        out_shape=jax.ShapeDtypeStruct((R, N), out_dtype),
        compiler_params=_cp(("arbitrary", "arbitrary")),
        name=name,
    )(x, w)


def _rope_tables(head_dim, width):
    half = head_dim // 2
    n = half // 2
    inv = ROPE_BASE ** (-np.arange(n, dtype=np.float64) / n)
    t = np.arange(SEQ)
    pos = np.stack([t // GRID_W, t % GRID_W], 0).astype(np.float64)
    lane = np.arange(head_dim)
    which = lane // half
    m = lane % half
    ang = (pos[which, :].T.astype(np.float32) * inv[m % n].astype(np.float32)[None, :]).astype(np.float64)
    cos = np.cos(ang)
    sin = np.where(m < n, -np.sin(ang), np.sin(ang))
    cos = np.concatenate([cos, np.ones((CTX_LEN, head_dim))], 0)
    sin = np.concatenate([sin, np.zeros((CTX_LEN, head_dim))], 0)
    reps = width // head_dim
    return (np.tile(cos, (1, reps)).astype(np.float32), np.tile(sin, (1, reps)).astype(np.float32))


def _mla_q_tables():
    cos64, sin64 = _rope_tables(MLA_ROPE_DIM, MLA_ROPE_DIM)
    ones = np.ones((S_ALL, MLA_NOPE_DIM), np.float32)
    zeros = np.zeros((S_ALL, MLA_NOPE_DIM), np.float32)
    pad1 = np.ones((S_ALL, MLA_QK_PAD - MLA_NOPE_DIM - MLA_ROPE_DIM), np.float32)
    cos = np.concatenate([ones, cos64, pad1], 1)
    sin = np.concatenate([zeros, sin64, 0 * pad1], 1)
    return cos, sin


def _rope(x, cos, sin, quarter):
    w = x.shape[-1]
    lane = lax.broadcasted_iota(jnp.int32, x.shape, 1)
    first = (lane % (2 * quarter)) < quarter
    partner = jnp.where(first, pltpu.roll(x, w - quarter, 1), pltpu.roll(x, quarter, 1))
    return x * cos + partner * sin


def _rms_heads(x, gain, head_dim):
    outs = []
    for h in range(x.shape[-1] // head_dim):
        xh = x[:, h * head_dim:(h + 1) * head_dim]
        outs.append(xh * lax.rsqrt(jnp.mean(xh * xh, -1, keepdims=True) + NORM_EPS) * gain)
    return outs[0] if len(outs) == 1 else jnp.concatenate(outs, -1)


def _prep_kernel(p_ref, c128_ref, s128_ref, c64_ref, s64_ref, gqn_ref, gkn_ref, mqn_ref, mkvn_ref,
                 gq_ref, gk_ref, gv_ref, rq_ref, rk_ref, rv_ref, cq_ref, ckv_ref, kr_ref):
    c128, s128 = c128_ref[...], s128_ref[...]
    c64, s64 = c64_ref[...], s64_ref[...]
    tile2 = lambda t: jnp.concatenate([t, t], -1)
    q = _rms_heads(p_ref[:, OFF_GQ:OFF_GQ + 512], gqn_ref[...], GQA_HEAD_DIM)
    q = _rope(q, jnp.concatenate([c128] * 4, -1), jnp.concatenate([s128] * 4, -1), GQA_HEAD_DIM // 4)
    gq_ref[...] = (q * (GQA_HEAD_DIM ** -0.5)).astype(BF16)
    k = _rms_heads(p_ref[:, OFF_GK:OFF_GK + 256], gkn_ref[...], GQA_HEAD_DIM)
    gk_ref[...] = _rope(k, tile2(c128), tile2(s128), GQA_HEAD_DIM // 4).astype(BF16)
    gv_ref[...] = p_ref[:, OFF_GV:OFF_GV + 256].astype(BF16)
    rq_ref[...] = _rope(p_ref[:, OFF_RQ:OFF_RQ + 256], tile2(c64), tile2(s64), RET_QK_DIM // 4).astype(BF16)
    rk = _rope(p_ref[:, OFF_RK:OFF_RK + 256], tile2(c64), tile2(s64), RET_QK_DIM // 4)
    rk_ref[...] = (rk * (RET_QK_DIM ** -0.5)).astype(BF16)
    rv_ref[...] = p_ref[:, OFF_RV:OFF_RV + 512].astype(BF16)
    cq_ref[...] = _rms_heads(p_ref[:, OFF_CQ:OFF_CQ + 512], mqn_ref[...], MLA_Q_LORA).astype(BF16)
    ckv_ref[...] = _rms_heads(p_ref[:, OFF_CKV:OFF_CKV + 256], mkvn_ref[...], MLA_KV_LORA).astype(BF16)
    lane = lax.broadcasted_iota(jnp.int32, (ROW_TILE, 128), 1)
    kr = _rope(p_ref[:, OFF_KR:OFF_KR + 128], c64, s64, MLA_ROPE_DIM // 4)
    kr_ref[...] = jnp.where(lane < MLA_ROPE_DIM, kr, 0.0)


def prep_branches(P, gqa_q_norm, gqa_k_norm, mla_q_norm, mla_kv_norm):
    R = P.shape[0]
    c128, s128 = _rope_tables(GQA_HEAD_DIM, 128)
    c64, s64 = _rope_tables(RET_QK_DIM, 128)
    row = lambda w: pl.BlockSpec((ROW_TILE, w), lambda i: (i, 0))
    tab = pl.BlockSpec((ROW_TILE, 128), lambda i: (i % BLOCKS_PER_SAMPLE, 0))
    vec = lambda w: pl.BlockSpec((1, w), lambda i: (0, 0))
    widths = (512, 256, 256, 256, 256, 512, 512, 256)
    return pl.pallas_call(
        _prep_kernel,
        grid=(R // ROW_TILE,),
        in_specs=[row(IN_PAD), tab, tab, tab, tab, vec(128), vec(128), vec(512), vec(256)],
        out_specs=[row(w) for w in widths] + [row(128)],
        out_shape=[jax.ShapeDtypeStruct((R, w), BF16) for w in widths] + [jax.ShapeDtypeStruct((R, 128), F32)],
        compiler_params=_cp(("parallel",)),
        name="prep_branches",
    )(P, jnp.asarray(c128), jnp.asarray(s128), jnp.asarray(c64), jnp.asarray(s64),
      gqa_q_norm.reshape(1, -1), gqa_k_norm.reshape(1, -1), mla_q_norm.reshape(1, -1), mla_kv_norm.reshape(1, -1))


def _mla_up_kernel(cq_ref, ckv_ref, kr_ref, wq_ref, wkv_ref, cos_ref, sin_ref, q_ref, k_ref, v_ref):
    scale = (MLA_NOPE_DIM + MLA_ROPE_DIM) ** -0.5
    q = jnp.dot(cq_ref[...], wq_ref[...], preferred_element_type=F32)
    cos, sin = cos_ref[...], sin_ref[...]
    kv = jnp.dot(ckv_ref[...], wkv_ref[...], preferred_element_type=F32)
    krp = jnp.concatenate([jnp.zeros((ROW_TILE, MLA_NOPE_DIM), F32), kr_ref[...]], -1)
    for h in range(MLA_HEADS):
        sl = slice(h * MLA_QK_PAD, (h + 1) * MLA_QK_PAD)
        q_ref[:, sl] = (_rope(q[:, sl], cos, sin, MLA_ROPE_DIM // 4) * scale).astype(BF16)
        k_ref[:, sl] = (kv[:, sl] + krp).astype(BF16)
    v_ref[...] = kv[:, MLA_HEADS * MLA_QK_PAD:].astype(BF16)


def mla_up(cqn, ckvn, krp, wq_pad, wkv_pad):
    R = cqn.shape[0]
    cos, sin = _mla_q_tables()
    row = lambda w: pl.BlockSpec((ROW_TILE, w), lambda i: (i, 0))
    full = lambda a: pl.BlockSpec(a.shape, lambda i: (0, 0))
    tab = pl.BlockSpec((ROW_TILE, MLA_QK_PAD), lambda i: (i % BLOCKS_PER_SAMPLE, 0))
    hq = MLA_HEADS * MLA_QK_PAD
    return pl.pallas_call(
        _mla_up_kernel,
        grid=(R // ROW_TILE,),
        in_specs=[row(MLA_Q_LORA), row(MLA_KV_LORA), row(128), full(wq_pad), full(wkv_pad), tab, tab],
        out_specs=[row(hq), row(hq), row(MLA_HEADS * MLA_V_DIM)],
        out_shape=[jax.ShapeDtypeStruct((R, hq), BF16), jax.ShapeDtypeStruct((R, hq), BF16),
                   jax.ShapeDtypeStruct((R, MLA_HEADS * MLA_V_DIM), BF16)],
        compiler_params=_cp(("parallel",)),
        name="mla_up",
    )(cqn, ckvn, krp, wq_pad, wkv_pad, jnp.asarray(cos), jnp.asarray(sin))


def _attn_kernel(q_ref, k_ref, v_ref, o_ref, *, groups, dk, dv):
    k = k_ref[0]
    v = v_ref[0]
    for g in range(groups):
        q = q_ref[0, :, g * dk:(g + 1) * dk]
        s = lax.dot_general(q, k, (((1,), (1,)), ((), ())), preferred_element_type=F32)
        p = jnp.exp(s - jnp.max(s, -1, keepdims=True))
        l = jnp.sum(p, -1, keepdims=True)
        o = jnp.dot(p.astype(BF16), v, preferred_element_type=F32) / l
        o_ref[0, :, g * dv:(g + 1) * dv] = o.astype(o_ref.dtype)


def attention(q, k, v, *, kv_heads, groups, dk, dv, q_rows, q_off, kv_rows, kv_off, tq=ROW_TILE):
    B = q.shape[0]
    qb0, kb0 = q_off // tq, kv_off // kv_rows
    return pl.pallas_call(
        functools.partial(_attn_kernel, groups=groups, dk=dk, dv=dv),
        grid=(B, kv_heads, q_rows // tq),
        in_specs=[pl.BlockSpec((1, tq, groups * dk), lambda b, h, i: (b, qb0 + i, h)),
                  pl.BlockSpec((1, kv_rows, dk), lambda b, h, i: (b, kb0, h)),
                  pl.BlockSpec((1, kv_rows, dv), lambda b, h, i: (b, kb0, h))],
        out_specs=pl.BlockSpec((1, tq, groups * dv), lambda b, h, i: (b, i, h)),
        out_shape=jax.ShapeDtypeStruct((B, q_rows, kv_heads * groups * dv), BF16),
        compiler_params=_cp(("parallel", "parallel", "arbitrary")),
        name="attention",
    )(q, k, v)


def _ret_kernel(lgf_ref, lgb_ref, q_ref, kt_ref, v_ref, g_ref, gain_ref, o_ref, sb_ref, *, n_out):
    L = RET_CHUNK
    nc = SEQ // L
    pair = pl.program_id(1)
    r_i = lax.broadcasted_iota(jnp.int32, (L, L), 0)
    c_i = lax.broadcasted_iota(jnp.int32, (L, L), 1)
    diff = (r_i - c_i).astype(F32)
    pos_col = lax.broadcasted_iota(jnp.int32, (L, 1), 0).astype(F32)
    pos_row = lax.broadcasted_iota(jnp.int32, (1, L), 1).astype(F32)
    for j in range(2):
        lgf = lgf_ref[pair * 2 + j]
        lgb = lgb_ref[pair * 2 + j]
        dmat = jnp.where(diff >= 0, jnp.exp(lgf * jnp.maximum(diff, 0.0)), jnp.exp(lgb * jnp.maximum(-diff, 0.0)))
        dq_f = jnp.exp(lgf * (pos_col + 1.0))
        dq_b = jnp.exp(lgb * (L - pos_col))
        dk_f = jnp.exp(lgf * (L - 1.0 - pos_row))
        dk_b = jnp.exp(lgb * pos_row)
        dc_f = jnp.exp(lgf * L)
        dc_b = jnp.exp(lgb * L)
        qs = slice(j * RET_QK_DIM, (j + 1) * RET_QK_DIM)
        vs = slice(j * RET_V_DIM, (j + 1) * RET_V_DIM)

        def chunk(c):
            rows = slice(c * L, (c + 1) * L)
            return q_ref[0, rows, qs], kt_ref[0, qs, rows], v_ref[0, rows, vs]

        def readout(o, c, out_row0):
            oc = o - jnp.mean(o, -1, keepdims=True)
            on = oc * lax.rsqrt(jnp.mean(oc * oc, -1, keepdims=True) + NORM_EPS)
            g = g_ref[0, c * L:(c + 1) * L, vs]
            o_ref[0, out_row0:out_row0 + L, vs] = (on * gain_ref[:, vs] * (g * _sigmoid(g))).astype(o_ref.dtype)

        def intra(q, kt, v):
            sc = jnp.dot(q, kt, preferred_element_type=F32) * dmat
            return jnp.dot(sc.astype(BF16), v, preferred_element_type=F32)

        def state_add(kt, dk, v):
            return jnp.dot((kt.astype(F32) * dk).astype(BF16), v, preferred_element_type=F32)

        qc, ktc, vc = chunk(nc)
        if n_out > SEQ:
            readout(intra(qc, ktc, vc), nc, SEQ)
        s_f = state_add(ktc, dk_f, vc)
        s_b = state_add(ktc, dk_b, vc)
        for c in range(nc - 1, -1, -1):
            sb_ref[c] = s_b
            _, kt, v = chunk(c)
            s_b = s_b * dc_b + state_add(kt, dk_b, v)
        for c in range(nc):
            q, kt, v = chunk(c)
            qf = q.astype(F32)
            o = (intra(q, kt, v)
                 + jnp.dot((qf * dq_f).astype(BF16), s_f.astype(BF16), preferred_element_type=F32)
                 + jnp.dot((qf * dq_b).astype(BF16), sb_ref[c].astype(BF16), preferred_element_type=F32))
            readout(o, c, c * L)
            s_f = s_f * dc_f + state_add(kt, dk_f, v)


def retention(rq, rkt, rv, P3, gain, lgf, lgb, n_out):
    B = rq.shape[0]
    smem = pl.BlockSpec(memory_space=pltpu.SMEM)
    gcol = OFF_RG // 256
    return pl.pallas_call(
        functools.partial(_ret_kernel, n_out=n_out),
        grid=(B, RET_HEADS // 2),
        in_specs=[smem, smem,
                  pl.BlockSpec((1, S_ALL, 128), lambda b, p: (b, 0, p)),
                  pl.BlockSpec((1, 128, S_ALL), lambda b, p: (b, p, 0)),
                  pl.BlockSpec((1, S_ALL, 256), lambda b, p: (b, 0, p)),
                  pl.BlockSpec((1, S_ALL, 256), lambda b, p: (b, 0, gcol + p)),
                  pl.BlockSpec((1, 256), lambda b, p: (0, p))],
        out_specs=pl.BlockSpec((1, n_out, 256), lambda b, p: (b, 0, p)),
        out_shape=jax.ShapeDtypeStruct((B, n_out, RET_HEADS * RET_V_DIM), BF16),
        scratch_shapes=[pltpu.VMEM((SEQ // RET_CHUNK, RET_QK_DIM, RET_V_DIM), F32)],
        compiler_params=_cp(("parallel", "parallel")),
        name="retention",
    )(lgf, lgb, rq, rkt, rv, P3, gain.reshape(1, -1))


def _s5_kernel(u_ref, bb_ref, lam_ref, cm_ref, y_ref, bu_ref, pw_ref, st_ref):
    M = S5_MODES
    chunk = pl.program_id(2)
    lam_re = lam_ref[0, 0]
    lam_im = lam_ref[0, 1]

    @pl.when(chunk == 0)
    def _():
        st_ref[...] = jnp.zeros_like(st_ref)
        p_re, p_im = lam_re, lam_im
        for j in range(S5_STEPS):
            pw_ref[0, j] = p_re
            pw_ref[1, j] = p_im
            p_re, p_im = p_re * lam_re - p_im * lam_im, p_re * lam_im + p_im * lam_re

    bu_ref[...] = jnp.dot(u_ref[0, 0], bb_ref[0], preferred_element_type=F32)

    tile = 512
    for t in range(M // tile):
        re_sl = slice(t * tile, (t + 1) * tile)
        im_sl = slice(M + t * tile, M + (t + 1) * tile)
        lr, li = lam_re[:, re_sl], lam_im[:, re_sl]

        def step(j, carry):
            s_re, s_im = carry
            rows = pl.ds(pl.multiple_of(j * S5_SEG, S5_SEG), S5_SEG)
            n_re = lr * s_re - li * s_im + bu_ref[rows, re_sl]
            n_im = lr * s_im + li * s_re + bu_ref[rows, im_sl]
            bu_ref[rows, re_sl] = n_re
            bu_ref[rows, im_sl] = n_im
            return n_re, n_im

        z = jnp.zeros((S5_SEG, tile), F32)
        lax.fori_loop(0, S5_STEPS, step, (z, z))

    last = slice((S5_STEPS - 1) * S5_SEG, S5_STEPS * S5_SEG)
    e_re, e_im = bu_ref[last, 0:M], bu_ref[last, M:2 * M]
    pl_re, pl_im = pw_ref[0, S5_STEPS - 1][0:1], pw_ref[1, S5_STEPS - 1][0:1]
    c_re, c_im = st_ref[0:1, :], st_ref[1:2, :]
    rows_re, rows_im = [], []
    for k in range(S5_SEG):
        rows_re.append(c_re)
        rows_im.append(c_im)
        c_re, c_im = (e_re[k:k + 1] + pl_re * c_re - pl_im * c_im,
                      e_im[k:k + 1] + pl_re * c_im + pl_im * c_re)
    st_ref[0:1, :] = c_re
    st_ref[1:2, :] = c_im
    car_re = jnp.concatenate(rows_re, 0)
    car_im = jnp.concatenate(rows_im, 0)

    def fix(j, _):
        rows = pl.ds(pl.multiple_of(j * S5_SEG, S5_SEG), S5_SEG)
        p_re, p_im = pw_ref[0, j], pw_ref[1, j]
        bu_ref[rows, 0:M] = bu_ref[rows, 0:M] + p_re * car_re - p_im * car_im
        bu_ref[rows, M:2 * M] = bu_ref[rows, M:2 * M] + p_re * car_im + p_im * car_re
        return 0

    lax.fori_loop(0, S5_STEPS, fix, 0)
    y_ref[0, 0] = jnp.dot(bu_ref[...].astype(BF16), cm_ref[...], preferred_element_type=F32)


def s5_scan(u_dirs, bb, lam, cmat):
    _, B, S, W = u_dirs.shape
    M2 = 2 * S5_MODES
    return pl.pallas_call(
        _s5_kernel,
        grid=(2, B, S // ROW_TILE),
        in_specs=[pl.BlockSpec((1, 1, ROW_TILE, W), lambda d, b, c: (d, b, c, 0)),
                  pl.BlockSpec((1, W, M2), lambda d, b, c: (d, 0, 0)),
                  pl.BlockSpec((1, 2, S5_SEG, S5_MODES), lambda d, b, c: (d, 0, 0, 0)),
                  pl.BlockSpec((M2, W), lambda d, b, c: (0, 0))],
        out_specs=pl.BlockSpec((1, 1, ROW_TILE, W), lambda d, b, c: (d, b, c, 0)),
        out_shape=jax.ShapeDtypeStruct((2, B, S, W), F32),
        scratch_shapes=[pltpu.VMEM((ROW_TILE, M2), F32),
                        pltpu.VMEM((2, S5_STEPS, S5_SEG, S5_MODES), F32),
                        pltpu.VMEM((8, S5_MODES), F32)],
        compiler_params=_cp(("arbitrary", "arbitrary", "arbitrary")),
        name="s5_scan",
    )(u_dirs, bb, lam, cmat)


def _s5_out_kernel(y_ref, p_ref, d_ref, w_ref, o_ref):
    y = y_ref[...] + d_ref[...] * p_ref[...]
    z = 0.5 * y * (1.0 + jnp.tanh(math.sqrt(2.0 / math.pi) * (y + 0.044715 * (y * y * y))))
    gate = _sigmoid(jnp.dot(z.astype(BF16), w_ref[...], preferred_element_type=F32))
    o_ref[...] = (z * gate).astype(o_ref.dtype)


def s5_output(y, P, d, w_glu):
    R = y.shape[0]
    return pl.pallas_call(
        _s5_out_kernel,
        grid=(R // ROW_TILE,),
        in_specs=[pl.BlockSpec((ROW_TILE, 512), lambda i: (i, 0)),
                  pl.BlockSpec((ROW_TILE, 512), lambda i: (i, 0)),
                  pl.BlockSpec((1, 512), lambda i: (0, 0)),
                  pl.BlockSpec((512, 512), lambda i: (0, 0))],
        out_specs=pl.BlockSpec((ROW_TILE, 512), lambda i: (i, 0)),
        out_shape=jax.ShapeDtypeStruct((R, 512), BF16),
        compiler_params=_cp(("parallel",)),
        name="s5_output",
    )(y, P, d.reshape(1, -1), w_glu)


def _merge_kernel(h_ref, o0_ref, o1_ref, o2_ref, o3_ref, g0_ref, g1_ref, g2_ref, g3_ref,
                  bg_ref, wb_ref, out_ref, wg_ref):
    @pl.when(pl.program_id(1) == 0)
    def _():
        for k, g in enumerate((g0_ref, g1_ref, g2_ref, g3_ref)):
            wg_ref[k] = g[...].astype(BF16)

    h = h_ref[...]
    acc = None
    for k, o in enumerate((o0_ref, o1_ref, o2_ref, o3_ref)):
        gate = _sigmoid(jnp.dot(h, wg_ref[k], preferred_element_type=F32) + bg_ref[k])
        term = gate * jnp.dot(o[...], wb_ref[k], preferred_element_type=F32)
        acc = term if acc is None else acc + term
    out_ref[...] = acc.astype(out_ref.dtype)


def merge_branches(h, outs, w_gate, b_gate, w_branch, tn=256):
    R, D = h.shape
    nb = D // tn
    gate_spec = lambda k: pl.BlockSpec((D, tn), lambda n, m: (0, k * nb + n))
    bg = b_gate.reshape(N_BRANCH, 1, D)
    return pl.pallas_call(
        _merge_kernel,
        grid=(nb, R // ROW_TILE),
        in_specs=[pl.BlockSpec((ROW_TILE, D), lambda n, m: (m, 0))]
                 + [pl.BlockSpec((ROW_TILE, BRANCH_W), lambda n, m: (m, 0))] * N_BRANCH
                 + [gate_spec(k) for k in range(N_BRANCH)]
                 + [pl.BlockSpec((N_BRANCH, 1, tn), lambda n, m: (0, 0, n)),
                    pl.BlockSpec((N_BRANCH, BRANCH_W, tn), lambda n, m: (0, 0, n))],
        out_specs=pl.BlockSpec((ROW_TILE, tn), lambda n, m: (m, n)),
        out_shape=jax.ShapeDtypeStruct((R, D), BF16),
        scratch_shapes=[pltpu.VMEM((N_BRANCH, D, tn), BF16)],
        compiler_params=_cp(("arbitrary", "arbitrary")),
        name="merge_branches",
    )(h, *outs, w_gate, w_gate, w_gate, w_gate, bg, w_branch)


def _out_kernel(m_ref, x_ref, w_ref, g1_ref, lg_ref, lb_ref, sh_ref, sc_ref, rw_ref,
                x1_ref, h2_ref, lo_ref):
    y = jnp.dot(m_ref[...], w_ref[...], preferred_element_type=F32)
    x1 = _standardize(DEEPNORM_ALPHA * x_ref[...] + g1_ref[0] * y) * lg_ref[...] + lb_ref[...]
    x1_ref[...] = x1
    h2 = _standardize(x1) * (1.0 + sc_ref[0]) + sh_ref[0]
    h2_ref[...] = h2.astype(BF16)
    lo_ref[...] = jnp.dot(h2, rw_ref[...], preferred_element_type=F32, precision=lax.Precision.HIGHEST)


def out_proj_norm(merged, x, w_out, modblk, ln_g, ln_b, router_pad):
    R, D = x.shape
    row = lambda w: pl.BlockSpec((ROW_TILE, w), lambda i: (i, 0))
    mod = lambda part: pl.BlockSpec((1, 1, D), lambda i: (i, 0, part))
    vec = pl.BlockSpec((1, D), lambda i: (0, 0))
    return pl.pallas_call(
        _out_kernel,
        grid=(R // ROW_TILE,),
        in_specs=[row(D), row(D), pl.BlockSpec((D, D), lambda i: (0, 0)), mod(2), vec, vec, mod(3), mod(4),
                  pl.BlockSpec((D, 128), lambda i: (0, 0))],
        out_specs=[row(D), row(D), row(128)],
        out_shape=[jax.ShapeDtypeStruct((R, D), F32), jax.ShapeDtypeStruct((R, D), BF16),
                   jax.ShapeDtypeStruct((R, 128), F32)],
        compiler_params=_cp(("parallel",)),
        name="out_proj_norm",
    )(merged, x, w_out, modblk, ln_g.reshape(1, -1), ln_b.reshape(1, -1), modblk, modblk, router_pad)


def _expert_kernel(x_ref, wg_ref, wu_ref, wd_ref, o_ref):
    x = x_ref[0]
    a = jnp.dot(x, wg_ref[0].astype(BF16), preferred_element_type=F32)
    u = jnp.dot(x, wu_ref[0].astype(BF16), preferred_element_type=F32)
    hid = (a * _sigmoid(a) * u).astype(BF16)
    part = jnp.dot(hid, wd_ref[0].astype(BF16), preferred_element_type=F32)

    @pl.when(pl.program_id(1) == 0)
    def _():
        o_ref[0] = part

    @pl.when(pl.program_id(1) != 0)
    def _():
        o_ref[0] += part


def expert_ffn(xs, w_gate, w_up, w_down, tf=256):
    E, T, D = xs.shape
    FF = w_gate.shape[-1]
    return pl.pallas_call(
        _expert_kernel,
        grid=(E, FF // tf),
        in_specs=[pl.BlockSpec((1, T, D), lambda e, f: (e, 0, 0)),
                  pl.BlockSpec((1, D, tf), lambda e, f: (e, 0, f)),
                  pl.BlockSpec((1, D, tf), lambda e, f: (e, 0, f)),
                  pl.BlockSpec((1, tf, D), lambda e, f: (e, f, 0))],
        out_specs=pl.BlockSpec((1, T, D), lambda e, f: (e, 0, 0)),
        out_shape=jax.ShapeDtypeStruct((E, T, D), F32),
        compiler_params=_cp(("parallel", "arbitrary")),
        name="expert_ffn",
    )(xs, w_gate, w_up, w_down)


def _post_kernel(x_ref, y_ref, g2_ref, lg_ref, lb_ref, sh_ref, sc_ref, x2_ref, h_ref):
    x2 = _standardize(DEEPNORM_ALPHA * x_ref[...] + g2_ref[0] * y_ref[...]) * lg_ref[...] + lb_ref[...]
    x2_ref[...] = x2
    h_ref[...] = (_standardize(x2) * (1.0 + sc_ref[0]) + sh_ref[0]).astype(BF16)


def post_moe_norm(x1, y, modblk, ln_g, ln_b, modblk_next):
    R, D = x1.shape
    row = pl.BlockSpec((ROW_TILE, D), lambda i: (i, 0))
    mod = lambda part: pl.BlockSpec((1, 1, D), lambda i: (i, 0, part))
    vec = pl.BlockSpec((1, D), lambda i: (0, 0))
    return pl.pallas_call(
        _post_kernel,
        grid=(R // ROW_TILE,),
        in_specs=[row, row, mod(5), vec, vec, mod(0), mod(1)],
        out_specs=[row, row],
        out_shape=[jax.ShapeDtypeStruct((R, D), F32), jax.ShapeDtypeStruct((R, D), BF16)],
        compiler_params=_cp(("parallel",)),
        name="post_moe_norm",
    )(x1, y, modblk, ln_g.reshape(1, -1), ln_b.reshape(1, -1), modblk_next, modblk_next)


def _s5_matrices(a_re, a_im, log_dt, b_re, b_im):
    dt = jnp.exp(log_dt)[:, None]
    mag = jnp.exp(a_re * dt)
    ab_re, ab_im = mag * jnp.cos(a_im * dt), mag * jnp.sin(a_im * dt)
    den = a_re * a_re + a_im * a_im
    num_re, num_im = ab_re - 1.0, ab_im
    coef_re = (num_re * a_re + num_im * a_im) / den
    coef_im = (num_im * a_re - num_re * a_im) / den
    bb_re = coef_re[..., None] * b_re - coef_im[..., None] * b_im
    bb_im = coef_re[..., None] * b_im + coef_im[..., None] * b_re
    eye = jnp.eye(S5_GROUPS, dtype=F32)
    dense = lambda t: jnp.einsum('gpi,gh->gihp', t, eye).reshape(BRANCH_W, S5_MODES)
    lam = jnp.stack([ab_re.reshape(-1), ab_im.reshape(-1)], 0)
    return lam, jnp.concatenate([dense(bb_re), dense(bb_im)], 1)


def _s5_readout_matrix(c_re, c_im):
    eye = jnp.eye(S5_GROUPS, dtype=F32)
    dense = lambda t: jnp.einsum('gip,gh->gphi', t, eye).reshape(S5_MODES, BRANCH_W)
    return jnp.concatenate([dense(c_re), -dense(c_im)], 0)


def _s5_permute(u):
    B, S, W = u.shape
    return u.reshape(B, S // ROW_TILE, S5_SEG, S5_STEPS, W).transpose(0, 1, 3, 2, 4).reshape(B, S, W)


def _s5_unpermute(y):
    B, S, W = y.shape
    return y.reshape(B, S // ROW_TILE, S5_STEPS, S5_SEG, W).transpose(0, 1, 3, 2, 4).reshape(B, S, W)


def _mla_weights(w_uq, w_ukv):
    qk = MLA_NOPE_DIM + MLA_ROPE_DIM
    wq = w_uq.reshape(MLA_Q_LORA, MLA_HEADS, qk)
    wq = jnp.pad(wq, ((0, 0), (0, 0), (0, MLA_QK_PAD - qk))).reshape(MLA_Q_LORA, MLA_HEADS * MLA_QK_PAD)
    wkv = w_ukv.reshape(MLA_KV_LORA, MLA_HEADS, MLA_NOPE_DIM + MLA_V_DIM)
    wk = jnp.pad(wkv[..., :MLA_NOPE_DIM], ((0, 0), (0, 0), (0, MLA_QK_PAD - MLA_NOPE_DIM)))
    wv = wkv[..., MLA_NOPE_DIM:]
    wkv_pad = jnp.concatenate([wk.reshape(MLA_KV_LORA, -1), wv.reshape(MLA_KV_LORA, -1)], 1)
    return wq.astype(BF16), wkv_pad.astype(BF16)


def _route(aff_t, cap):
    return lax.top_k(aff_t, cap)


def kernel(x, c, ctx, c_ctx, ada_w, ada_b, w_in, s5_a_re_f, s5_a_im_f, s5_log_dt_f, s5_a_re_b, s5_a_im_b,
           s5_log_dt_b, s5_b_re, s5_b_im, s5_c_re, s5_c_im, s5_d, s5_w_glu, gqa_q_norm, gqa_k_norm,
           ret_decay_f, ret_decay_b, ret_norm, mla_q_norm, mla_kv_norm, mla_w_uq, mla_w_ukv,
           w_branch, w_gate, b_gate, w_out, ln1_g, ln1_b, router_w, moe_w_gate, moe_w_up, moe_w_down,
           ln2_g, ln2_b):
    B, N, D = x.shape
    R = B * S_ALL
    assert (B, N, D) == (BATCH, SEQ, D_MODEL) and RET_CHUNK == CTX_LEN == ROW_TILE

    cc = jnp.zeros((16, D), F32).at[:B].set(c).at[B].set(c_ctx)
    mod = ada_modulation(cc, ada_w, ada_b)
    sel = np.concatenate([np.r_[np.full(LAT_BLOCKS, b), B] for b in range(B)])
    modblks = [mod[l][sel].reshape(R // ROW_TILE, 1, 6 * D) for l in range(DEPTH)]

    X = jnp.concatenate([x, ctx], 1).reshape(R, D)
    h = modulate_rows(X, modblks[0], 0, 1)

    for l in range(DEPTH):
        need_ctx = l < DEPTH - 1
        modblk = modblks[l]
        w_in_p = jnp.pad(w_in[l], ((0, 0), (0, IN_PAD - IN_TOTAL))).astype(BF16)
        P = matmul_bf16(h, w_in_p, tn=1024, name="in_proj")
        P3 = P.reshape(B, S_ALL, IN_PAD)
        gq, gk, gv, rq, rk, rv, cqn, ckvn, krp = prep_branches(P, gqa_q_norm[l], gqa_k_norm[l],
                                                               mla_q_norm[l], mla_kv_norm[l])
        to3 = lambda t: t.reshape(B, S_ALL, t.shape[-1])

        lam_f, bb_f = _s5_matrices(s5_a_re_f[l], s5_a_im_f[l], s5_log_dt_f[l], s5_b_re[l], s5_b_im[l])
        lam_b, bb_b = _s5_matrices(s5_a_re_b[l], s5_a_im_b[l], s5_log_dt_b[l], s5_b_re[l], s5_b_im[l])
        lam = jnp.broadcast_to(jnp.stack([lam_f, lam_b], 0)[:, :, None, :], (2, 2, S5_SEG, S5_MODES))
        bb = jnp.stack([bb_f, bb_b], 0).astype(BF16)
        cmat = _s5_readout_matrix(s5_c_re[l], s5_c_im[l]).astype(BF16)
        u3 = P3[:, :, :BRANCH_W]
        u_lat, u_ctx = u3[:, :SEQ], u3[:, SEQ:]
        seq_f = jnp.concatenate([u_ctx, u_lat], 1)
        seq_b = jnp.concatenate([u_ctx[:, ::-1], u_lat[:, ::-1]], 1)
        u_dirs = jnp.stack([_s5_permute(seq_f), _s5_permute(seq_b)], 0).astype(BF16)
        y_dirs = s5_scan(u_dirs, bb, lam, cmat)
        y_f, y_b = _s5_unpermute(y_dirs[0]), _s5_unpermute(y_dirs[1])
        y_lat = y_f[:, CTX_LEN:] + y_b[:, CTX_LEN:][:, ::-1]
        y_ctx = y_f[:, :CTX_LEN] + y_b[:, :CTX_LEN][:, ::-1]
        y_s5 = jnp.concatenate([y_lat, y_ctx], 1).reshape(R, BRANCH_W)
        o_s5 = s5_output(y_s5, P, s5_d[l], s5_w_glu[l].astype(BF16))

        gq3, gk3, gv3 = to3(gq), to3(gk), to3(gv)
        att = functools.partial(attention, kv_heads=GQA_KV_HEADS, groups=GQA_HEADS // GQA_KV_HEADS,
                                dk=GQA_HEAD_DIM, dv=GQA_HEAD_DIM)
        o_lat = att(gq3, gk3, gv3, q_rows=SEQ, q_off=0, kv_rows=S_ALL, kv_off=0)
        o_ctx = att(gq3, gk3, gv3, q_rows=CTX_LEN, q_off=SEQ, kv_rows=CTX_LEN, kv_off=SEQ)
        o_gqa = jnp.concatenate([o_lat, o_ctx], 1).reshape(R, BRANCH_W)

        lgf = -jnp.exp(ret_decay_f[l])
        lgb = -jnp.exp(ret_decay_b[l])
        rkt = jnp.swapaxes(to3(rk), 1, 2)
        o_ret = retention(to3(rq), rkt, to3(rv), P3, ret_norm[l], lgf, lgb, S_ALL).reshape(R, BRANCH_W)

        wq_pad, wkv_pad = _mla_weights(mla_w_uq[l], mla_w_ukv[l])
        mq, mk, mv = mla_up(cqn, ckvn, krp, wq_pad, wkv_pad)
        matt = functools.partial(attention, kv_heads=MLA_HEADS, groups=1, dk=MLA_QK_PAD, dv=MLA_V_DIM)
        m_lat = matt(to3(mq), to3(mk), to3(mv), q_rows=SEQ, q_off=0, kv_rows=S_ALL, kv_off=0)
        m_ctx = matt(to3(mq), to3(mk), to3(mv), q_rows=CTX_LEN, q_off=SEQ, kv_rows=CTX_LEN, kv_off=SEQ)
        o_mla = jnp.concatenate([m_lat, m_ctx], 1).reshape(R, BRANCH_W)

        merged = merge_branches(h, (o_s5, o_gqa, o_ret, o_mla), w_gate[l], b_gate[l], w_branch[l].astype(BF16))
        router_pad = jnp.pad(router_w[l], ((0, 0), (0, 128 - N_EXPERTS)))
        x1, h2, logits = out_proj_norm(merged, X, w_out[l].astype(BF16), modblk, ln1_g[l], ln1_b[l], router_pad)

        lg3 = logits.reshape(B, S_ALL, 128)[:, :, :N_EXPERTS]
        h23 = h2.reshape(B, S_ALL, D)
        sets = [(0, SEQ)] + ([(SEQ, CTX_LEN)] if need_ctx else [])
        xs_parts, routes = [], []
        for off, n in sets:
            cap = EC_CAPACITY_FACTOR * n // N_EXPERTS
            aff = jax.nn.softmax(lg3[:, off:off + n], axis=-1)
            gate, idx = _route(jnp.swapaxes(aff, 1, 2), cap)
            xs = jax.vmap(lambda hb, ib: hb[ib])(h23[:, off:off + n], idx)
            xs_parts.append(jnp.swapaxes(xs, 0, 1).reshape(N_EXPERTS, B * cap, D))
            routes.append((off, n, cap, gate, idx))
        ys = expert_ffn(jnp.concatenate(xs_parts, 1), moe_w_gate[l], moe_w_up[l], moe_w_down[l])
        moe = jnp.zeros((B, S_ALL, D), F32)
        t0 = 0
        for off, n, cap, gate, idx in routes:
            y = ys[:, t0:t0 + B * cap].reshape(N_EXPERTS, B, cap, D).swapaxes(0, 1) * gate[..., None]
            t0 += B * cap
            part = jax.vmap(lambda yb, ib: jnp.zeros((n, D), F32).at[ib.reshape(-1)].add(yb.reshape(-1, D)))(y, idx)
            moe = moe.at[:, off:off + n].set(part)
        X, h = post_moe_norm(x1, moe.reshape(R, D), modblk, ln2_g[l], ln2_b[l], modblks[min(l + 1, DEPTH - 1)])

    return X.reshape(B, S_ALL, D)[:, :SEQ]
```

```python
import functools
import math

import numpy as np
import jax
import jax.numpy as jnp
from jax import lax
from jax.experimental import pallas as pl
from jax.experimental.pallas import tpu as pltpu

F32 = jnp.float32
BF16 = jnp.bfloat16

D_MODEL = 2048
BATCH = 2
SEQ = 4096
DEPTH = 2
GRID_W = 64
CTX_LEN = 256
S_ALL = SEQ + CTX_LEN
N_BRANCH = 4
BRANCH_W = D_MODEL // 4
S5_GROUP_CH = 16
S5_GROUPS = BRANCH_W // S5_GROUP_CH
S5_STATE = 64
S5_MODES = S5_GROUPS * S5_STATE
GQA_HEAD_DIM = 128
GQA_HEADS = 4
GQA_KV_HEADS = 2
RET_HEADS = 4
RET_V_DIM = 128
RET_QK_DIM = 64
MLA_HEADS = 4
MLA_Q_LORA = 512
MLA_KV_LORA = 256
MLA_NOPE_DIM = 128
MLA_ROPE_DIM = 64
MLA_V_DIM = 128
MLA_QK_PAD = 256
N_EXPERTS = 16
EXPERT_FF = D_MODEL // 2
EC_CAPACITY_FACTOR = 2
ROPE_BASE = 10000.0
NORM_EPS = 1e-6
LOG2E = math.log2(math.e)
DEEPNORM_ALPHA = (2 * DEPTH) ** 0.25
IN_WIDTHS = (512, 512, 256, 256, 256, 256, 512, 512, 512, 256, 64)
IN_TOTAL = sum(IN_WIDTHS)
IN_PAD = 4096
(OFF_U, OFF_GQ, OFF_GK, OFF_GV, OFF_RQ, OFF_RK, OFF_RV, OFF_RG,
 OFF_CQ, OFF_CKV, OFF_KR) = (int(v) for v in np.concatenate([[0], np.cumsum(IN_WIDTHS)[:-1]]))

LANES = 128
SUBLANES = 8
ROW_TILE = 256
MM_ROW_TILE = 512
ATTN_SUB = 256
BLOCKS_PER_SAMPLE = S_ALL // ROW_TILE
LAT_BLOCKS = SEQ // ROW_TILE
S5_SLABS = 4
S5_SEG = 8
S5_STEPS = ROW_TILE // S5_SEG
RET_CHUNK = 256
VMEM_LIMIT = 56 * 1024 * 1024


def _cp(sem, vmem=VMEM_LIMIT):
    return pltpu.CompilerParams(dimension_semantics=sem, vmem_limit_bytes=vmem)


def _standardize(x):
    xc = x - jnp.mean(x, -1, keepdims=True)
    return xc * lax.rsqrt(jnp.mean(xc * xc, -1, keepdims=True) + NORM_EPS)


def _sigmoid(x):
    return 1.0 / (1.0 + jnp.exp(-x))


def _store_token_tiles(ref, x):
    rows, w = x.shape
    tt = w // LANES
    for j in range(tt):
        ref[pl.ds(j, rows, stride=tt), :] = x[:, j * LANES:(j + 1) * LANES]


def _load_token_tiles(ref, rows, w, dtype=F32):
    tt = w // LANES
    return jnp.concatenate([ref[pl.ds(j, rows, stride=tt), :].astype(dtype) for j in range(tt)], -1)


def _ada_kernel(c_ref, w_ref, b_ref, o_ref):
    c = c_ref[...]
    cs = (c * _sigmoid(c)).astype(BF16)
    o_ref[0] = jnp.dot(cs, w_ref[0].astype(BF16), preferred_element_type=F32) + b_ref[0]


def ada_modulation(cc, ada_w, ada_b, tn=1024):
    L, D, N = ada_w.shape
    return pl.pallas_call(
        _ada_kernel,
        grid=(L, N // tn),
        in_specs=[pl.BlockSpec((16, D), lambda l, n: (0, 0)),
                  pl.BlockSpec((1, D, tn), lambda l, n: (l, 0, n)),
                  pl.BlockSpec((1, 1, tn), lambda l, n: (l, 0, n))],
        out_specs=pl.BlockSpec((1, 16, tn), lambda l, n: (l, 0, n)),
        out_shape=jax.ShapeDtypeStruct((L, 16, N), F32),
        compiler_params=_cp(("arbitrary", "arbitrary")),
        name="ada_modulation",
    )(cc, ada_w, ada_b.reshape(L, 1, N))


def _modulate_kernel(x_ref, sh_ref, sc_ref, o_ref):
    o_ref[...] = (_standardize(x_ref[...]) * (1.0 + sc_ref[0]) + sh_ref[0]).astype(o_ref.dtype)


def modulate_rows(x, modblk, shift_part, scale_part):
    R, D = x.shape
    return pl.pallas_call(
        _modulate_kernel,
        grid=(R // ROW_TILE,),
        in_specs=[pl.BlockSpec((ROW_TILE, D), lambda i: (i, 0)),
                  pl.BlockSpec((1, 1, D), lambda i: (i, 0, shift_part)),
                  pl.BlockSpec((1, 1, D), lambda i: (i, 0, scale_part))],
        out_specs=pl.BlockSpec((ROW_TILE, D), lambda i: (i, 0)),
        out_shape=jax.ShapeDtypeStruct((R, D), BF16),
        compiler_params=_cp(("parallel",)),
        name="modulate_rows",
    )(x, modblk, modblk)


def _mm_kernel(x_ref, w_ref, o_ref):
    o_ref[...] = jnp.dot(x_ref[...], w_ref[...], preferred_element_type=F32).astype(o_ref.dtype)


def matmul_bf16(x, w, tn, out_dtype=F32, tm=MM_ROW_TILE, name="matmul_bf16"):
    R, K = x.shape
    N = w.shape[1]
    return pl.pallas_call(
        _mm_kernel,
        grid=(N // tn, R // tm),
        in_specs=[pl.BlockSpec((tm, K), lambda n, m: (m, 0)),
                  pl.BlockSpec((K, tn), lambda n, m: (0, n))],
        out_specs=pl.BlockSpec((tm, tn), lambda n, m: (m, n)),
        out_shape=jax.ShapeDtypeStruct((R, N), out_dtype),
        compiler_params=_cp(("arbitrary", "arbitrary")),
        name=name,
    )(x, w)


def _rope_tables(head_dim, width):
    half = head_dim // 2
    n = half // 2
    inv = ROPE_BASE ** (-np.arange(n, dtype=np.float64) / n)
    t = np.arange(SEQ)
    pos = np.stack([t // GRID_W, t % GRID_W], 0).astype(np.float64)
    lane = np.arange(head_dim)
    which = lane // half
    m = lane % half
    ang = (pos[which, :].T.astype(np.float32) * inv[m % n].astype(np.float32)[None, :]).astype(np.float64)
    cos = np.cos(ang)
    sin = np.where(m < n, -np.sin(ang), np.sin(ang))
    cos = np.concatenate([cos, np.ones((CTX_LEN, head_dim))], 0)
    sin = np.concatenate([sin, np.zeros((CTX_LEN, head_dim))], 0)
    reps = width // head_dim
    return (np.tile(cos, (1, reps)).astype(np.float32), np.tile(sin, (1, reps)).astype(np.float32))


def _mla_q_tables():
    cos64, sin64 = _rope_tables(MLA_ROPE_DIM, MLA_ROPE_DIM)
    ones = np.ones((S_ALL, MLA_NOPE_DIM), np.float32)
    zeros = np.zeros((S_ALL, MLA_NOPE_DIM), np.float32)
    pad1 = np.ones((S_ALL, MLA_QK_PAD - MLA_NOPE_DIM - MLA_ROPE_DIM), np.float32)
    cos = np.concatenate([ones, cos64, pad1], 1)
    sin = np.concatenate([zeros, sin64, 0 * pad1], 1)
    return cos, sin


def _rope(x, cos, sin, quarter):
    w = x.shape[-1]
    lane = lax.broadcasted_iota(jnp.int32, x.shape, 1)
    first = (lane % (2 * quarter)) < quarter
    partner = jnp.where(first, pltpu.roll(x, w - quarter, 1), pltpu.roll(x, quarter, 1))
    return x * cos + partner * sin


def _rms_heads(x, gain, head_dim):
    outs = []
    for h in range(x.shape[-1] // head_dim):
        xh = x[:, h * head_dim:(h + 1) * head_dim]
        outs.append(xh * lax.rsqrt(jnp.mean(xh * xh, -1, keepdims=True) + NORM_EPS) * gain)
    return outs[0] if len(outs) == 1 else jnp.concatenate(outs, -1)


def _prep_kernel(p_ref, c128_ref, s128_ref, c64_ref, s64_ref, gqn_ref, gkn_ref, mqn_ref, mkvn_ref,
                 gq_ref, gk_ref, gv_ref, rq_ref, rk_ref, rv_ref, cq_ref, ckv_ref, kr_ref):
    c128, s128 = c128_ref[...], s128_ref[...]
    c64, s64 = c64_ref[...], s64_ref[...]
    tile2 = lambda t: jnp.concatenate([t, t], -1)
    q = _rms_heads(p_ref[:, OFF_GQ:OFF_GQ + 512], gqn_ref[...], GQA_HEAD_DIM)
    q = _rope(q, jnp.concatenate([c128] * 4, -1), jnp.concatenate([s128] * 4, -1), GQA_HEAD_DIM // 4)
    gq_ref[...] = (q * (GQA_HEAD_DIM ** -0.5 * LOG2E)).astype(BF16)
    k = _rms_heads(p_ref[:, OFF_GK:OFF_GK + 256], gkn_ref[...], GQA_HEAD_DIM)
    gk_ref[...] = _rope(k, tile2(c128), tile2(s128), GQA_HEAD_DIM // 4).astype(BF16)
    gv_ref[...] = p_ref[:, OFF_GV:OFF_GV + 256].astype(BF16)
    rq_ref[...] = _rope(p_ref[:, OFF_RQ:OFF_RQ + 256], tile2(c64), tile2(s64), RET_QK_DIM // 4).astype(BF16)
    rk = _rope(p_ref[:, OFF_RK:OFF_RK + 256], tile2(c64), tile2(s64), RET_QK_DIM // 4)
    rk_ref[...] = (rk * (RET_QK_DIM ** -0.5)).astype(BF16)
    rv_ref[...] = p_ref[:, OFF_RV:OFF_RV + 512].astype(BF16)
    cq_ref[...] = _rms_heads(p_ref[:, OFF_CQ:OFF_CQ + 512], mqn_ref[...], MLA_Q_LORA).astype(BF16)
    ckv_ref[...] = _rms_heads(p_ref[:, OFF_CKV:OFF_CKV + 256], mkvn_ref[...], MLA_KV_LORA).astype(BF16)
    lane = lax.broadcasted_iota(jnp.int32, (ROW_TILE, 128), 1)
    kr = _rope(p_ref[:, OFF_KR:OFF_KR + 128], c64, s64, MLA_ROPE_DIM // 4)
    kr_ref[...] = jnp.where(lane < MLA_ROPE_DIM, kr, 0.0)


def prep_branches(P, gqa_q_norm, gqa_k_norm, mla_q_norm, mla_kv_norm):
    R = P.shape[0]
    c128, s128 = _rope_tables(GQA_HEAD_DIM, 128)
    c64, s64 = _rope_tables(RET_QK_DIM, 128)
    row = lambda w: pl.BlockSpec((ROW_TILE, w), lambda i: (i, 0))
    tab = pl.BlockSpec((ROW_TILE, 128), lambda i: (i % BLOCKS_PER_SAMPLE, 0))
    vec = lambda w: pl.BlockSpec((1, w), lambda i: (0, 0))
    widths = (512, 256, 256, 256, 256, 512, 512, 256)
    return pl.pallas_call(
        _prep_kernel,
        grid=(R // ROW_TILE,),
        in_specs=[row(IN_PAD), tab, tab, tab, tab, vec(128), vec(128), vec(512), vec(256)],
        out_specs=[row(w) for w in widths] + [row(128)],
        out_shape=[jax.ShapeDtypeStruct((R, w), BF16) for w in widths] + [jax.ShapeDtypeStruct((R, 128), F32)],
        compiler_params=_cp(("parallel",)),
        name="prep_branches",
    )(P, jnp.asarray(c128), jnp.asarray(s128), jnp.asarray(c64), jnp.asarray(s64),
      gqa_q_norm.reshape(1, -1), gqa_k_norm.reshape(1, -1), mla_q_norm.reshape(1, -1), mla_kv_norm.reshape(1, -1))


def _mla_up_kernel(cq_ref, ckv_ref, kr_ref, wq_ref, wkv_ref, cos_ref, sin_ref, q_ref, k_ref, v_ref):
    scale = (MLA_NOPE_DIM + MLA_ROPE_DIM) ** -0.5 * LOG2E
    q = jnp.dot(cq_ref[...], wq_ref[...], preferred_element_type=F32)
    cos, sin = cos_ref[...], sin_ref[...]
    kv = jnp.dot(ckv_ref[...], wkv_ref[...], preferred_element_type=F32)
    krp = jnp.concatenate([jnp.zeros((ROW_TILE, MLA_NOPE_DIM), F32), kr_ref[...]], -1)
    for h in range(MLA_HEADS):
        sl = slice(h * MLA_QK_PAD, (h + 1) * MLA_QK_PAD)
        q_ref[:, sl] = (_rope(q[:, sl], cos, sin, MLA_ROPE_DIM // 4) * scale).astype(BF16)
        k_ref[:, sl] = (kv[:, sl] + krp).astype(BF16)
    v_ref[...] = kv[:, MLA_HEADS * MLA_QK_PAD:].astype(BF16)


def mla_up(cqn, ckvn, krp, wq_pad, wkv_pad):
    R = cqn.shape[0]
    cos, sin = _mla_q_tables()
    row = lambda w: pl.BlockSpec((ROW_TILE, w), lambda i: (i, 0))
    full = lambda a: pl.BlockSpec(a.shape, lambda i: (0, 0))
    tab = pl.BlockSpec((ROW_TILE, MLA_QK_PAD), lambda i: (i % BLOCKS_PER_SAMPLE, 0))
    hq = MLA_HEADS * MLA_QK_PAD
    return pl.pallas_call(
        _mla_up_kernel,
        grid=(R // ROW_TILE,),
        in_specs=[row(MLA_Q_LORA), row(MLA_KV_LORA), row(128), full(wq_pad), full(wkv_pad), tab, tab],
        out_specs=[row(hq), row(hq), row(MLA_HEADS * MLA_V_DIM)],
        out_shape=[jax.ShapeDtypeStruct((R, hq), BF16), jax.ShapeDtypeStruct((R, hq), BF16),
                   jax.ShapeDtypeStruct((R, MLA_HEADS * MLA_V_DIM), BF16)],
        compiler_params=_cp(("parallel",)),
        name="mla_up",
    )(cqn, ckvn, krp, wq_pad, wkv_pad, jnp.asarray(cos), jnp.asarray(sin))


def _attn_kernel(q_ref, k_ref, v_ref, o_ref, *, groups, dk, dv):
    k = k_ref[0]
    v = v_ref[0]
    v1 = jnp.concatenate([v, jnp.ones_like(v)], -1)
    tq = q_ref.shape[1]
    q = jnp.concatenate([q_ref[0, :, g * dk:(g + 1) * dk] for g in range(groups)], 0)
    s = lax.dot_general(q, k, (((1,), (1,)), ((), ())), preferred_element_type=F32)
    for g in range(groups):
        for half in range(tq // ATTN_SUB):
            r0 = g * tq + half * ATTN_SUB
            sh = s[r0:r0 + ATTN_SUB]
            p = jnp.exp2(sh - jnp.max(sh, -1, keepdims=True)).astype(BF16)
            o = jnp.dot(p, v1, preferred_element_type=F32)
            rows = slice(half * ATTN_SUB, (half + 1) * ATTN_SUB)
            o_ref[0, rows, g * dv:(g + 1) * dv] = (o[:, :dv] / o[:, dv:]).astype(o_ref.dtype)


def attention(q, k, v, *, kv_heads, groups, dk, dv, q_rows, q_off, kv_rows, kv_off, tq):
    B = q.shape[0]
    assert tq % ATTN_SUB == 0 and q_off % tq == 0 and q_rows % tq == 0
    qb0, kb0 = q_off // tq, kv_off // kv_rows
    return pl.pallas_call(
        functools.partial(_attn_kernel, groups=groups, dk=dk, dv=dv),
        grid=(B, kv_heads, q_rows // tq),
        in_specs=[pl.BlockSpec((1, tq, groups * dk), lambda b, h, i: (b, qb0 + i, h)),
                  pl.BlockSpec((1, kv_rows, dk), lambda b, h, i: (b, kb0, h)),
                  pl.BlockSpec((1, kv_rows, dv), lambda b, h, i: (b, kb0, h))],
        out_specs=pl.BlockSpec((1, tq, groups * dv), lambda b, h, i: (b, i, h)),
        out_shape=jax.ShapeDtypeStruct((B, q_rows, kv_heads * groups * dv), BF16),
        compiler_params=_cp(("parallel", "parallel", "arbitrary")),
        name="attention",
    )(q, k, v)


def _ret_kernel(lgf_ref, lgb_ref, q_ref, kt_ref, v_ref, g_ref, gain_ref, o_ref, sb_ref, *, n_out):
    L = RET_CHUNK
    nc = SEQ // L
    pair = pl.program_id(1)
    r_i = lax.broadcasted_iota(jnp.int32, (L, L), 0)
    c_i = lax.broadcasted_iota(jnp.int32, (L, L), 1)
    diff = (r_i - c_i).astype(F32)
    pos_col = lax.broadcasted_iota(jnp.int32, (L, 1), 0).astype(F32)
    pos_row = lax.broadcasted_iota(jnp.int32, (1, L), 1).astype(F32)
    for j in range(2):
        lgf = lgf_ref[pair * 2 + j]
        lgb = lgb_ref[pair * 2 + j]
        dmat = jnp.where(diff >= 0, jnp.exp(lgf * jnp.maximum(diff, 0.0)), jnp.exp(lgb * jnp.maximum(-diff, 0.0)))
        dq_f = jnp.exp(lgf * (pos_col + 1.0))
        dq_b = jnp.exp(lgb * (L - pos_col))
        dk_f = jnp.exp(lgf * (L - 1.0 - pos_row))
        dk_b = jnp.exp(lgb * pos_row)
        dc_f = jnp.exp(lgf * L)
        dc_b = jnp.exp(lgb * L)
        qs = slice(j * RET_QK_DIM, (j + 1) * RET_QK_DIM)
        vs = slice(j * RET_V_DIM, (j + 1) * RET_V_DIM)

        def chunk(c):
            rows = slice(c * L, (c + 1) * L)
            return q_ref[0, rows, qs], kt_ref[0, qs, rows], v_ref[0, rows, vs]

        def readout(o, c, out_row0):
            oc = o - jnp.mean(o, -1, keepdims=True)
            on = oc * lax.rsqrt(jnp.mean(oc * oc, -1, keepdims=True) + NORM_EPS)
            g = g_ref[0, c * L:(c + 1) * L, vs]
            o_ref[0, out_row0:out_row0 + L, vs] = (on * gain_ref[:, vs] * (g * _sigmoid(g))).astype(o_ref.dtype)

        def intra(q, kt, v):
            sc = jnp.dot(q, kt, preferred_element_type=F32) * dmat
            return jnp.dot(sc.astype(BF16), v, preferred_element_type=F32)

        def state_add(kt, dk, v):
            return jnp.dot((kt.astype(F32) * dk).astype(BF16), v, preferred_element_type=F32)

        qc, ktc, vc = chunk(nc)
        if n_out > SEQ:
            readout(intra(qc, ktc, vc), nc, SEQ)
        s_f = state_add(ktc, dk_f, vc)
        s_b = state_add(ktc, dk_b, vc)
        for c in range(nc - 1, -1, -1):
            sb_ref[c] = s_b
            _, kt, v = chunk(c)
            s_b = s_b * dc_b + state_add(kt, dk_b, v)
        for c in range(nc):
            q, kt, v = chunk(c)
            qf = q.astype(F32)
            o = (intra(q, kt, v)
                 + jnp.dot((qf * dq_f).astype(BF16), s_f.astype(BF16), preferred_element_type=F32)
                 + jnp.dot((qf * dq_b).astype(BF16), sb_ref[c].astype(BF16), preferred_element_type=F32))
            readout(o, c, c * L)
            s_f = s_f * dc_f + state_add(kt, dk_f, v)


def retention(rq, rkt, rv, P3, gain, lgf, lgb, n_out):
    B = rq.shape[0]
    smem = pl.BlockSpec(memory_space=pltpu.SMEM)
    gcol = OFF_RG // 256
    return pl.pallas_call(
        functools.partial(_ret_kernel, n_out=n_out),
        grid=(B, RET_HEADS // 2),
        in_specs=[smem, smem,
                  pl.BlockSpec((1, S_ALL, 128), lambda b, p: (b, 0, p)),
                  pl.BlockSpec((1, 128, S_ALL), lambda b, p: (b, p, 0)),
                  pl.BlockSpec((1, S_ALL, 256), lambda b, p: (b, 0, p)),
                  pl.BlockSpec((1, S_ALL, 256), lambda b, p: (b, 0, gcol + p)),
                  pl.BlockSpec((1, 256), lambda b, p: (0, p))],
        out_specs=pl.BlockSpec((1, n_out, 256), lambda b, p: (b, 0, p)),
        out_shape=jax.ShapeDtypeStruct((B, n_out, RET_HEADS * RET_V_DIM), BF16),
        scratch_shapes=[pltpu.VMEM((SEQ // RET_CHUNK, RET_QK_DIM, RET_V_DIM), F32)],
        compiler_params=_cp(("parallel", "parallel")),
        name="retention",
    )(lgf, lgb, rq, rkt, rv, P3, gain.reshape(1, -1))


def _s5_kernel(u_ref, bb_ref, lam_ref, cm_ref, y_ref, bu_ref, pw_ref, st_ref):
    M = S5_MODES
    chunk = pl.program_id(2)
    lam_re = lam_ref[0, 0]
    lam_im = lam_ref[0, 1]

    @pl.when(chunk == 0)
    def _():
        st_ref[...] = jnp.zeros_like(st_ref)
        p_re, p_im = lam_re, lam_im
        for j in range(S5_STEPS):
            pw_ref[0, j] = p_re
            pw_ref[1, j] = p_im
            p_re, p_im = p_re * lam_re - p_im * lam_im, p_re * lam_im + p_im * lam_re

    r_i = lax.broadcasted_iota(jnp.int32, (ROW_TILE, ROW_TILE), 0)
    c_i = lax.broadcasted_iota(jnp.int32, (ROW_TILE, ROW_TILE), 1)
    flip = pl.program_id(0) == 1

    def scan_time(r):
        t = (r % S5_SEG) * S5_STEPS + r // S5_SEG
        return jnp.where(flip, ROW_TILE - 1 - t, t)

    to_scan = (c_i == scan_time(r_i)).astype(BF16)
    to_time = (r_i == scan_time(c_i)).astype(BF16)
    u = jnp.dot(to_scan, u_ref[...].astype(BF16), preferred_element_type=F32).astype(BF16)

    slab = BRANCH_W // S5_SLABS
    ms = M // S5_SLABS
    for s in range(S5_SLABS):
        part = jnp.dot(u[:, s * slab:(s + 1) * slab], bb_ref[0, s], preferred_element_type=F32)
        bu_ref[:, s * ms:(s + 1) * ms] = part[:, :ms]
        bu_ref[:, M + s * ms:M + (s + 1) * ms] = part[:, ms:]

    tile = 512
    for t in range(M // tile):
        re_sl = slice(t * tile, (t + 1) * tile)
        im_sl = slice(M + t * tile, M + (t + 1) * tile)
        lr, li = lam_re[:, re_sl], lam_im[:, re_sl]

        def step(j, carry):
            s_re, s_im = carry
            rows = pl.ds(pl.multiple_of(j * S5_SEG, S5_SEG), S5_SEG)
            n_re = lr * s_re - li * s_im + bu_ref[rows, re_sl]
            n_im = lr * s_im + li * s_re + bu_ref[rows, im_sl]
            bu_ref[rows, re_sl] = n_re
            bu_ref[rows, im_sl] = n_im
            return n_re, n_im

        z = jnp.zeros((S5_SEG, tile), F32)
        lax.fori_loop(0, S5_STEPS, step, (z, z))

    last = slice((S5_STEPS - 1) * S5_SEG, S5_STEPS * S5_SEG)
    e_re, e_im = bu_ref[last, 0:M], bu_ref[last, M:2 * M]
    pl_re, pl_im = pw_ref[0, S5_STEPS - 1][0:1], pw_ref[1, S5_STEPS - 1][0:1]
    c_re, c_im = st_ref[0:1, :], st_ref[1:2, :]
    rows_re, rows_im = [], []
    for k in range(S5_SEG):
        rows_re.append(c_re)
        rows_im.append(c_im)
        c_re, c_im = (e_re[k:k + 1] + pl_re * c_re - pl_im * c_im,
                      e_im[k:k + 1] + pl_re * c_im + pl_im * c_re)
    st_ref[0:1, :] = c_re
    st_ref[1:2, :] = c_im
    car_re = jnp.concatenate(rows_re, 0)
    car_im = jnp.concatenate(rows_im, 0)

    def fix(j, _):
        rows = pl.ds(pl.multiple_of(j * S5_SEG, S5_SEG), S5_SEG)
        p_re, p_im = pw_ref[0, j], pw_ref[1, j]
        bu_ref[rows, 0:M] = bu_ref[rows, 0:M] + p_re * car_re - p_im * car_im
        bu_ref[rows, M:2 * M] = bu_ref[rows, M:2 * M] + p_re * car_im + p_im * car_re
        return 0

    lax.fori_loop(0, S5_STEPS, fix, 0)

    ys = []
    for s in range(S5_SLABS):
        hs = jnp.concatenate([bu_ref[:, s * ms:(s + 1) * ms], bu_ref[:, M + s * ms:M + (s + 1) * ms]], -1)
        ys.append(jnp.dot(hs.astype(BF16), cm_ref[s], preferred_element_type=F32))
    y = jnp.concatenate(ys, -1)
    out = None
    for _ in range(3):
        piece = y.astype(BF16)
        y = y - piece.astype(F32)
        term = jnp.dot(to_time, piece, preferred_element_type=F32)
        out = term if out is None else out + term
    y_ref[0] = out


def _s5_block(d, c):
    fwd = (c + LAT_BLOCKS) % BLOCKS_PER_SAMPLE
    bwd = jnp.where(c == 0, LAT_BLOCKS, LAT_BLOCKS - c)
    return jnp.where(d == 0, fwd, bwd)


def s5_scan(P, bb, lam, cmat):
    R = P.shape[0]
    B = R // S_ALL
    W = BRANCH_W
    row_block = lambda d, b, c: b * BLOCKS_PER_SAMPLE + _s5_block(d, c)
    return pl.pallas_call(
        _s5_kernel,
        grid=(2, B, BLOCKS_PER_SAMPLE),
        in_specs=[pl.BlockSpec((ROW_TILE, W), lambda d, b, c: (row_block(d, b, c), 0)),
                  pl.BlockSpec((1,) + bb.shape[1:], lambda d, b, c: (d, 0, 0, 0)),
                  pl.BlockSpec((1, 2, S5_SEG, S5_MODES), lambda d, b, c: (d, 0, 0, 0)),
                  pl.BlockSpec(cmat.shape, lambda d, b, c: (0, 0, 0))],
        out_specs=pl.BlockSpec((1, ROW_TILE, W), lambda d, b, c: (d, row_block(d, b, c), 0)),
        out_shape=jax.ShapeDtypeStruct((2, R, W), F32),
        scratch_shapes=[pltpu.VMEM((ROW_TILE, 2 * S5_MODES), F32),
                        pltpu.VMEM((2, S5_STEPS, S5_SEG, S5_MODES), F32),
                        pltpu.VMEM((8, S5_MODES), F32)],
        compiler_params=_cp(("arbitrary", "arbitrary", "arbitrary")),
        name="s5_scan",
    )(P, bb, lam, cmat)


def _s5_out_kernel(yf_ref, yb_ref, p_ref, d_ref, w_ref, o_ref):
    y = yf_ref[0] + yb_ref[0] + d_ref[...] * p_ref[...]
    z = 0.5 * y * (1.0 + jnp.tanh(math.sqrt(2.0 / math.pi) * (y + 0.044715 * (y * y * y))))
    gate = _sigmoid(jnp.dot(z.astype(BF16), w_ref[...], preferred_element_type=F32))
    o_ref[...] = (z * gate).astype(o_ref.dtype)


def s5_output(y_dirs, P, d, w_glu):
    R = y_dirs.shape[1]
    return pl.pallas_call(
        _s5_out_kernel,
        grid=(R // ROW_TILE,),
        in_specs=[pl.BlockSpec((1, ROW_TILE, 512), lambda i: (0, i, 0)),
                  pl.BlockSpec((1, ROW_TILE, 512), lambda i: (1, i, 0)),
                  pl.BlockSpec((ROW_TILE, 512), lambda i: (i, 0)),
                  pl.BlockSpec((1, 512), lambda i: (0, 0)),
                  pl.BlockSpec((512, 512), lambda i: (0, 0))],
        out_specs=pl.BlockSpec((ROW_TILE, 512), lambda i: (i, 0)),
        out_shape=jax.ShapeDtypeStruct((R, 512), BF16),
        compiler_params=_cp(("parallel",)),
        name="s5_output",
    )(y_dirs, y_dirs, P, d.reshape(1, -1), w_glu)


def _merge_kernel(h_ref, o0_ref, o1_ref, o2_ref, o3_ref, g0_ref, g1_ref, g2_ref, g3_ref,
                  bg_ref, wb_ref, out_ref, wg_ref):
    @pl.when(pl.program_id(1) == 0)
    def _():
        for k, g in enumerate((g0_ref, g1_ref, g2_ref, g3_ref)):
            wg_ref[k] = g[0].astype(BF16)

    h = h_ref[...]
    acc = None
    for k, o in enumerate((o0_ref, o1_ref, o2_ref, o3_ref)):
        gate = _sigmoid(jnp.dot(h, wg_ref[k], preferred_element_type=F32) + bg_ref[k])
        term = gate * jnp.dot(o[...], wb_ref[k], preferred_element_type=F32)
        acc = term if acc is None else acc + term
    out_ref[...] = acc.astype(out_ref.dtype)


def merge_branches(h, outs, w_gate_all, layer, b_gate, w_branch, tn=256, tm=MM_ROW_TILE):
    R, D = h.shape
    nb = D // tn
    gate_spec = lambda k: pl.BlockSpec((1, D, tn), lambda n, m: (layer, 0, k * nb + n))
    bg = b_gate.reshape(N_BRANCH, 1, D)
    return pl.pallas_call(
        _merge_kernel,
        grid=(nb, R // tm),
        in_specs=[pl.BlockSpec((tm, D), lambda n, m: (m, 0))]
                 + [pl.BlockSpec((tm, BRANCH_W), lambda n, m: (m, 0))] * N_BRANCH
                 + [gate_spec(k) for k in range(N_BRANCH)]
                 + [pl.BlockSpec((N_BRANCH, 1, tn), lambda n, m: (0, 0, n)),
                    pl.BlockSpec((N_BRANCH, BRANCH_W, tn), lambda n, m: (0, 0, n))],
        out_specs=pl.BlockSpec((tm, tn), lambda n, m: (m, n)),
        out_shape=jax.ShapeDtypeStruct((R, D), BF16),
        scratch_shapes=[pltpu.VMEM((N_BRANCH, D, tn), BF16)],
        compiler_params=_cp(("arbitrary", "arbitrary")),
        name="merge_branches",
    )(h, *outs, w_gate_all, w_gate_all, w_gate_all, w_gate_all, bg, w_branch)


def _out_kernel(m_ref, x_ref, w_ref, g1_ref, lg_ref, lb_ref, sh_ref, sc_ref, rw_ref,
                x1_ref, h2_ref, lo_ref):
    y = jnp.dot(m_ref[...], w_ref[...], preferred_element_type=F32)
    x1 = _standardize(DEEPNORM_ALPHA * x_ref[...] + g1_ref[0] * y) * lg_ref[...] + lb_ref[...]
    x1_ref[...] = x1
    h2 = _standardize(x1) * (1.0 + sc_ref[0]) + sh_ref[0]
    _store_token_tiles(h2_ref, h2)
    rw = rw_ref[...]
    h_hi, rw_hi = h2.astype(BF16), rw.astype(BF16)
    h_lo, rw_lo = (h2 - h_hi.astype(F32)).astype(BF16), (rw - rw_hi.astype(F32)).astype(BF16)
    lo_ref[...] = (jnp.dot(h_hi, rw_hi, preferred_element_type=F32)
                   + jnp.dot(h_lo, rw_hi, preferred_element_type=F32)
                   + jnp.dot(h_hi, rw_lo, preferred_element_type=F32))


def out_proj_norm(merged, x, w_out, modblk, ln_g, ln_b, router_pad):
    R, D = x.shape
    tt = D // LANES
    row = lambda w: pl.BlockSpec((ROW_TILE, w), lambda i: (i, 0))
    mod = lambda part: pl.BlockSpec((1, 1, D), lambda i: (i, 0, part))
    vec = pl.BlockSpec((1, D), lambda i: (0, 0))
    return pl.pallas_call(
        _out_kernel,
        grid=(R // ROW_TILE,),
        in_specs=[row(D), row(D), pl.BlockSpec((D, D), lambda i: (0, 0)), mod(2), vec, vec, mod(3), mod(4),
                  pl.BlockSpec((D, 128), lambda i: (0, 0))],
        out_specs=[row(D), pl.BlockSpec((ROW_TILE * tt, LANES), lambda i: (i, 0)), row(128)],
        out_shape=[jax.ShapeDtypeStruct((R, D), F32), jax.ShapeDtypeStruct((R * tt, LANES), F32),
                   jax.ShapeDtypeStruct((R, 128), F32)],
        compiler_params=_cp(("parallel",)),
        name="out_proj_norm",
    )(merged, x, w_out, modblk, ln_g.reshape(1, -1), ln_b.reshape(1, -1), modblk, modblk, router_pad)


ROUTE_GEOMETRIC_STEPS = 40
ROUTE_BISECT_STEPS = ROUTE_GEOMETRIC_STEPS + 8


def _route_kernel(lt_ref, idx_ref, gate_ref, slot_ref, aff_ref, *, n, cap):
    E = N_EXPERTS
    lt = lt_ref[0]
    ex = jnp.exp(lt - jnp.max(lt, 0, keepdims=True))
    aff = ex / jnp.sum(ex, 0, keepdims=True)

    def count_ge(v):
        return jnp.sum((aff >= v).astype(F32), 1, keepdims=True)

    tiny = jnp.full((E, 1), float(np.finfo(np.float32).tiny), F32)
    normal = count_ge(tiny) >= cap
    lo0 = jnp.where(normal, tiny, 0.0)
    hi0 = jnp.where(normal, 2.0, tiny)

    def bisect(i, lo_hi):
        lo, hi = lo_hi
        geo = jnp.clip(jnp.sqrt(lo) * jnp.sqrt(hi), lo, hi)
        mid = jnp.where(jnp.logical_and(i < ROUTE_GEOMETRIC_STEPS, lo > 0.0), geo, lo + 0.5 * (hi - lo))
        ok = count_ge(mid) >= cap
        return jnp.where(ok, mid, lo), jnp.where(ok, hi, mid)

    lo, _ = lax.fori_loop(0, ROUTE_BISECT_STEPS, bisect, (lo0, hi0))
    thr = jnp.min(jnp.where(aff >= lo, aff, 2.0), 1, keepdims=True)
    gt = aff > thr
    eq = aff == thr
    need = cap - jnp.sum(gt.astype(F32), 1, keepdims=True)

    blk = min(n, 512)
    upper = (lax.broadcasted_iota(jnp.int32, (blk, blk), 0)
             < lax.broadcasted_iota(jnp.int32, (blk, blk), 1)).astype(BF16)

    def excl_cumsum(mask):
        parts, carry = [], jnp.zeros((E, 1), F32)
        for j in range(n // blk):
            m = mask[:, j * blk:(j + 1) * blk].astype(F32)
            parts.append(jnp.dot(m.astype(BF16), upper, preferred_element_type=F32) + carry)
            carry = carry + jnp.sum(m, 1, keepdims=True)
        return parts[0] if len(parts) == 1 else jnp.concatenate(parts, 1)

    sel = gt | (eq & (excl_cumsum(eq) < need))
    slot_ref[...] = jnp.where(sel, excl_cumsum(sel), -1.0)
    aff_ref[...] = aff
    idx_ref[0] = jnp.zeros((cap, 128), F32)
    gate_ref[0] = jnp.zeros((cap, 128), F32)

    rows = min(cap, 64)
    lane_e = lax.broadcasted_iota(jnp.int32, (rows, 128), 1)
    tok = lax.broadcasted_iota(jnp.int32, (1, 128), 1).astype(F32)

    for e in range(E):
        def per_rows(c, _, e=e):
            r0 = pl.multiple_of(c * rows, rows)
            s_col = (lax.broadcasted_iota(jnp.int32, (rows, 1), 0) + r0).astype(F32)
            acc_i = jnp.zeros((rows, 128), F32)
            acc_g = jnp.zeros((rows, 128), F32)
            for j in range(n // 128):
                hit = slot_ref[e:e + 1, j * 128:(j + 1) * 128] == s_col
                acc_i = acc_i + jnp.where(hit, tok + float(j * 128), 0.0)
                acc_g = acc_g + jnp.where(hit, aff_ref[e:e + 1, j * 128:(j + 1) * 128], 0.0)
            icol = jnp.sum(acc_i, 1, keepdims=True)
            gcol = jnp.sum(acc_g, 1, keepdims=True)
            idx_ref[0, pl.ds(r0, rows), :] = jnp.where(lane_e == e, icol, idx_ref[0, pl.ds(r0, rows), :])
            gate_ref[0, pl.ds(r0, rows), :] = jnp.where(lane_e == e, gcol, gate_ref[0, pl.ds(r0, rows), :])
            return 0

        lax.fori_loop(0, cap // rows, per_rows, 0)


def route(logits_t, off, n):
    B = logits_t.shape[0]
    cap = EC_CAPACITY_FACTOR * n // N_EXPERTS
    out = pl.BlockSpec((1, cap, 128), lambda b: (b, 0, 0))
    return pl.pallas_call(
        functools.partial(_route_kernel, n=n, cap=cap),
        grid=(B,),
        in_specs=[pl.BlockSpec((1, N_EXPERTS, n), lambda b: (b, 0, off // n))],
        out_specs=[out, out],
        out_shape=[jax.ShapeDtypeStruct((B, cap, 128), F32)] * 2,
        scratch_shapes=[pltpu.VMEM((N_EXPERTS, n), F32), pltpu.VMEM((N_EXPERTS, n), F32)],
        compiler_params=_cp(("parallel",)),
        name="route",
    )(logits_t)


GATHER_UNROLL = 8


def _gather_kernel(rows_ref, h_hbm, o_ref, land_ref, sem, *, T, tt):
    def tile_copy(s, r):
        src = h_hbm.at[pl.ds(pl.multiple_of(r * tt, tt), tt), :]
        dst = land_ref.at[pl.ds(pl.multiple_of(s * tt, tt), tt), :]
        return pltpu.make_async_copy(src, dst, sem)

    def issue(g, _):
        for i in range(GATHER_UNROLL):
            s = g * GATHER_UNROLL + i
            tile_copy(s, rows_ref[0, 0, s]).start()
        return 0

    lax.fori_loop(0, T // GATHER_UNROLL, issue, 0)

    def drain(g, _):
        for i in range(GATHER_UNROLL):
            tile_copy(g * GATHER_UNROLL + i, 0).wait()
        return 0

    lax.fori_loop(0, T // GATHER_UNROLL, drain, 0)
    o_ref[0] = _load_token_tiles(land_ref, T, tt * LANES, BF16)


def gather_rows(rows, h_tiles, D):
    E, _, T = rows.shape
    tt = D // LANES
    return pl.pallas_call(
        functools.partial(_gather_kernel, T=T, tt=tt),
        grid=(E,),
        in_specs=[pl.BlockSpec((1, 1, T), lambda e: (e, 0, 0), memory_space=pltpu.SMEM),
                  pl.BlockSpec(memory_space=pl.ANY)],
        out_specs=pl.BlockSpec((1, T, D), lambda e: (e, 0, 0)),
        out_shape=jax.ShapeDtypeStruct((E, T, D), BF16),
        scratch_shapes=[pltpu.VMEM((T * tt, LANES), F32), pltpu.SemaphoreType.DMA],
        compiler_params=_cp(("arbitrary",)),
        name="gather_rows",
    )(rows, h_tiles)


def _expert_kernel(x_ref, gate_ref, wg_ref, wu_ref, wd_ref, o_ref):
    last = pl.num_programs(1) - 1
    x = x_ref[0]
    a = jnp.dot(x, wg_ref[0, 0].astype(BF16), preferred_element_type=F32)
    u = jnp.dot(x, wu_ref[0, 0].astype(BF16), preferred_element_type=F32)
    hid = (a * _sigmoid(a) * u).astype(BF16)
    part = jnp.dot(hid, wd_ref[0, 0].astype(BF16), preferred_element_type=F32)

    @pl.when(pl.program_id(1) == 0)
    def _():
        o_ref[0] = part

    @pl.when(jnp.logical_and(pl.program_id(1) != 0, pl.program_id(1) != last))
    def _():
        o_ref[0] += part

    @pl.when(pl.program_id(1) == last)
    def _():
        o_ref[0] = (o_ref[0] + part) * gate_ref[0]


def expert_ffn(xs, gate, w_gate, w_up, w_down, layer, tf=256):
    E, T, D = xs.shape
    FF = w_gate.shape[-1]
    return pl.pallas_call(
        _expert_kernel,
        grid=(E, FF // tf),
        in_specs=[pl.BlockSpec((1, T, D), lambda e, f: (e, 0, 0)),
                  pl.BlockSpec((1, T, 1), lambda e, f: (e, 0, 0)),
                  pl.BlockSpec((1, 1, D, tf), lambda e, f: (layer, e, 0, f)),
                  pl.BlockSpec((1, 1, D, tf), lambda e, f: (layer, e, 0, f)),
                  pl.BlockSpec((1, 1, tf, D), lambda e, f: (layer, e, f, 0))],
        out_specs=pl.BlockSpec((1, T, D), lambda e, f: (e, 0, 0)),
        out_shape=jax.ShapeDtypeStruct((E, T, D), F32),
        compiler_params=_cp(("parallel", "arbitrary")),
        name="expert_ffn",
    )(xs, gate, w_gate, w_up, w_down)


COMBINE_UNROLL = 8


def _combine_kernel(*refs, caps, bases, dh):
    n_sets = len(caps)
    idx_refs, y_refs = refs[:n_sets], refs[n_sets:2 * n_sets]
    o_ref, yt_ref = refs[2 * n_sets], refs[2 * n_sets + 1]
    tt = dh // LANES

    @pl.when(pl.program_id(2) == 0)
    def _():
        o_ref[...] = jnp.zeros_like(o_ref)

    for idx_ref, y_ref, cap, base in zip(idx_refs, y_refs, caps, bases):
        for j in range(tt):
            yt_ref[pl.ds(j, cap, stride=tt), :] = y_ref[0, :, j * LANES:(j + 1) * LANES]

        def body(g, _, idx_ref=idx_ref, base=base):
            s0 = g * COMBINE_UNROLL
            toks = [idx_ref[0, 0, 0, s0 + i] + base for i in range(COMBINE_UNROLL)]
            sums = [o_ref[0, toks[i], 0] + yt_ref[pl.ds(pl.multiple_of((s0 + i) * tt, tt), tt), :]
                    for i in range(COMBINE_UNROLL)]
            for i in range(COMBINE_UNROLL):
                o_ref[0, toks[i], 0] = sums[i]
            return 0

        lax.fori_loop(0, cap // COMBINE_UNROLL, body, 0)


def combine(ys, idx_sets, bases, dh=SUBLANES * LANES):
    E, T, D = ys.shape
    B = idx_sets[0].shape[0]
    caps = tuple(int(i.shape[-1]) for i in idx_sets)
    starts = np.concatenate([[0], np.cumsum([B * c for c in caps])[:-1]])
    idx_specs = [pl.BlockSpec((1, 1, 1, c), lambda b, hf, e: (b, e, 0, 0), memory_space=pltpu.SMEM) for c in caps]
    y_specs = [pl.BlockSpec((1, c, dh), lambda b, hf, e, blk0=int(st) // c: (e, blk0 + b, hf))
               for c, st in zip(caps, starts)]
    return pl.pallas_call(
        functools.partial(_combine_kernel, caps=caps, bases=tuple(bases), dh=dh),
        grid=(B, D // dh, E),
        in_specs=idx_specs + y_specs,
        out_specs=pl.BlockSpec((1, S_ALL, 1, SUBLANES, LANES), lambda b, hf, e: (b, 0, hf, 0, 0)),
        out_shape=jax.ShapeDtypeStruct((B, S_ALL, D // dh, SUBLANES, LANES), F32),
        scratch_shapes=[pltpu.VMEM((max(caps) * dh // LANES, LANES), F32)],
        compiler_params=_cp(("parallel", "parallel", "arbitrary")),
        name="combine",
    )(*[i[:, :, None, :] for i in idx_sets], *([ys] * len(caps)))


def _post_kernel(x_ref, y_ref, g2_ref, lg_ref, lb_ref, sh_ref, sc_ref, x2_ref, h_ref):
    y = _load_token_tiles(y_ref, ROW_TILE, x_ref.shape[1])
    x2 = _standardize(DEEPNORM_ALPHA * x_ref[...] + g2_ref[0] * y) * lg_ref[...] + lb_ref[...]
    x2_ref[...] = x2
    h_ref[...] = (_standardize(x2) * (1.0 + sc_ref[0]) + sh_ref[0]).astype(BF16)


def post_moe_norm(x1, y_tiles, modblk, ln_g, ln_b, modblk_next):
    R, D = x1.shape
    row = pl.BlockSpec((ROW_TILE, D), lambda i: (i, 0))
    mod = lambda part: pl.BlockSpec((1, 1, D), lambda i: (i, 0, part))
    vec = pl.BlockSpec((1, D), lambda i: (0, 0))
    y = pl.BlockSpec((ROW_TILE * D // LANES, LANES), lambda i: (i, 0))
    return pl.pallas_call(
        _post_kernel,
        grid=(R // ROW_TILE,),
        in_specs=[row, y, mod(5), vec, vec, mod(0), mod(1)],
        out_specs=[row, row],
        out_shape=[jax.ShapeDtypeStruct((R, D), F32), jax.ShapeDtypeStruct((R, D), BF16)],
        compiler_params=_cp(("parallel",)),
        name="post_moe_norm",
    )(x1, y_tiles, modblk, ln_g.reshape(1, -1), ln_b.reshape(1, -1), modblk_next, modblk_next)


def _s5_matrices(a_re, a_im, log_dt, b_re, b_im):
    dt = jnp.exp(log_dt)[:, None]
    mag = jnp.exp(a_re * dt)
    ab_re, ab_im = mag * jnp.cos(a_im * dt), mag * jnp.sin(a_im * dt)
    den = a_re * a_re + a_im * a_im
    num_re, num_im = ab_re - 1.0, ab_im
    coef_re = (num_re * a_re + num_im * a_im) / den
    coef_im = (num_im * a_re - num_re * a_im) / den
    bb_re = coef_re[..., None] * b_re - coef_im[..., None] * b_im
    bb_im = coef_re[..., None] * b_im + coef_im[..., None] * b_re
    gs = S5_GROUPS // S5_SLABS
    eye = jnp.eye(gs, dtype=F32)

    def slabs(t):
        t = t.reshape(S5_SLABS, gs, S5_STATE, S5_GROUP_CH)
        return jnp.einsum('sgpi,gh->sgihp', t, eye).reshape(S5_SLABS, gs * S5_GROUP_CH, gs * S5_STATE)

    lam = jnp.stack([ab_re.reshape(-1), ab_im.reshape(-1)], 0)
    return lam, jnp.concatenate([slabs(bb_re), slabs(bb_im)], -1)


def _s5_readout_matrix(c_re, c_im):
    gs = S5_GROUPS // S5_SLABS
    eye = jnp.eye(gs, dtype=F32)

    def slabs(t):
        t = t.reshape(S5_SLABS, gs, S5_GROUP_CH, S5_STATE)
        return jnp.einsum('sgip,gh->sgphi', t, eye).reshape(S5_SLABS, gs * S5_STATE, gs * S5_GROUP_CH)

    return jnp.concatenate([slabs(c_re), -slabs(c_im)], 1)


def _mla_weights(w_uq, w_ukv):
    qk = MLA_NOPE_DIM + MLA_ROPE_DIM
    wq = w_uq.reshape(MLA_Q_LORA, MLA_HEADS, qk)
    wq = jnp.pad(wq, ((0, 0), (0, 0), (0, MLA_QK_PAD - qk))).reshape(MLA_Q_LORA, MLA_HEADS * MLA_QK_PAD)
    wkv = w_ukv.reshape(MLA_KV_LORA, MLA_HEADS, MLA_NOPE_DIM + MLA_V_DIM)
    wk = jnp.pad(wkv[..., :MLA_NOPE_DIM], ((0, 0), (0, 0), (0, MLA_QK_PAD - MLA_NOPE_DIM)))
    wv = wkv[..., MLA_NOPE_DIM:]
    wkv_pad = jnp.concatenate([wk.reshape(MLA_KV_LORA, -1), wv.reshape(MLA_KV_LORA, -1)], 1)
    return wq.astype(BF16), wkv_pad.astype(BF16)


def kernel(x, c, ctx, c_ctx, ada_w, ada_b, w_in, s5_a_re_f, s5_a_im_f, s5_log_dt_f, s5_a_re_b, s5_a_im_b,
           s5_log_dt_b, s5_b_re, s5_b_im, s5_c_re, s5_c_im, s5_d, s5_w_glu, gqa_q_norm, gqa_k_norm,
           ret_decay_f, ret_decay_b, ret_norm, mla_q_norm, mla_kv_norm, mla_w_uq, mla_w_ukv,
           w_branch, w_gate, b_gate, w_out, ln1_g, ln1_b, router_w, moe_w_gate, moe_w_up, moe_w_down,
           ln2_g, ln2_b):
    B, N, D = x.shape
    R = B * S_ALL
    assert (B, N, D) == (BATCH, SEQ, D_MODEL) and RET_CHUNK == CTX_LEN == ROW_TILE

    cc = jnp.zeros((16, D), F32).at[:B].set(c).at[B].set(c_ctx)
    mod = ada_modulation(cc, ada_w, ada_b)
    sel = np.concatenate([np.r_[np.full(LAT_BLOCKS, b), B] for b in range(B)]).astype(np.int32)
    modblks = [mod[l][sel].reshape(R // ROW_TILE, 1, 6 * D) for l in range(DEPTH)]

    X = jnp.concatenate([x, ctx], 1).reshape(R, D)
    h = modulate_rows(X, modblks[0], 0, 1)

    for l in range(DEPTH):
        need_ctx = l < DEPTH - 1
        modblk = modblks[l]
        w_in_p = jnp.pad(w_in[l], ((0, 0), (0, IN_PAD - IN_TOTAL))).astype(BF16)
        P = matmul_bf16(h, w_in_p, tn=1024, name="in_proj")
        P3 = P.reshape(B, S_ALL, IN_PAD)
        gq, gk, gv, rq, rk, rv, cqn, ckvn, krp = prep_branches(P, gqa_q_norm[l], gqa_k_norm[l],
                                                               mla_q_norm[l], mla_kv_norm[l])
        to3 = lambda t: t.reshape(B, S_ALL, t.shape[-1])

        lam_f, bb_f = _s5_matrices(s5_a_re_f[l], s5_a_im_f[l], s5_log_dt_f[l], s5_b_re[l], s5_b_im[l])
        lam_b, bb_b = _s5_matrices(s5_a_re_b[l], s5_a_im_b[l], s5_log_dt_b[l], s5_b_re[l], s5_b_im[l])
        lam = jnp.broadcast_to(jnp.stack([lam_f, lam_b], 0)[:, :, None, :], (2, 2, S5_SEG, S5_MODES))
        bb = jnp.stack([bb_f, bb_b], 0).astype(BF16)
        cmat = _s5_readout_matrix(s5_c_re[l], s5_c_im[l]).astype(BF16)
        o_s5 = s5_output(s5_scan(P, bb, lam, cmat), P, s5_d[l], s5_w_glu[l].astype(BF16))

        gq3, gk3, gv3 = to3(gq), to3(gk), to3(gv)
        att = functools.partial(attention, kv_heads=GQA_KV_HEADS, groups=GQA_HEADS // GQA_KV_HEADS,
                                dk=GQA_HEAD_DIM, dv=GQA_HEAD_DIM)
        o_lat = att(gq3, gk3, gv3, q_rows=SEQ, q_off=0, kv_rows=S_ALL, kv_off=0, tq=256)
        o_ctx = att(gq3, gk3, gv3, q_rows=CTX_LEN, q_off=SEQ, kv_rows=CTX_LEN, kv_off=SEQ, tq=256)
        o_gqa = jnp.concatenate([o_lat, o_ctx], 1).reshape(R, BRANCH_W)

        lgf = -jnp.exp(ret_decay_f[l])
        lgb = -jnp.exp(ret_decay_b[l])
        rkt = jnp.swapaxes(to3(rk), 1, 2)
        o_ret = retention(to3(rq), rkt, to3(rv), P3, ret_norm[l], lgf, lgb, S_ALL).reshape(R, BRANCH_W)

        wq_pad, wkv_pad = _mla_weights(mla_w_uq[l], mla_w_ukv[l])
        mq, mk, mv = mla_up(cqn, ckvn, krp, wq_pad, wkv_pad)
        matt = functools.partial(attention, kv_heads=MLA_HEADS, groups=1, dk=MLA_QK_PAD, dv=MLA_V_DIM)
        m_lat = matt(to3(mq), to3(mk), to3(mv), q_rows=SEQ, q_off=0, kv_rows=S_ALL, kv_off=0, tq=512)
        m_ctx = matt(to3(mq), to3(mk), to3(mv), q_rows=CTX_LEN, q_off=SEQ, kv_rows=CTX_LEN, kv_off=SEQ, tq=256)
        o_mla = jnp.concatenate([m_lat, m_ctx], 1).reshape(R, BRANCH_W)

        merged = merge_branches(h, (o_s5, o_gqa, o_ret, o_mla), w_gate, l, b_gate[l], w_branch[l].astype(BF16))
        router_pad = jnp.pad(router_w[l], ((0, 0), (0, 128 - N_EXPERTS)))
        x1, h2, logits = out_proj_norm(merged, X, w_out[l].astype(BF16), modblk, ln1_g[l], ln1_b[l], router_pad)

        logits_t = jnp.swapaxes(logits.reshape(B, S_ALL, 128)[:, :, :N_EXPERTS], 1, 2)
        sets = [(0, SEQ)] + ([(SEQ, CTX_LEN)] if need_ctx else [])
        idx_sets, row_parts, gate_parts = [], [], []
        sample_row0 = (jnp.arange(B, dtype=jnp.int32) * S_ALL)[:, None, None]
        for off, n in sets:
            idx_f, gate_f = route(logits_t, off, n)
            idx = jnp.swapaxes(idx_f[:, :, :N_EXPERTS], 1, 2).astype(jnp.int32)
            gate = jnp.swapaxes(gate_f[:, :, :N_EXPERTS], 1, 2)
            idx_sets.append(idx)
            row_parts.append(jnp.swapaxes(idx + sample_row0 + off, 0, 1).reshape(N_EXPERTS, -1))
            gate_parts.append(jnp.swapaxes(gate, 0, 1).reshape(N_EXPERTS, -1))
        rows = jnp.concatenate(row_parts, 1)[:, None, :]
        gates = jnp.concatenate(gate_parts, 1)[:, :, None]
        xs = gather_rows(rows, h2, D)
        ys = expert_ffn(xs, gates, moe_w_gate, moe_w_up, moe_w_down, l)
        moe = combine(ys, idx_sets, [off for off, _ in sets]).reshape(R * D // LANES, LANES)
        X, h = post_moe_norm(x1, moe, modblk, ln2_g[l], ln2_b[l], modblks[min(l + 1, DEPTH - 1)])

    return X.reshape(B, S_ALL, D)[:, :SEQ]
```

```python
import functools
import math

import numpy as np
import jax
import jax.numpy as jnp
from jax import lax
from jax.experimental import pallas as pl
from jax.experimental.pallas import tpu as pltpu

F32 = jnp.float32
BF16 = jnp.bfloat16

D_MODEL = 2048
BATCH = 2
SEQ = 4096
DEPTH = 2
GRID_W = 64
CTX_LEN = 256
S_ALL = SEQ + CTX_LEN
N_BRANCH = 4
BRANCH_W = D_MODEL // 4
S5_GROUP_CH = 16
S5_GROUPS = BRANCH_W // S5_GROUP_CH
S5_STATE = 64
S5_MODES = S5_GROUPS * S5_STATE
GQA_HEAD_DIM = 128
GQA_HEADS = 4
GQA_KV_HEADS = 2
RET_HEADS = 4
RET_V_DIM = 128
RET_QK_DIM = 64
MLA_HEADS = 4
MLA_Q_LORA = 512
MLA_KV_LORA = 256
MLA_NOPE_DIM = 128
MLA_ROPE_DIM = 64
MLA_V_DIM = 128
MLA_QK_PAD = 256
N_EXPERTS = 16
EXPERT_FF = D_MODEL // 2
EC_CAPACITY_FACTOR = 2
ROPE_BASE = 10000.0
NORM_EPS = 1e-6
LOG2E = math.log2(math.e)
DEEPNORM_ALPHA = (2 * DEPTH) ** 0.25
IN_WIDTHS = (512, 512, 256, 256, 256, 256, 512, 512, 512, 256, 64)
IN_TOTAL = sum(IN_WIDTHS)
IN_PAD = 4096
(OFF_U, OFF_GQ, OFF_GK, OFF_GV, OFF_RQ, OFF_RK, OFF_RV, OFF_RG,
 OFF_CQ, OFF_CKV, OFF_KR) = (int(v) for v in np.concatenate([[0], np.cumsum(IN_WIDTHS)[:-1]]))

LANES = 128
SUBLANES = 8
ROW_TILE = 256
MM_ROW_TILE = 512
ATTN_SUB = 256
BLOCKS_PER_SAMPLE = S_ALL // ROW_TILE
LAT_BLOCKS = SEQ // ROW_TILE
S5_SLABS = 4
S5_SEG = 8
S5_STEPS = ROW_TILE // S5_SEG
RET_CHUNK = 256
VMEM_LIMIT = 56 * 1024 * 1024


def _cp(sem, vmem=VMEM_LIMIT):
    return pltpu.CompilerParams(dimension_semantics=sem, vmem_limit_bytes=vmem)


def _standardize(x):
    xc = x - jnp.mean(x, -1, keepdims=True)
    return xc * lax.rsqrt(jnp.mean(xc * xc, -1, keepdims=True) + NORM_EPS)


def _sigmoid(x):
    return 1.0 / (1.0 + jnp.exp(-x))


def _store_token_tiles(ref, x, row0=0):
    rows, w = x.shape
    tt = w // LANES
    for j in range(tt):
        ref[pl.ds(row0 * tt + j, rows, stride=tt), :] = x[:, j * LANES:(j + 1) * LANES]


def _load_token_tiles(ref, rows, w, dtype=F32):
    tt = w // LANES
    return jnp.concatenate([ref[pl.ds(j, rows, stride=tt), :].astype(dtype) for j in range(tt)], -1)


def _ada_kernel(c_ref, w_ref, b_ref, o_ref):
    c = c_ref[...]
    cs = (c * _sigmoid(c)).astype(BF16)
    o_ref[0] = jnp.dot(cs, w_ref[0].astype(BF16), preferred_element_type=F32) + b_ref[0]


def ada_modulation(cc, ada_w, ada_b, tn=1024):
    L, D, N = ada_w.shape
    return pl.pallas_call(
        _ada_kernel,
        grid=(L, N // tn),
        in_specs=[pl.BlockSpec((16, D), lambda l, n: (0, 0)),
                  pl.BlockSpec((1, D, tn), lambda l, n: (l, 0, n)),
                  pl.BlockSpec((1, 1, tn), lambda l, n: (l, 0, n))],
        out_specs=pl.BlockSpec((1, 16, tn), lambda l, n: (l, 0, n)),
        out_shape=jax.ShapeDtypeStruct((L, 16, N), F32),
        compiler_params=_cp(("arbitrary", "arbitrary")),
        name="ada_modulation",
    )(cc, ada_w, ada_b.reshape(L, 1, N))


def _modulate_kernel(x_ref, sh_ref, sc_ref, o_ref):
    o_ref[...] = (_standardize(x_ref[...]) * (1.0 + sc_ref[0]) + sh_ref[0]).astype(o_ref.dtype)


def modulate_rows(x, modblk, shift_part, scale_part):
    R, D = x.shape
    return pl.pallas_call(
        _modulate_kernel,
        grid=(R // ROW_TILE,),
        in_specs=[pl.BlockSpec((ROW_TILE, D), lambda i: (i, 0)),
                  pl.BlockSpec((1, 1, D), lambda i: (i, 0, shift_part)),
                  pl.BlockSpec((1, 1, D), lambda i: (i, 0, scale_part))],
        out_specs=pl.BlockSpec((ROW_TILE, D), lambda i: (i, 0)),
        out_shape=jax.ShapeDtypeStruct((R, D), BF16),
        compiler_params=_cp(("parallel",)),
        name="modulate_rows",
    )(x, modblk, modblk)


def _mm_kernel(x_ref, w_ref, o_ref):
    o_ref[...] = jnp.dot(x_ref[...], w_ref[...], preferred_element_type=F32).astype(o_ref.dtype)


def matmul_bf16(x, w, tn, out_dtype=F32, tm=MM_ROW_TILE, name="matmul_bf16"):
    R, K = x.shape
    N = w.shape[1]
    return pl.pallas_call(
        _mm_kernel,
        grid=(N // tn, R // tm),
        in_specs=[pl.BlockSpec((tm, K), lambda n, m: (m, 0)),
                  pl.BlockSpec((K, tn), lambda n, m: (0, n))],
        out_specs=pl.BlockSpec((tm, tn), lambda n, m: (m, n)),
        out_shape=jax.ShapeDtypeStruct((R, N), out_dtype),
        compiler_params=_cp(("arbitrary", "arbitrary")),
        name=name,
    )(x, w)


def _rope_tables(head_dim, width):
    half = head_dim // 2
    n = half // 2
    inv = ROPE_BASE ** (-np.arange(n, dtype=np.float64) / n)
    t = np.arange(SEQ)
    pos = np.stack([t // GRID_W, t % GRID_W], 0).astype(np.float64)
    lane = np.arange(head_dim)
    which = lane // half
    m = lane % half
    ang = (pos[which, :].T.astype(np.float32) * inv[m % n].astype(np.float32)[None, :]).astype(np.float64)
    cos = np.cos(ang)
    sin = np.where(m < n, -np.sin(ang), np.sin(ang))
    cos = np.concatenate([cos, np.ones((CTX_LEN, head_dim))], 0)
    sin = np.concatenate([sin, np.zeros((CTX_LEN, head_dim))], 0)
    reps = width // head_dim
    return (np.tile(cos, (1, reps)).astype(np.float32), np.tile(sin, (1, reps)).astype(np.float32))


def _mla_q_tables():
    cos64, sin64 = _rope_tables(MLA_ROPE_DIM, MLA_ROPE_DIM)
    ones = np.ones((S_ALL, MLA_NOPE_DIM), np.float32)
    zeros = np.zeros((S_ALL, MLA_NOPE_DIM), np.float32)
    pad1 = np.ones((S_ALL, MLA_QK_PAD - MLA_NOPE_DIM - MLA_ROPE_DIM), np.float32)
    cos = np.concatenate([ones, cos64, pad1], 1)
    sin = np.concatenate([zeros, sin64, 0 * pad1], 1)
    return cos, sin


def _rope(x, cos, sin, quarter):
    w = x.shape[-1]
    lane = lax.broadcasted_iota(jnp.int32, x.shape, 1)
    first = (lane % (2 * quarter)) < quarter
    partner = jnp.where(first, pltpu.roll(x, w - quarter, 1), pltpu.roll(x, quarter, 1))
    return x * cos + partner * sin


def _rms_heads(x, gain, head_dim):
    outs = []
    for h in range(x.shape[-1] // head_dim):
        xh = x[:, h * head_dim:(h + 1) * head_dim]
        outs.append(xh * lax.rsqrt(jnp.mean(xh * xh, -1, keepdims=True) + NORM_EPS) * gain)
    return outs[0] if len(outs) == 1 else jnp.concatenate(outs, -1)


def _prep_kernel(p_ref, c128_ref, s128_ref, c64_ref, s64_ref, gqn_ref, gkn_ref, mqn_ref, mkvn_ref,
                 gq_ref, gk_ref, gv_ref, rq_ref, rk_ref, rv_ref, cq_ref, ckv_ref, kr_ref):
    c128, s128 = c128_ref[...], s128_ref[...]
    c64, s64 = c64_ref[...], s64_ref[...]
    tile2 = lambda t: jnp.concatenate([t, t], -1)
    q = _rms_heads(p_ref[:, OFF_GQ:OFF_GQ + 512], gqn_ref[...], GQA_HEAD_DIM)
    q = _rope(q, jnp.concatenate([c128] * 4, -1), jnp.concatenate([s128] * 4, -1), GQA_HEAD_DIM // 4)
    gq_ref[...] = (q * (GQA_HEAD_DIM ** -0.5 * LOG2E)).astype(BF16)
    k = _rms_heads(p_ref[:, OFF_GK:OFF_GK + 256], gkn_ref[...], GQA_HEAD_DIM)
    gk_ref[...] = _rope(k, tile2(c128), tile2(s128), GQA_HEAD_DIM // 4).astype(BF16)
    gv_ref[...] = p_ref[:, OFF_GV:OFF_GV + 256].astype(BF16)
    rq_ref[...] = _rope(p_ref[:, OFF_RQ:OFF_RQ + 256], tile2(c64), tile2(s64), RET_QK_DIM // 4).astype(BF16)
    rk = _rope(p_ref[:, OFF_RK:OFF_RK + 256], tile2(c64), tile2(s64), RET_QK_DIM // 4)
    rk_ref[...] = (rk * (RET_QK_DIM ** -0.5)).astype(BF16)
    rv_ref[...] = p_ref[:, OFF_RV:OFF_RV + 512].astype(BF16)
    cq_ref[...] = _rms_heads(p_ref[:, OFF_CQ:OFF_CQ + 512], mqn_ref[...], MLA_Q_LORA).astype(BF16)
    ckv_ref[...] = _rms_heads(p_ref[:, OFF_CKV:OFF_CKV + 256], mkvn_ref[...], MLA_KV_LORA).astype(BF16)
    lane = lax.broadcasted_iota(jnp.int32, (ROW_TILE, 128), 1)
    kr = _rope(p_ref[:, OFF_KR:OFF_KR + 128], c64, s64, MLA_ROPE_DIM // 4)
    kr_ref[...] = jnp.where(lane < MLA_ROPE_DIM, kr, 0.0)


def prep_branches(P, gqa_q_norm, gqa_k_norm, mla_q_norm, mla_kv_norm):
    R = P.shape[0]
    c128, s128 = _rope_tables(GQA_HEAD_DIM, 128)
    c64, s64 = _rope_tables(RET_QK_DIM, 128)
    row = lambda w: pl.BlockSpec((ROW_TILE, w), lambda i: (i, 0))
    tab = pl.BlockSpec((ROW_TILE, 128), lambda i: (i % BLOCKS_PER_SAMPLE, 0))
    vec = lambda w: pl.BlockSpec((1, w), lambda i: (0, 0))
    widths = (512, 256, 256, 256, 256, 512, 512, 256)
    return pl.pallas_call(
        _prep_kernel,
        grid=(R // ROW_TILE,),
        in_specs=[row(IN_PAD), tab, tab, tab, tab, vec(128), vec(128), vec(512), vec(256)],
        out_specs=[row(w) for w in widths] + [row(128)],
        out_shape=[jax.ShapeDtypeStruct((R, w), BF16) for w in widths] + [jax.ShapeDtypeStruct((R, 128), F32)],
        compiler_params=_cp(("parallel",)),
        name="prep_branches",
    )(P, jnp.asarray(c128), jnp.asarray(s128), jnp.asarray(c64), jnp.asarray(s64),
      gqa_q_norm.reshape(1, -1), gqa_k_norm.reshape(1, -1), mla_q_norm.reshape(1, -1), mla_kv_norm.reshape(1, -1))


def _mla_up_kernel(cq_ref, ckv_ref, kr_ref, wq_ref, wkv_ref, cos_ref, sin_ref, q_ref, k_ref, v_ref):
    scale = (MLA_NOPE_DIM + MLA_ROPE_DIM) ** -0.5 * LOG2E
    q = jnp.dot(cq_ref[...], wq_ref[...], preferred_element_type=F32)
    cos, sin = cos_ref[...], sin_ref[...]
    kv = jnp.dot(ckv_ref[...], wkv_ref[...], preferred_element_type=F32)
    krp = jnp.concatenate([jnp.zeros((ROW_TILE, MLA_NOPE_DIM), F32), kr_ref[...]], -1)
    for h in range(MLA_HEADS):
        sl = slice(h * MLA_QK_PAD, (h + 1) * MLA_QK_PAD)
        q_ref[:, sl] = (_rope(q[:, sl], cos, sin, MLA_ROPE_DIM // 4) * scale).astype(BF16)
        k_ref[:, sl] = (kv[:, sl] + krp).astype(BF16)
    v_ref[...] = kv[:, MLA_HEADS * MLA_QK_PAD:].astype(BF16)


def mla_up(cqn, ckvn, krp, wq_pad, wkv_pad):
    R = cqn.shape[0]
    cos, sin = _mla_q_tables()
    row = lambda w: pl.BlockSpec((ROW_TILE, w), lambda i: (i, 0))
    full = lambda a: pl.BlockSpec(a.shape, lambda i: (0, 0))
    tab = pl.BlockSpec((ROW_TILE, MLA_QK_PAD), lambda i: (i % BLOCKS_PER_SAMPLE, 0))
    hq = MLA_HEADS * MLA_QK_PAD
    return pl.pallas_call(
        _mla_up_kernel,
        grid=(R // ROW_TILE,),
        in_specs=[row(MLA_Q_LORA), row(MLA_KV_LORA), row(128), full(wq_pad), full(wkv_pad), tab, tab],
        out_specs=[row(hq), row(hq), row(MLA_HEADS * MLA_V_DIM)],
        out_shape=[jax.ShapeDtypeStruct((R, hq), BF16), jax.ShapeDtypeStruct((R, hq), BF16),
                   jax.ShapeDtypeStruct((R, MLA_HEADS * MLA_V_DIM), BF16)],
        compiler_params=_cp(("parallel",)),
        name="mla_up",
    )(cqn, ckvn, krp, wq_pad, wkv_pad, jnp.asarray(cos), jnp.asarray(sin))


def _attn_kernel(q_ref, k_ref, v_ref, o_ref, *, groups, dk, dv):
    k = k_ref[0]
    v = v_ref[0]
    v1 = jnp.concatenate([v, jnp.ones_like(v)], -1)
    tq = q_ref.shape[1]
    q = jnp.concatenate([q_ref[0, :, g * dk:(g + 1) * dk] for g in range(groups)], 0)
    s = lax.dot_general(q, k, (((1,), (1,)), ((), ())), preferred_element_type=F32)
    for g in range(groups):
        for half in range(tq // ATTN_SUB):
            r0 = g * tq + half * ATTN_SUB
            sh = s[r0:r0 + ATTN_SUB]
            p = jnp.exp2(sh - jnp.max(sh, -1, keepdims=True)).astype(BF16)
            o = jnp.dot(p, v1, preferred_element_type=F32)
            rows = slice(half * ATTN_SUB, (half + 1) * ATTN_SUB)
            o_ref[0, rows, g * dv:(g + 1) * dv] = (o[:, :dv] / o[:, dv:]).astype(o_ref.dtype)


def attention(q, k, v, *, kv_heads, groups, dk, dv, q_rows, q_off, kv_rows, kv_off, tq):
    B = q.shape[0]
    assert tq % ATTN_SUB == 0 and q_off % tq == 0 and q_rows % tq == 0
    qb0, kb0 = q_off // tq, kv_off // kv_rows
    return pl.pallas_call(
        functools.partial(_attn_kernel, groups=groups, dk=dk, dv=dv),
        grid=(B, kv_heads, q_rows // tq),
        in_specs=[pl.BlockSpec((1, tq, groups * dk), lambda b, h, i: (b, qb0 + i, h)),
                  pl.BlockSpec((1, kv_rows, dk), lambda b, h, i: (b, kb0, h)),
                  pl.BlockSpec((1, kv_rows, dv), lambda b, h, i: (b, kb0, h))],
        out_specs=pl.BlockSpec((1, tq, groups * dv), lambda b, h, i: (b, i, h)),
        out_shape=jax.ShapeDtypeStruct((B, q_rows, kv_heads * groups * dv), BF16),
        compiler_params=_cp(("parallel", "parallel", "arbitrary")),
        name="attention",
    )(q, k, v)


def _ret_kernel(lgf_ref, lgb_ref, q_ref, kt_ref, v_ref, g_ref, gain_ref, o_ref, sb_ref, *, n_out):
    L = RET_CHUNK
    nc = SEQ // L
    pair = pl.program_id(1)
    r_i = lax.broadcasted_iota(jnp.int32, (L, L), 0)
    c_i = lax.broadcasted_iota(jnp.int32, (L, L), 1)
    diff = (r_i - c_i).astype(F32)
    pos_col = lax.broadcasted_iota(jnp.int32, (L, 1), 0).astype(F32)
    pos_row = lax.broadcasted_iota(jnp.int32, (1, L), 1).astype(F32)
    for j in range(2):
        lgf = lgf_ref[pair * 2 + j]
        lgb = lgb_ref[pair * 2 + j]
        dmat = jnp.where(diff >= 0, jnp.exp(lgf * jnp.maximum(diff, 0.0)), jnp.exp(lgb * jnp.maximum(-diff, 0.0)))
        dq_f = jnp.exp(lgf * (pos_col + 1.0))
        dq_b = jnp.exp(lgb * (L - pos_col))
        dk_f = jnp.exp(lgf * (L - 1.0 - pos_row))
        dk_b = jnp.exp(lgb * pos_row)
        dc_f = jnp.exp(lgf * L)
        dc_b = jnp.exp(lgb * L)
        qs = slice(j * RET_QK_DIM, (j + 1) * RET_QK_DIM)
        vs = slice(j * RET_V_DIM, (j + 1) * RET_V_DIM)

        def chunk(c):
            rows = slice(c * L, (c + 1) * L)
            return q_ref[0, rows, qs], kt_ref[0, qs, rows], v_ref[0, rows, vs]

        def readout(o, c, out_row0):
            oc = o - jnp.mean(o, -1, keepdims=True)
            on = oc * lax.rsqrt(jnp.mean(oc * oc, -1, keepdims=True) + NORM_EPS)
            g = g_ref[0, c * L:(c + 1) * L, vs]
            o_ref[0, out_row0:out_row0 + L, vs] = (on * gain_ref[:, vs] * (g * _sigmoid(g))).astype(o_ref.dtype)

        def intra(q, kt, v):
            sc = jnp.dot(q, kt, preferred_element_type=F32) * dmat
            return jnp.dot(sc.astype(BF16), v, preferred_element_type=F32)

        def state_add(kt, dk, v):
            return jnp.dot((kt.astype(F32) * dk).astype(BF16), v, preferred_element_type=F32)

        qc, ktc, vc = chunk(nc)
        if n_out > SEQ:
            readout(intra(qc, ktc, vc), nc, SEQ)
        s_f = state_add(ktc, dk_f, vc)
        s_b = state_add(ktc, dk_b, vc)
        for c in range(nc - 1, -1, -1):
            sb_ref[c] = s_b
            _, kt, v = chunk(c)
            s_b = s_b * dc_b + state_add(kt, dk_b, v)
        for c in range(nc):
            q, kt, v = chunk(c)
            qf = q.astype(F32)
            o = (intra(q, kt, v)
                 + jnp.dot((qf * dq_f).astype(BF16), s_f.astype(BF16), preferred_element_type=F32)
                 + jnp.dot((qf * dq_b).astype(BF16), sb_ref[c].astype(BF16), preferred_element_type=F32))
            readout(o, c, c * L)
            s_f = s_f * dc_f + state_add(kt, dk_f, v)


def retention(rq, rkt, rv, P3, gain, lgf, lgb, n_out):
    B = rq.shape[0]
    smem = pl.BlockSpec(memory_space=pltpu.SMEM)
    gcol = OFF_RG // 256
    return pl.pallas_call(
        functools.partial(_ret_kernel, n_out=n_out),
        grid=(B, RET_HEADS // 2),
        in_specs=[smem, smem,
                  pl.BlockSpec((1, S_ALL, 128), lambda b, p: (b, 0, p)),
                  pl.BlockSpec((1, 128, S_ALL), lambda b, p: (b, p, 0)),
                  pl.BlockSpec((1, S_ALL, 256), lambda b, p: (b, 0, p)),
                  pl.BlockSpec((1, S_ALL, 256), lambda b, p: (b, 0, gcol + p)),
                  pl.BlockSpec((1, 256), lambda b, p: (0, p))],
        out_specs=pl.BlockSpec((1, n_out, 256), lambda b, p: (b, 0, p)),
        out_shape=jax.ShapeDtypeStruct((B, n_out, RET_HEADS * RET_V_DIM), BF16),
        scratch_shapes=[pltpu.VMEM((SEQ // RET_CHUNK, RET_QK_DIM, RET_V_DIM), F32)],
        compiler_params=_cp(("parallel", "parallel")),
        name="retention",
    )(lgf, lgb, rq, rkt, rv, P3, gain.reshape(1, -1))


def _s5_kernel(u_ref, bb_ref, lam_ref, cm_ref, y_ref, bu_ref, pw_ref, st_ref):
    M = S5_MODES
    chunk = pl.program_id(2)
    lam_re = lam_ref[0, 0]
    lam_im = lam_ref[0, 1]

    @pl.when(chunk == 0)
    def _():
        st_ref[...] = jnp.zeros_like(st_ref)
        p_re, p_im = lam_re, lam_im
        for j in range(S5_STEPS):
            pw_ref[0, j] = p_re
            pw_ref[1, j] = p_im
            p_re, p_im = p_re * lam_re - p_im * lam_im, p_re * lam_im + p_im * lam_re

    r_i = lax.broadcasted_iota(jnp.int32, (ROW_TILE, ROW_TILE), 0)
    c_i = lax.broadcasted_iota(jnp.int32, (ROW_TILE, ROW_TILE), 1)
    flip = pl.program_id(0) == 1

    def scan_time(r):
        t = (r % S5_SEG) * S5_STEPS + r // S5_SEG
        return jnp.where(flip, ROW_TILE - 1 - t, t)

    to_scan = (c_i == scan_time(r_i)).astype(BF16)
    to_time = (r_i == scan_time(c_i)).astype(BF16)
    u = jnp.dot(to_scan, u_ref[...].astype(BF16), preferred_element_type=F32).astype(BF16)

    slab = BRANCH_W // S5_SLABS
    ms = M // S5_SLABS
    for s in range(S5_SLABS):
        part = jnp.dot(u[:, s * slab:(s + 1) * slab], bb_ref[0, s], preferred_element_type=F32)
        bu_ref[:, s * ms:(s + 1) * ms] = part[:, :ms]
        bu_ref[:, M + s * ms:M + (s + 1) * ms] = part[:, ms:]

    tile = 512
    for t in range(M // tile):
        re_sl = slice(t * tile, (t + 1) * tile)
        im_sl = slice(M + t * tile, M + (t + 1) * tile)
        lr, li = lam_re[:, re_sl], lam_im[:, re_sl]

        def step(j, carry):
            s_re, s_im = carry
            rows = pl.ds(pl.multiple_of(j * S5_SEG, S5_SEG), S5_SEG)
            n_re = lr * s_re - li * s_im + bu_ref[rows, re_sl]
            n_im = lr * s_im + li * s_re + bu_ref[rows, im_sl]
            bu_ref[rows, re_sl] = n_re
            bu_ref[rows, im_sl] = n_im
            return n_re, n_im

        z = jnp.zeros((S5_SEG, tile), F32)
        lax.fori_loop(0, S5_STEPS, step, (z, z))

    last = slice((S5_STEPS - 1) * S5_SEG, S5_STEPS * S5_SEG)
    e_re, e_im = bu_ref[last, 0:M], bu_ref[last, M:2 * M]
    pl_re, pl_im = pw_ref[0, S5_STEPS - 1][0:1], pw_ref[1, S5_STEPS - 1][0:1]
    c_re, c_im = st_ref[0:1, :], st_ref[1:2, :]
    rows_re, rows_im = [], []
    for k in range(S5_SEG):
        rows_re.append(c_re)
        rows_im.append(c_im)
        c_re, c_im = (e_re[k:k + 1] + pl_re * c_re - pl_im * c_im,
                      e_im[k:k + 1] + pl_re * c_im + pl_im * c_re)
    st_ref[0:1, :] = c_re
    st_ref[1:2, :] = c_im
    car_re = jnp.concatenate(rows_re, 0)
    car_im = jnp.concatenate(rows_im, 0)

    def fix(j, _):
        rows = pl.ds(pl.multiple_of(j * S5_SEG, S5_SEG), S5_SEG)
        p_re, p_im = pw_ref[0, j], pw_ref[1, j]
        bu_ref[rows, 0:M] = bu_ref[rows, 0:M] + p_re * car_re - p_im * car_im
        bu_ref[rows, M:2 * M] = bu_ref[rows, M:2 * M] + p_re * car_im + p_im * car_re
        return 0

    lax.fori_loop(0, S5_STEPS, fix, 0)

    ys = []
    for s in range(S5_SLABS):
        hs = jnp.concatenate([bu_ref[:, s * ms:(s + 1) * ms], bu_ref[:, M + s * ms:M + (s + 1) * ms]], -1)
        ys.append(jnp.dot(hs.astype(BF16), cm_ref[s], preferred_element_type=F32))
    y = jnp.concatenate(ys, -1)
    out = None
    for _ in range(2):
        piece = y.astype(BF16)
        y = y - piece.astype(F32)
        term = jnp.dot(to_time, piece, preferred_element_type=F32)
        out = term if out is None else out + term
    y_ref[0] = out


def _s5_block(d, c):
    fwd = (c + LAT_BLOCKS) % BLOCKS_PER_SAMPLE
    bwd = jnp.where(c == 0, LAT_BLOCKS, LAT_BLOCKS - c)
    return jnp.where(d == 0, fwd, bwd)


def s5_scan(P, bb, lam, cmat):
    R = P.shape[0]
    B = R // S_ALL
    W = BRANCH_W
    row_block = lambda d, b, c: b * BLOCKS_PER_SAMPLE + _s5_block(d, c)
    return pl.pallas_call(
        _s5_kernel,
        grid=(2, B, BLOCKS_PER_SAMPLE),
        in_specs=[pl.BlockSpec((ROW_TILE, W), lambda d, b, c: (row_block(d, b, c), 0)),
                  pl.BlockSpec((1,) + bb.shape[1:], lambda d, b, c: (d, 0, 0, 0)),
                  pl.BlockSpec((1, 2, S5_SEG, S5_MODES), lambda d, b, c: (d, 0, 0, 0)),
                  pl.BlockSpec(cmat.shape, lambda d, b, c: (0, 0, 0))],
        out_specs=pl.BlockSpec((1, ROW_TILE, W), lambda d, b, c: (d, row_block(d, b, c), 0)),
        out_shape=jax.ShapeDtypeStruct((2, R, W), F32),
        scratch_shapes=[pltpu.VMEM((ROW_TILE, 2 * S5_MODES), F32),
                        pltpu.VMEM((2, S5_STEPS, S5_SEG, S5_MODES), F32),
                        pltpu.VMEM((8, S5_MODES), F32)],
        compiler_params=_cp(("arbitrary", "arbitrary", "arbitrary")),
        name="s5_scan",
    )(P, bb, lam, cmat)


def _s5_out_kernel(yf_ref, yb_ref, p_ref, d_ref, w_ref, o_ref):
    y = yf_ref[0] + yb_ref[0] + d_ref[...] * p_ref[...]
    z = 0.5 * y * (1.0 + jnp.tanh(math.sqrt(2.0 / math.pi) * (y + 0.044715 * (y * y * y))))
    gate = _sigmoid(jnp.dot(z.astype(BF16), w_ref[...], preferred_element_type=F32))
    o_ref[...] = (z * gate).astype(o_ref.dtype)


def s5_output(y_dirs, P, d, w_glu):
    R = y_dirs.shape[1]
    return pl.pallas_call(
        _s5_out_kernel,
        grid=(R // ROW_TILE,),
        in_specs=[pl.BlockSpec((1, ROW_TILE, 512), lambda i: (0, i, 0)),
                  pl.BlockSpec((1, ROW_TILE, 512), lambda i: (1, i, 0)),
                  pl.BlockSpec((ROW_TILE, 512), lambda i: (i, 0)),
                  pl.BlockSpec((1, 512), lambda i: (0, 0)),
                  pl.BlockSpec((512, 512), lambda i: (0, 0))],
        out_specs=pl.BlockSpec((ROW_TILE, 512), lambda i: (i, 0)),
        out_shape=jax.ShapeDtypeStruct((R, 512), BF16),
        compiler_params=_cp(("parallel",)),
        name="s5_output",
    )(y_dirs, y_dirs, P, d.reshape(1, -1), w_glu)


def _merge_kernel(h_ref, o0_ref, o1_ref, o2_ref, o3_ref, g0_ref, g1_ref, g2_ref, g3_ref,
                  bg_ref, wb_ref, out_ref, wg_ref):
    @pl.when(pl.program_id(1) == 0)
    def _():
        for k, g in enumerate((g0_ref, g1_ref, g2_ref, g3_ref)):
            wg_ref[k] = g[0].astype(BF16)

    h = h_ref[...]
    acc = None
    for k, o in enumerate((o0_ref, o1_ref, o2_ref, o3_ref)):
        gate = _sigmoid(jnp.dot(h, wg_ref[k], preferred_element_type=F32) + bg_ref[k])
        term = gate * jnp.dot(o[...], wb_ref[k], preferred_element_type=F32)
        acc = term if acc is None else acc + term
    out_ref[...] = acc.astype(out_ref.dtype)


def merge_branches(h, outs, w_gate_all, layer, b_gate, w_branch, tn=256, tm=MM_ROW_TILE):
    R, D = h.shape
    nb = D // tn
    gate_spec = lambda k: pl.BlockSpec((1, D, tn), lambda n, m: (layer, 0, k * nb + n))
    bg = b_gate.reshape(N_BRANCH, 1, D)
    return pl.pallas_call(
        _merge_kernel,
        grid=(nb, R // tm),
        in_specs=[pl.BlockSpec((tm, D), lambda n, m: (m, 0))]
                 + [pl.BlockSpec((tm, BRANCH_W), lambda n, m: (m, 0))] * N_BRANCH
                 + [gate_spec(k) for k in range(N_BRANCH)]
                 + [pl.BlockSpec((N_BRANCH, 1, tn), lambda n, m: (0, 0, n)),
                    pl.BlockSpec((N_BRANCH, BRANCH_W, tn), lambda n, m: (0, 0, n))],
        out_specs=pl.BlockSpec((tm, tn), lambda n, m: (m, n)),
        out_shape=jax.ShapeDtypeStruct((R, D), BF16),
        scratch_shapes=[pltpu.VMEM((N_BRANCH, D, tn), BF16)],
        compiler_params=_cp(("arbitrary", "arbitrary")),
        name="merge_branches",
    )(h, *outs, w_gate_all, w_gate_all, w_gate_all, w_gate_all, bg, w_branch)


def _out_kernel(m_ref, x_ref, w_ref, g1_ref, lg_ref, lb_ref, sh_ref, sc_ref, rw_ref,
                x1_ref, h2_ref, lo_ref, ya_ref, yb_ref):
    step = pl.program_id(0)

    @pl.when(step == 0)
    def _():
        yb_ref[...] = jnp.zeros_like(yb_ref)

    def body(y_prev_ref, y_next_ref):
        y_next_ref[...] = jnp.dot(m_ref[...], w_ref[...], preferred_element_type=F32)
        x1 = _standardize(DEEPNORM_ALPHA * x_ref[...] + g1_ref[0] * y_prev_ref[...]) * lg_ref[...] + lb_ref[...]
        x1_ref[...] = x1
        h2 = _standardize(x1) * (1.0 + sc_ref[0]) + sh_ref[0]
        _store_token_tiles(h2_ref, h2)
        rw = rw_ref[...]
        rw_hi = rw.astype(BF16)
        rw_lo = (rw - rw_hi.astype(F32)).astype(BF16)
        h_hi = h2.astype(BF16)
        h_lo = (h2 - h_hi.astype(F32)).astype(BF16)
        lo_ref[...] = (jnp.dot(h_hi, rw_hi, preferred_element_type=F32)
                       + jnp.dot(h_lo, rw_hi, preferred_element_type=F32)
                       + jnp.dot(h_hi, rw_lo, preferred_element_type=F32))

    @pl.when(step % 2 == 0)
    def _():
        body(yb_ref, ya_ref)

    @pl.when(step % 2 == 1)
    def _():
        body(ya_ref, yb_ref)


def out_proj_norm(merged, x, w_out, modblk, ln_g, ln_b, router_pad):
    R, D = x.shape
    tt = D // LANES
    nblk = R // ROW_TILE
    cur = lambda i: jnp.minimum(i, nblk - 1)
    prev = lambda i: jnp.maximum(i - 1, 0)
    row = lambda w: pl.BlockSpec((ROW_TILE, w), lambda i: (prev(i), 0))
    mod = lambda part: pl.BlockSpec((1, 1, D), lambda i: (prev(i), 0, part))
    vec = pl.BlockSpec((1, D), lambda i: (0, 0))
    return pl.pallas_call(
        _out_kernel,
        grid=(nblk + 1,),
        in_specs=[pl.BlockSpec((ROW_TILE, D), lambda i: (cur(i), 0)), row(D),
                  pl.BlockSpec((D, D), lambda i: (0, 0)), mod(2), vec, vec, mod(3), mod(4),
                  pl.BlockSpec((D, 128), lambda i: (0, 0))],
        out_specs=[row(D), pl.BlockSpec((ROW_TILE * tt, LANES), lambda i: (prev(i), 0)), row(128)],
        out_shape=[jax.ShapeDtypeStruct((R, D), F32), jax.ShapeDtypeStruct((R * tt, LANES), F32),
                   jax.ShapeDtypeStruct((R, 128), F32)],
        scratch_shapes=[pltpu.VMEM((ROW_TILE, D), F32), pltpu.VMEM((ROW_TILE, D), F32)],
        compiler_params=_cp(("arbitrary",)),
        name="out_proj_norm",
    )(merged, x, w_out, modblk, ln_g.reshape(1, -1), ln_b.reshape(1, -1), modblk, modblk, router_pad)


ROUTE_DIGIT = 64
ROUTE_GEOMETRIC_STEPS = 40
ROUTE_BISECT_STEPS = ROUTE_GEOMETRIC_STEPS + 8


def _route_kernel(lt_ref, idx_ref, gate_ref, slot_ref, *, n, cap):
    E = N_EXPERTS
    lt = lt_ref[0]
    ex = jnp.exp(lt - jnp.max(lt, 0, keepdims=True))
    aff = ex / jnp.sum(ex, 0, keepdims=True)

    def count_ge(v):
        return jnp.sum((aff >= v).astype(F32), 1, keepdims=True)

    tiny = jnp.full((E, 1), float(np.finfo(np.float32).tiny), F32)
    normal = count_ge(tiny) >= cap
    lo0 = jnp.where(normal, tiny, 0.0)
    hi0 = jnp.where(normal, 2.0, tiny)

    def bisect(i, lo_hi):
        lo, hi = lo_hi
        geo = jnp.clip(jnp.sqrt(lo) * jnp.sqrt(hi), lo, hi)
        mid = jnp.where(jnp.logical_and(i < ROUTE_GEOMETRIC_STEPS, lo > 0.0), geo, lo + 0.5 * (hi - lo))
        ok = count_ge(mid) >= cap
        return jnp.where(ok, mid, lo), jnp.where(ok, hi, mid)

    lo, _ = lax.fori_loop(0, ROUTE_BISECT_STEPS, bisect, (lo0, hi0))
    thr = jnp.min(jnp.where(aff >= lo, aff, 2.0), 1, keepdims=True)
    gt = aff > thr
    eq = aff == thr
    need = cap - jnp.sum(gt.astype(F32), 1, keepdims=True)

    blk = min(n, 512)
    upper = (lax.broadcasted_iota(jnp.int32, (blk, blk), 0)
             < lax.broadcasted_iota(jnp.int32, (blk, blk), 1)).astype(BF16)

    def excl_cumsum(mask):
        parts, carry = [], jnp.zeros((E, 1), F32)
        for j in range(n // blk):
            m = mask[:, j * blk:(j + 1) * blk].astype(F32)
            parts.append(jnp.dot(m.astype(BF16), upper, preferred_element_type=F32) + carry)
            carry = carry + jnp.sum(m, 1, keepdims=True)
        return parts[0] if len(parts) == 1 else jnp.concatenate(parts, 1)

    sel = gt | (eq & (excl_cumsum(eq) < need))
    slot_ref[...] = jnp.where(sel, excl_cumsum(sel), -1.0)
    idx_ref[0] = jnp.zeros((cap, 128), F32)
    gate_ref[0] = jnp.zeros((cap, 128), F32)

    rows = min(cap, 64)
    lane_e = lax.broadcasted_iota(jnp.int32, (rows, 128), 1)
    tok = lax.broadcasted_iota(jnp.int32, (1, n), 1)
    tok_hi = (tok // ROUTE_DIGIT).astype(F32)
    tok_lo = (tok % ROUTE_DIGIT).astype(F32)
    pad_rows = jnp.zeros((16 - 5, n), F32)

    for e in range(E):
        a0 = aff[e:e + 1].astype(BF16).astype(F32)
        a1 = (aff[e:e + 1] - a0).astype(BF16).astype(F32)
        a2 = aff[e:e + 1] - a0 - a1
        cols = jnp.concatenate([tok_hi, tok_lo, a0, a1, a2, pad_rows], 0).astype(BF16)

        def per_rows(c, _, e=e, cols=cols):
            r0 = pl.multiple_of(c * rows, rows)
            s_col = (lax.broadcasted_iota(jnp.int32, (rows, 1), 0) + r0).astype(F32)
            onehot = jnp.concatenate(
                [jnp.where(slot_ref[e:e + 1, j * 128:(j + 1) * 128] == s_col, 1.0, 0.0).astype(BF16)
                 for j in range(n // 128)], -1)
            res = lax.dot_general(onehot, cols, (((1,), (1,)), ((), ())), preferred_element_type=F32)
            icol = res[:, 0:1] * float(ROUTE_DIGIT) + res[:, 1:2]
            gcol = res[:, 2:3] + res[:, 3:4] + res[:, 4:5]
            idx_ref[0, pl.ds(r0, rows), :] = jnp.where(lane_e == e, icol, idx_ref[0, pl.ds(r0, rows), :])
            gate_ref[0, pl.ds(r0, rows), :] = jnp.where(lane_e == e, gcol, gate_ref[0, pl.ds(r0, rows), :])
            return 0

        lax.fori_loop(0, cap // rows, per_rows, 0)


def route(logits_t, off, n):
    B = logits_t.shape[0]
    cap = EC_CAPACITY_FACTOR * n // N_EXPERTS
    out = pl.BlockSpec((1, cap, 128), lambda b: (b, 0, 0))
    return pl.pallas_call(
        functools.partial(_route_kernel, n=n, cap=cap),
        grid=(B,),
        in_specs=[pl.BlockSpec((1, N_EXPERTS, n), lambda b: (b, 0, off // n))],
        out_specs=[out, out],
        out_shape=[jax.ShapeDtypeStruct((B, cap, 128), F32)] * 2,
        scratch_shapes=[pltpu.VMEM((N_EXPERTS, n), F32)],
        compiler_params=_cp(("parallel",)),
        name="route",
    )(logits_t)


GATHER_UNROLL = 8


def _gather_kernel(rows_ref, h_hbm, o_ref, land_ref, sem, *, T, tt):
    def tile_copy(s, r):
        src = h_hbm.at[pl.ds(pl.multiple_of(r * tt, tt), tt), :]
        dst = land_ref.at[pl.ds(pl.multiple_of(s * tt, tt), tt), :]
        return pltpu.make_async_copy(src, dst, sem)

    def issue(g, _):
        for i in range(GATHER_UNROLL):
            s = g * GATHER_UNROLL + i
            tile_copy(s, rows_ref[0, 0, s]).start()
        return 0

    lax.fori_loop(0, T // GATHER_UNROLL, issue, 0)

    def drain(g, _):
        for i in range(GATHER_UNROLL):
            tile_copy(g * GATHER_UNROLL + i, 0).wait()
        return 0

    lax.fori_loop(0, T // GATHER_UNROLL, drain, 0)
    o_ref[0] = _load_token_tiles(land_ref, T, tt * LANES, BF16)


def gather_rows(rows, h_tiles, D):
    E, _, T = rows.shape
    tt = D // LANES
    return pl.pallas_call(
        functools.partial(_gather_kernel, T=T, tt=tt),
        grid=(E,),
        in_specs=[pl.BlockSpec((1, 1, T), lambda e: (e, 0, 0), memory_space=pltpu.SMEM),
                  pl.BlockSpec(memory_space=pl.ANY)],
        out_specs=pl.BlockSpec((1, T, D), lambda e: (e, 0, 0)),
        out_shape=jax.ShapeDtypeStruct((E, T, D), BF16),
        scratch_shapes=[pltpu.VMEM((T * tt, LANES), F32), pltpu.SemaphoreType.DMA],
        compiler_params=_cp(("arbitrary",)),
        name="gather_rows",
    )(rows, h_tiles)


def _expert_kernel(x_ref, gate_ref, wg_ref, wu_ref, wd_ref, o_ref, hid_ref, *, nf):
    s = pl.program_id(1)

    @pl.when(s < nf)
    def _():
        x = x_ref[0]
        a = jnp.dot(x, wg_ref[0, 0].astype(BF16), preferred_element_type=F32)
        u = jnp.dot(x, wu_ref[0, 0].astype(BF16), preferred_element_type=F32)
        hid_ref[s] = (a * _sigmoid(a) * u).astype(BF16)

    @pl.when(s >= nf)
    def _():
        hid = jnp.concatenate([hid_ref[f] for f in range(nf)], -1)
        y = jnp.dot(hid, wd_ref[0, 0].astype(BF16), preferred_element_type=F32)
        o_ref[0] = y * gate_ref[0]


def expert_ffn(xs, gate, w_gate, w_up, w_down, layer, tf=256, td=512):
    E, T, D = xs.shape
    FF = w_gate.shape[-1]
    nf, nd = FF // tf, D // td
    up_tile = lambda e, s: (layer, e, 0, jnp.minimum(s, nf - 1))
    down_tile = lambda s: jnp.maximum(s - nf, 0)
    return pl.pallas_call(
        functools.partial(_expert_kernel, nf=nf),
        grid=(E, nf + nd),
        in_specs=[pl.BlockSpec((1, T, D), lambda e, s: (e, 0, 0)),
                  pl.BlockSpec((1, T, 1), lambda e, s: (e, 0, 0)),
                  pl.BlockSpec((1, 1, D, tf), up_tile),
                  pl.BlockSpec((1, 1, D, tf), up_tile),
                  pl.BlockSpec((1, 1, FF, td), lambda e, s: (layer, e, 0, down_tile(s)))],
        out_specs=pl.BlockSpec((1, T, td), lambda e, s: (e, 0, down_tile(s))),
        out_shape=jax.ShapeDtypeStruct((E, T, D), F32),
        scratch_shapes=[pltpu.VMEM((nf, T, tf), BF16)],
        compiler_params=_cp(("parallel", "arbitrary")),
        name="expert_ffn",
    )(xs, gate, w_gate, w_up, w_down)


COMBINE_UNROLL = 8


def _combine_kernel(*refs, caps, bases, dh):
    n_sets = len(caps)
    idx_refs, y_refs = refs[:n_sets], refs[n_sets:2 * n_sets]
    o_ref, yt_ref = refs[2 * n_sets], refs[2 * n_sets + 1]
    tt = dh // LANES

    @pl.when(pl.program_id(2) == 0)
    def _():
        o_ref[...] = jnp.zeros_like(o_ref)

    for idx_ref, y_ref, cap, base in zip(idx_refs, y_refs, caps, bases):
        for j in range(tt):
            yt_ref[pl.ds(j, cap, stride=tt), :] = y_ref[0, :, j * LANES:(j + 1) * LANES]

        def body(g, _, idx_ref=idx_ref, base=base):
            s0 = g * COMBINE_UNROLL
            toks = [idx_ref[0, 0, 0, s0 + i] + base for i in range(COMBINE_UNROLL)]
            sums = [o_ref[0, toks[i], 0] + yt_ref[pl.ds(pl.multiple_of((s0 + i) * tt, tt), tt), :]
                    for i in range(COMBINE_UNROLL)]
            for i in range(COMBINE_UNROLL):
                o_ref[0, toks[i], 0] = sums[i]
            return 0

        lax.fori_loop(0, cap // COMBINE_UNROLL, body, 0)


def combine(ys, idx_sets, bases, dh=SUBLANES * LANES):
    E, T, D = ys.shape
    B = idx_sets[0].shape[0]
    caps = tuple(int(i.shape[-1]) for i in idx_sets)
    starts = np.concatenate([[0], np.cumsum([B * c for c in caps])[:-1]])
    idx_specs = [pl.BlockSpec((1, 1, 1, c), lambda b, hf, e: (b, e, 0, 0), memory_space=pltpu.SMEM) for c in caps]
    y_specs = [pl.BlockSpec((1, c, dh), lambda b, hf, e, blk0=int(st) // c: (e, blk0 + b, hf))
               for c, st in zip(caps, starts)]
    return pl.pallas_call(
        functools.partial(_combine_kernel, caps=caps, bases=tuple(bases), dh=dh),
        grid=(B, D // dh, E),
        in_specs=idx_specs + y_specs,
        out_specs=pl.BlockSpec((1, S_ALL, 1, SUBLANES, LANES), lambda b, hf, e: (b, 0, hf, 0, 0)),
        out_shape=jax.ShapeDtypeStruct((B, S_ALL, D // dh, SUBLANES, LANES), F32),
        scratch_shapes=[pltpu.VMEM((max(caps) * dh // LANES, LANES), F32)],
        compiler_params=_cp(("parallel", "parallel", "arbitrary")),
        name="combine",
    )(*[i[:, :, None, :] for i in idx_sets], *([ys] * len(caps)))


def _post_kernel(x_ref, y_ref, g2_ref, lg_ref, lb_ref, sh_ref, sc_ref, x2_ref, h_ref):
    y = _load_token_tiles(y_ref, ROW_TILE, x_ref.shape[1])
    x2 = _standardize(DEEPNORM_ALPHA * x_ref[...] + g2_ref[0] * y) * lg_ref[...] + lb_ref[...]
    x2_ref[...] = x2
    h_ref[...] = (_standardize(x2) * (1.0 + sc_ref[0]) + sh_ref[0]).astype(BF16)


def post_moe_norm(x1, y_tiles, modblk, ln_g, ln_b, modblk_next):
    R, D = x1.shape
    row = pl.BlockSpec((ROW_TILE, D), lambda i: (i, 0))
    mod = lambda part: pl.BlockSpec((1, 1, D), lambda i: (i, 0, part))
    vec = pl.BlockSpec((1, D), lambda i: (0, 0))
    y = pl.BlockSpec((ROW_TILE * D // LANES, LANES), lambda i: (i, 0))
    return pl.pallas_call(
        _post_kernel,
        grid=(R // ROW_TILE,),
        in_specs=[row, y, mod(5), vec, vec, mod(0), mod(1)],
        out_specs=[row, row],
        out_shape=[jax.ShapeDtypeStruct((R, D), F32), jax.ShapeDtypeStruct((R, D), BF16)],
        compiler_params=_cp(("parallel",)),
        name="post_moe_norm",
    )(x1, y_tiles, modblk, ln_g.reshape(1, -1), ln_b.reshape(1, -1), modblk_next, modblk_next)


def _final_kernel(x_ref, y_ref, g2_ref, lg_ref, lb_ref, x2_ref):
    y = _load_token_tiles(y_ref, ROW_TILE, x_ref.shape[1])
    x2_ref[...] = _standardize(DEEPNORM_ALPHA * x_ref[...] + g2_ref[0] * y) * lg_ref[...] + lb_ref[...]


def final_norm(x1, y_tiles, modblk, ln_g, ln_b):
    R, D = x1.shape
    n_lat = R // S_ALL * LAT_BLOCKS
    blk = lambda i: i + i // LAT_BLOCKS
    vec = pl.BlockSpec((1, D), lambda i: (0, 0))
    return pl.pallas_call(
        _final_kernel,
        grid=(n_lat,),
        in_specs=[pl.BlockSpec((ROW_TILE, D), lambda i: (blk(i), 0)),
                  pl.BlockSpec((ROW_TILE * D // LANES, LANES), lambda i: (blk(i), 0)),
                  pl.BlockSpec((1, 1, D), lambda i: (blk(i), 0, 5)), vec, vec],
        out_specs=pl.BlockSpec((ROW_TILE, D), lambda i: (i, 0)),
        out_shape=jax.ShapeDtypeStruct((n_lat * ROW_TILE, D), F32),
        compiler_params=_cp(("parallel",)),
        name="final_norm",
    )(x1, y_tiles, modblk, ln_g.reshape(1, -1), ln_b.reshape(1, -1))


def _s5_matrices(a_re, a_im, log_dt, b_re, b_im):
    dt = jnp.exp(log_dt)[:, None]
    mag = jnp.exp(a_re * dt)
    ab_re, ab_im = mag * jnp.cos(a_im * dt), mag * jnp.sin(a_im * dt)
    den = a_re * a_re + a_im * a_im
    num_re, num_im = ab_re - 1.0, ab_im
    coef_re = (num_re * a_re + num_im * a_im) / den
    coef_im = (num_im * a_re - num_re * a_im) / den
    bb_re = coef_re[..., None] * b_re - coef_im[..., None] * b_im
    bb_im = coef_re[..., None] * b_im + coef_im[..., None] * b_re
    gs = S5_GROUPS // S5_SLABS
    eye = jnp.eye(gs, dtype=F32)

    def slabs(t):
        t = t.reshape(S5_SLABS, gs, S5_STATE, S5_GROUP_CH)
        return jnp.einsum('sgpi,gh->sgihp', t, eye).reshape(S5_SLABS, gs * S5_GROUP_CH, gs * S5_STATE)

    lam = jnp.stack([ab_re.reshape(-1), ab_im.reshape(-1)], 0)
    return lam, jnp.concatenate([slabs(bb_re), slabs(bb_im)], -1)


def _s5_readout_matrix(c_re, c_im):
    gs = S5_GROUPS // S5_SLABS
    eye = jnp.eye(gs, dtype=F32)

    def slabs(t):
        t = t.reshape(S5_SLABS, gs, S5_GROUP_CH, S5_STATE)
        return jnp.einsum('sgip,gh->sgphi', t, eye).reshape(S5_SLABS, gs * S5_STATE, gs * S5_GROUP_CH)

    return jnp.concatenate([slabs(c_re), -slabs(c_im)], 1)


def _mla_weights(w_uq, w_ukv):
    qk = MLA_NOPE_DIM + MLA_ROPE_DIM
    wq = w_uq.reshape(MLA_Q_LORA, MLA_HEADS, qk)
    wq = jnp.pad(wq, ((0, 0), (0, 0), (0, MLA_QK_PAD - qk))).reshape(MLA_Q_LORA, MLA_HEADS * MLA_QK_PAD)
    wkv = w_ukv.reshape(MLA_KV_LORA, MLA_HEADS, MLA_NOPE_DIM + MLA_V_DIM)
    wk = jnp.pad(wkv[..., :MLA_NOPE_DIM], ((0, 0), (0, 0), (0, MLA_QK_PAD - MLA_NOPE_DIM)))
    wv = wkv[..., MLA_NOPE_DIM:]
    wkv_pad = jnp.concatenate([wk.reshape(MLA_KV_LORA, -1), wv.reshape(MLA_KV_LORA, -1)], 1)
    return wq.astype(BF16), wkv_pad.astype(BF16)


def kernel(x, c, ctx, c_ctx, ada_w, ada_b, w_in, s5_a_re_f, s5_a_im_f, s5_log_dt_f, s5_a_re_b, s5_a_im_b,
           s5_log_dt_b, s5_b_re, s5_b_im, s5_c_re, s5_c_im, s5_d, s5_w_glu, gqa_q_norm, gqa_k_norm,
           ret_decay_f, ret_decay_b, ret_norm, mla_q_norm, mla_kv_norm, mla_w_uq, mla_w_ukv,
           w_branch, w_gate, b_gate, w_out, ln1_g, ln1_b, router_w, moe_w_gate, moe_w_up, moe_w_down,
           ln2_g, ln2_b):
    B, N, D = x.shape
    R = B * S_ALL
    assert (B, N, D) == (BATCH, SEQ, D_MODEL) and RET_CHUNK == CTX_LEN == ROW_TILE

    cc = jnp.zeros((16, D), F32).at[:B].set(c).at[B].set(c_ctx)
    mod = ada_modulation(cc, ada_w, ada_b)
    sel = np.concatenate([np.r_[np.full(LAT_BLOCKS, b), B] for b in range(B)]).astype(np.int32)
    modblks = [mod[l][sel].reshape(R // ROW_TILE, 1, 6 * D) for l in range(DEPTH)]

    X = jnp.concatenate([x, ctx], 1).reshape(R, D)
    h = modulate_rows(X, modblks[0], 0, 1)

    for l in range(DEPTH):
        need_ctx = l < DEPTH - 1
        modblk = modblks[l]
        w_in_p = jnp.pad(w_in[l], ((0, 0), (0, IN_PAD - IN_TOTAL))).astype(BF16)
        P = matmul_bf16(h, w_in_p, tn=1024, name="in_proj")
        P3 = P.reshape(B, S_ALL, IN_PAD)
        gq, gk, gv, rq, rk, rv, cqn, ckvn, krp = prep_branches(P, gqa_q_norm[l], gqa_k_norm[l],
                                                               mla_q_norm[l], mla_kv_norm[l])
        to3 = lambda t: t.reshape(B, S_ALL, t.shape[-1])

        lam_f, bb_f = _s5_matrices(s5_a_re_f[l], s5_a_im_f[l], s5_log_dt_f[l], s5_b_re[l], s5_b_im[l])
        lam_b, bb_b = _s5_matrices(s5_a_re_b[l], s5_a_im_b[l], s5_log_dt_b[l], s5_b_re[l], s5_b_im[l])
        lam = jnp.broadcast_to(jnp.stack([lam_f, lam_b], 0)[:, :, None, :], (2, 2, S5_SEG, S5_MODES))
        bb = jnp.stack([bb_f, bb_b], 0).astype(BF16)
        cmat = _s5_readout_matrix(s5_c_re[l], s5_c_im[l]).astype(BF16)
        o_s5 = s5_output(s5_scan(P, bb, lam, cmat), P, s5_d[l], s5_w_glu[l].astype(BF16))

        gq3, gk3, gv3 = to3(gq), to3(gk), to3(gv)
        att = functools.partial(attention, kv_heads=GQA_KV_HEADS, groups=GQA_HEADS // GQA_KV_HEADS,
                                dk=GQA_HEAD_DIM, dv=GQA_HEAD_DIM)
        o_lat = att(gq3, gk3, gv3, q_rows=SEQ, q_off=0, kv_rows=S_ALL, kv_off=0, tq=256)
        no_ctx = jnp.zeros((B, CTX_LEN, BRANCH_W), BF16)
        o_ctx = (att(gq3, gk3, gv3, q_rows=CTX_LEN, q_off=SEQ, kv_rows=CTX_LEN, kv_off=SEQ, tq=256)
                 if need_ctx else no_ctx)
        o_gqa = jnp.concatenate([o_lat, o_ctx], 1).reshape(R, BRANCH_W)

        lgf = -jnp.exp(ret_decay_f[l])
        lgb = -jnp.exp(ret_decay_b[l])
        rkt = jnp.swapaxes(to3(rk), 1, 2)
        o_ret = retention(to3(rq), rkt, to3(rv), P3, ret_norm[l], lgf, lgb, S_ALL).reshape(R, BRANCH_W)

        wq_pad, wkv_pad = _mla_weights(mla_w_uq[l], mla_w_ukv[l])
        mq, mk, mv = mla_up(cqn, ckvn, krp, wq_pad, wkv_pad)
        matt = functools.partial(attention, kv_heads=MLA_HEADS, groups=1, dk=MLA_QK_PAD, dv=MLA_V_DIM)
        m_lat = matt(to3(mq), to3(mk), to3(mv), q_rows=SEQ, q_off=0, kv_rows=S_ALL, kv_off=0, tq=512)
        m_ctx = (matt(to3(mq), to3(mk), to3(mv), q_rows=CTX_LEN, q_off=SEQ, kv_rows=CTX_LEN, kv_off=SEQ, tq=256)
                 if need_ctx else no_ctx)
        o_mla = jnp.concatenate([m_lat, m_ctx], 1).reshape(R, BRANCH_W)

        merged = merge_branches(h, (o_s5, o_gqa, o_ret, o_mla), w_gate, l, b_gate[l], w_branch[l].astype(BF16))
        router_pad = jnp.pad(router_w[l], ((0, 0), (0, 128 - N_EXPERTS)))
        x1, h2, logits = out_proj_norm(merged, X, w_out[l].astype(BF16), modblk, ln1_g[l], ln1_b[l], router_pad)

        logits_t = jnp.swapaxes(logits.reshape(B, S_ALL, 128)[:, :, :N_EXPERTS], 1, 2)
        sets = [(0, SEQ)] + ([(SEQ, CTX_LEN)] if need_ctx else [])
        idx_sets, row_parts, gate_parts = [], [], []
        sample_row0 = (jnp.arange(B, dtype=jnp.int32) * S_ALL)[:, None, None]
        for off, n in sets:
            idx_f, gate_f = route(logits_t, off, n)
            idx = jnp.swapaxes(idx_f[:, :, :N_EXPERTS], 1, 2).astype(jnp.int32)
            gate = jnp.swapaxes(gate_f[:, :, :N_EXPERTS], 1, 2)
            idx_sets.append(idx)
            row_parts.append(jnp.swapaxes(idx + sample_row0 + off, 0, 1).reshape(N_EXPERTS, -1))
            gate_parts.append(jnp.swapaxes(gate, 0, 1).reshape(N_EXPERTS, -1))
        rows = jnp.concatenate(row_parts, 1)[:, None, :]
        gates = jnp.concatenate(gate_parts, 1)[:, :, None]
        xs = gather_rows(rows, h2, D)
        ys = expert_ffn(xs, gates, moe_w_gate, moe_w_up, moe_w_down, l)
        moe = combine(ys, idx_sets, [off for off, _ in sets]).reshape(R * D // LANES, LANES)
        if l == DEPTH - 1:
            return final_norm(x1, moe, modblk, ln2_g[l], ln2_b[l]).reshape(B, SEQ, D)
        X, h = post_moe_norm(x1, moe, modblk, ln2_g[l], ln2_b[l], modblks[l + 1])
```

```python
import functools
import math

import numpy as np
import jax
import jax.numpy as jnp
from jax import lax
from jax.experimental import pallas as pl
from jax.experimental.pallas import tpu as pltpu

F32 = jnp.float32
BF16 = jnp.bfloat16

D_MODEL = 2048
BATCH = 2
SEQ = 4096
DEPTH = 2
GRID_W = 64
CTX_LEN = 256
S_ALL = SEQ + CTX_LEN
N_BRANCH = 4
BRANCH_W = D_MODEL // 4
S5_GROUP_CH = 16
S5_GROUPS = BRANCH_W // S5_GROUP_CH
S5_STATE = 64
S5_MODES = S5_GROUPS * S5_STATE
GQA_HEAD_DIM = 128
GQA_HEADS = 4
GQA_KV_HEADS = 2
RET_HEADS = 4
RET_V_DIM = 128
RET_QK_DIM = 64
MLA_HEADS = 4
MLA_Q_LORA = 512
MLA_KV_LORA = 256
MLA_NOPE_DIM = 128
MLA_ROPE_DIM = 64
MLA_V_DIM = 128
MLA_QK_PAD = 256
N_EXPERTS = 16
EXPERT_FF = D_MODEL // 2
EC_CAPACITY_FACTOR = 2
ROPE_BASE = 10000.0
NORM_EPS = 1e-6
LOG2E = math.log2(math.e)
DEEPNORM_ALPHA = (2 * DEPTH) ** 0.25
IN_WIDTHS = (512, 512, 256, 256, 256, 256, 512, 512, 512, 256, 64)
IN_TOTAL = sum(IN_WIDTHS)
IN_PAD = 4096

LANES = 128
SUBLANES = 8
ROW_TILE = 256
MM_ROW_TILE = 512
MERGE_ROW_TILE = 1088
ATTN_SUB = 256
BLOCKS_PER_SAMPLE = S_ALL // ROW_TILE
LAT_BLOCKS = SEQ // ROW_TILE
S5_SLABS = 4
S5_SEG = 8
S5_STEPS = ROW_TILE // S5_SEG
RET_CHUNK = 256
VMEM_LIMIT = 56 * 1024 * 1024


def _cp(sem, vmem=VMEM_LIMIT):
    return pltpu.CompilerParams(dimension_semantics=sem, vmem_limit_bytes=vmem)


def _standardize(x):
    xc = x - jnp.mean(x, -1, keepdims=True)
    return xc * lax.rsqrt(jnp.mean(xc * xc, -1, keepdims=True) + NORM_EPS)


def _sigmoid(x):
    return 1.0 / (1.0 + jnp.exp(-x))


def _store_token_tiles(ref, x, row0=0):
    rows, w = x.shape
    tt = w // LANES
    for j in range(tt):
        ref[pl.ds(row0 * tt + j, rows, stride=tt), :] = x[:, j * LANES:(j + 1) * LANES]


def _load_token_tiles(ref, rows, w, dtype=F32):
    tt = w // LANES
    return jnp.concatenate([ref[pl.ds(j, rows, stride=tt), :].astype(dtype) for j in range(tt)], -1)


def _ada_kernel(c_ref, w_ref, b_ref, o_ref):
    c = c_ref[...]
    cs = (c * _sigmoid(c)).astype(BF16)
    o_ref[0] = jnp.dot(cs, w_ref[0].astype(BF16), preferred_element_type=F32) + b_ref[0]


def ada_modulation(cc, ada_w, ada_b, tn=1024):
    L, D, N = ada_w.shape
    return pl.pallas_call(
        _ada_kernel,
        grid=(L, N // tn),
        in_specs=[pl.BlockSpec((16, D), lambda l, n: (0, 0)),
                  pl.BlockSpec((1, D, tn), lambda l, n: (l, 0, n)),
                  pl.BlockSpec((1, 1, tn), lambda l, n: (l, 0, n))],
        out_specs=pl.BlockSpec((1, 16, tn), lambda l, n: (l, 0, n)),
        out_shape=jax.ShapeDtypeStruct((L, 16, N), F32),
        compiler_params=_cp(("arbitrary", "arbitrary")),
        name="ada_modulation",
    )(cc, ada_w, ada_b.reshape(L, 1, N))


def _modulate_kernel(x_ref, sh_ref, sc_ref, o_ref):
    o_ref[...] = (_standardize(x_ref[...]) * (1.0 + sc_ref[0]) + sh_ref[0]).astype(o_ref.dtype)


def modulate_rows(x, modblk, shift_part, scale_part):
    R, D = x.shape
    return pl.pallas_call(
        _modulate_kernel,
        grid=(R // ROW_TILE,),
        in_specs=[pl.BlockSpec((ROW_TILE, D), lambda i: (i, 0)),
                  pl.BlockSpec((1, 1, D), lambda i: (i, 0, shift_part)),
                  pl.BlockSpec((1, 1, D), lambda i: (i, 0, scale_part))],
        out_specs=pl.BlockSpec((ROW_TILE, D), lambda i: (i, 0)),
        out_shape=jax.ShapeDtypeStruct((R, D), BF16),
        compiler_params=_cp(("parallel",)),
        name="modulate_rows",
    )(x, modblk, modblk)


def _rope_tables(head_dim, width):
    half = head_dim // 2
    n = half // 2
    inv = ROPE_BASE ** (-np.arange(n, dtype=np.float64) / n)
    t = np.arange(SEQ)
    pos = np.stack([t // GRID_W, t % GRID_W], 0).astype(np.float64)
    lane = np.arange(head_dim)
    which = lane // half
    m = lane % half
    ang = (pos[which, :].T.astype(np.float32) * inv[m % n].astype(np.float32)[None, :]).astype(np.float64)
    cos = np.cos(ang)
    sin = np.where(m < n, -np.sin(ang), np.sin(ang))
    cos = np.concatenate([cos, np.ones((CTX_LEN, head_dim))], 0)
    sin = np.concatenate([sin, np.zeros((CTX_LEN, head_dim))], 0)
    reps = width // head_dim
    return (np.tile(cos, (1, reps)).astype(np.float32), np.tile(sin, (1, reps)).astype(np.float32))


def _mla_q_tables():
    cos64, sin64 = _rope_tables(MLA_ROPE_DIM, MLA_ROPE_DIM)
    ones = np.ones((S_ALL, MLA_NOPE_DIM), np.float32)
    zeros = np.zeros((S_ALL, MLA_NOPE_DIM), np.float32)
    pad1 = np.ones((S_ALL, MLA_QK_PAD - MLA_NOPE_DIM - MLA_ROPE_DIM), np.float32)
    cos = np.concatenate([ones, cos64, pad1], 1)
    sin = np.concatenate([zeros, sin64, 0 * pad1], 1)
    return cos, sin


def _rope(x, cos, sin, quarter):
    w = x.shape[-1]
    lane = lax.broadcasted_iota(jnp.int32, x.shape, 1)
    first = (lane % (2 * quarter)) < quarter
    partner = jnp.where(first, pltpu.roll(x, w - quarter, 1), pltpu.roll(x, quarter, 1))
    return x * cos + partner * sin


def _rms_heads(x, gain, head_dim):
    outs = []
    for h in range(x.shape[-1] // head_dim):
        xh = x[:, h * head_dim:(h + 1) * head_dim]
        outs.append(xh * lax.rsqrt(jnp.mean(xh * xh, -1, keepdims=True) + NORM_EPS) * gain)
    return outs[0] if len(outs) == 1 else jnp.concatenate(outs, -1)


PROJ_TILE = 1024


def _tab(ref_a, ref_b):
    return jnp.concatenate([ref_a[...], ref_b[...]], 0)


def _proj_s5_gq_kernel(h_ref, w_ref, ca, cb, sa, sb, gqn_ref, u_ref, gq_ref):
    p = jnp.dot(h_ref[...], w_ref[...], preferred_element_type=F32)
    u_ref[...] = p[:, :512]
    c128, s128 = _tab(ca, cb), _tab(sa, sb)
    q = _rms_heads(p[:, 512:], gqn_ref[...], GQA_HEAD_DIM)
    q = _rope(q, jnp.concatenate([c128] * 4, -1), jnp.concatenate([s128] * 4, -1), GQA_HEAD_DIM // 4)
    gq_ref[...] = (q * (GQA_HEAD_DIM ** -0.5 * LOG2E)).astype(BF16)


def _proj_kv_ret_kernel(h_ref, w_ref, ca, cb, sa, sb, c6a, c6b, s6a, s6b, gkn_ref,
                        gk_ref, gv_ref, rq_ref, rk_ref):
    p = jnp.dot(h_ref[...], w_ref[...], preferred_element_type=F32)
    tile2 = lambda t: jnp.concatenate([t, t], -1)
    c128, s128, c64, s64 = _tab(ca, cb), _tab(sa, sb), _tab(c6a, c6b), _tab(s6a, s6b)
    k = _rms_heads(p[:, :256], gkn_ref[...], GQA_HEAD_DIM)
    gk_ref[...] = _rope(k, tile2(c128), tile2(s128), GQA_HEAD_DIM // 4).astype(BF16)
    gv_ref[...] = p[:, 256:512].astype(BF16)
    rq_ref[...] = _rope(p[:, 512:768], tile2(c64), tile2(s64), RET_QK_DIM // 4).astype(BF16)
    rk = _rope(p[:, 768:], tile2(c64), tile2(s64), RET_QK_DIM // 4)
    rk_ref[...] = (rk * (RET_QK_DIM ** -0.5)).astype(BF16)


def _proj_ret_vg_kernel(h_ref, w_ref, rv_ref, rg_ref):
    p = jnp.dot(h_ref[...], w_ref[...], preferred_element_type=F32)
    rv_ref[...] = p[:, :512].astype(BF16)
    rg_ref[...] = p[:, 512:]


def _proj_mla_kernel(h_ref, w_ref, c6a, c6b, s6a, s6b, cqa, cqb, sqa, sqb, mqn_ref, mkvn_ref, wq_ref, wkv_ref,
                     q_ref, k_ref, v_ref):
    p = jnp.dot(h_ref[...], w_ref[...], preferred_element_type=F32)
    tm = p.shape[0]
    cqn = _rms_heads(p[:, :512], mqn_ref[...], MLA_Q_LORA).astype(BF16)
    ckvn = _rms_heads(p[:, 512:768], mkvn_ref[...], MLA_KV_LORA).astype(BF16)
    lane = lax.broadcasted_iota(jnp.int32, (tm, 128), 1)
    kr = jnp.where(lane < MLA_ROPE_DIM, _rope(p[:, 768:896], _tab(c6a, c6b), _tab(s6a, s6b), MLA_ROPE_DIM // 4), 0.0)
    scale = (MLA_NOPE_DIM + MLA_ROPE_DIM) ** -0.5 * LOG2E
    q = jnp.dot(cqn, wq_ref[...], preferred_element_type=F32)
    kv = jnp.dot(ckvn, wkv_ref[...], preferred_element_type=F32)
    cos, sin = _tab(cqa, cqb), _tab(sqa, sqb)
    krp = jnp.concatenate([jnp.zeros((tm, MLA_NOPE_DIM), F32), kr], -1)
    for h in range(MLA_HEADS):
        sl = slice(h * MLA_QK_PAD, (h + 1) * MLA_QK_PAD)
        q_ref[:, sl] = (_rope(q[:, sl], cos, sin, MLA_ROPE_DIM // 4) * scale).astype(BF16)
        k_ref[:, sl] = (kv[:, sl] + krp).astype(BF16)
    v_ref[...] = kv[:, MLA_HEADS * MLA_QK_PAD:].astype(BF16)


def in_proj_branches(h, w_in_p, gqa_q_norm, gqa_k_norm, mla_q_norm, mla_kv_norm, wq_pad, wkv_pad,
                     tm=MM_ROW_TILE):
    R, D = h.shape
    half = tm // 2
    assert half == ROW_TILE
    c128, s128 = (jnp.asarray(t) for t in _rope_tables(GQA_HEAD_DIM, 128))
    c64, s64 = (jnp.asarray(t) for t in _rope_tables(RET_QK_DIM, 128))
    cq, sq = (jnp.asarray(t) for t in _mla_q_tables())
    x_spec = pl.BlockSpec((tm, D), lambda i: (i, 0))
    w_spec = lambda n: pl.BlockSpec((D, PROJ_TILE), lambda i: (0, n))
    row = lambda w: pl.BlockSpec((tm, w), lambda i: (i, 0))
    vec = lambda w: pl.BlockSpec((1, w), lambda i: (0, 0))
    full = lambda a: pl.BlockSpec(a.shape, lambda i: (0, 0))

    def tabs(t):
        w = t.shape[1]
        return ([pl.BlockSpec((half, w), lambda i: ((2 * i) % BLOCKS_PER_SAMPLE, 0)),
                 pl.BlockSpec((half, w), lambda i: ((2 * i + 1) % BLOCKS_PER_SAMPLE, 0))], [t, t])

    def call(kernel, n, extra_specs, extra_args, outs, name):
        return pl.pallas_call(
            kernel, grid=(R // tm,),
            in_specs=[x_spec, w_spec(n)] + extra_specs,
            out_specs=[row(w) for w, _ in outs],
            out_shape=[jax.ShapeDtypeStruct((R, w), dt) for w, dt in outs],
            compiler_params=_cp(("parallel",)), name=name,
        )(h, w_in_p, *extra_args)

    def gather_tabs(*ts):
        specs, args = [], []
        for t in ts:
            s, a = tabs(t)
            specs += s
            args += a
        return specs, args

    s0, a0 = gather_tabs(c128, s128)
    u, gq = call(_proj_s5_gq_kernel, 0, s0 + [vec(128)], a0 + [gqa_q_norm.reshape(1, -1)],
                 [(512, F32), (512, BF16)], "proj_s5_gq")
    s1, a1 = gather_tabs(c128, s128, c64, s64)
    gk, gv, rq, rk = call(_proj_kv_ret_kernel, 1, s1 + [vec(128)], a1 + [gqa_k_norm.reshape(1, -1)],
                          [(256, BF16)] * 4, "proj_kv_ret")
    rv, rg = call(_proj_ret_vg_kernel, 2, [], [], [(512, BF16), (512, F32)], "proj_ret_vg")
    s3, a3 = gather_tabs(c64, s64, cq, sq)
    hq = MLA_HEADS * MLA_QK_PAD
    mq, mk, mv = call(_proj_mla_kernel, 3, s3 + [vec(512), vec(256), full(wq_pad), full(wkv_pad)],
                      a3 + [mla_q_norm.reshape(1, -1), mla_kv_norm.reshape(1, -1), wq_pad, wkv_pad],
                      [(hq, BF16), (hq, BF16), (MLA_HEADS * MLA_V_DIM, BF16)], "proj_mla")
    return u, gq, gk, gv, rq, rk, rv, rg, mq, mk, mv


def _attn_kernel(q_ref, k_ref, v_ref, o_ref, *, groups, dk, dv):
    k = k_ref[0]
    v = v_ref[0]
    v1 = jnp.concatenate([v, jnp.ones_like(v)], -1)
    tq = q_ref.shape[1]
    q = jnp.concatenate([q_ref[0, :, g * dk:(g + 1) * dk] for g in range(groups)], 0)
    s = lax.dot_general(q, k, (((1,), (1,)), ((), ())), preferred_element_type=F32)
    for g in range(groups):
        for half in range(tq // ATTN_SUB):
            r0 = g * tq + half * ATTN_SUB
            sh = s[r0:r0 + ATTN_SUB]
            p = jnp.exp2(sh - jnp.max(sh, -1, keepdims=True)).astype(BF16)
            o = jnp.dot(p, v1, preferred_element_type=F32)
            rows = slice(half * ATTN_SUB, (half + 1) * ATTN_SUB)
            o_ref[0, rows, g * dv:(g + 1) * dv] = (o[:, :dv] / o[:, dv:]).astype(o_ref.dtype)


def attention(q, k, v, *, kv_heads, groups, dk, dv, q_rows, q_off, kv_rows, kv_off, tq):
    B = q.shape[0]
    assert tq % ATTN_SUB == 0 and q_off % tq == 0 and q_rows % tq == 0
    qb0, kb0 = q_off // tq, kv_off // kv_rows
    return pl.pallas_call(
        functools.partial(_attn_kernel, groups=groups, dk=dk, dv=dv),
        grid=(B, kv_heads, q_rows // tq),
        in_specs=[pl.BlockSpec((1, tq, groups * dk), lambda b, h, i: (b, qb0 + i, h)),
                  pl.BlockSpec((1, kv_rows, dk), lambda b, h, i: (b, kb0, h)),
                  pl.BlockSpec((1, kv_rows, dv), lambda b, h, i: (b, kb0, h))],
        out_specs=pl.BlockSpec((1, tq, groups * dv), lambda b, h, i: (b, i, h)),
        out_shape=jax.ShapeDtypeStruct((B, q_rows, kv_heads * groups * dv), BF16),
        compiler_params=_cp(("parallel", "parallel", "arbitrary")),
        name="attention",
    )(q, k, v)


def _ret_kernel(lgf_ref, lgb_ref, q_ref, kt_ref, v_ref, g_ref, gain_ref, o_ref, sb_ref, *, n_out):
    L = RET_CHUNK
    nc = SEQ // L
    pair = pl.program_id(1)
    r_i = lax.broadcasted_iota(jnp.int32, (L, L), 0)
    c_i = lax.broadcasted_iota(jnp.int32, (L, L), 1)
    diff = (r_i - c_i).astype(F32)
    pos_col = lax.broadcasted_iota(jnp.int32, (L, 1), 0).astype(F32)
    pos_row = lax.broadcasted_iota(jnp.int32, (1, L), 1).astype(F32)
    for j in range(2):
        lgf = lgf_ref[pair * 2 + j]
        lgb = lgb_ref[pair * 2 + j]
        dmat = jnp.where(diff >= 0, jnp.exp(lgf * jnp.maximum(diff, 0.0)), jnp.exp(lgb * jnp.maximum(-diff, 0.0)))
        dq_f = jnp.exp(lgf * (pos_col + 1.0))
        dq_b = jnp.exp(lgb * (L - pos_col))
        dk_f = jnp.exp(lgf * (L - 1.0 - pos_row))
        dk_b = jnp.exp(lgb * pos_row)
        dc_f = jnp.exp(lgf * L)
        dc_b = jnp.exp(lgb * L)
        qs = slice(j * RET_QK_DIM, (j + 1) * RET_QK_DIM)
        vs = slice(j * RET_V_DIM, (j + 1) * RET_V_DIM)

        def chunk(c):
            rows = slice(c * L, (c + 1) * L)
            return q_ref[0, rows, qs], kt_ref[0, qs, rows], v_ref[0, rows, vs]

        def readout(o, c, out_row0):
            oc = o - jnp.mean(o, -1, keepdims=True)
            on = oc * lax.rsqrt(jnp.mean(oc * oc, -1, keepdims=True) + NORM_EPS)
            g = g_ref[0, c * L:(c + 1) * L, vs]
            o_ref[0, out_row0:out_row0 + L, vs] = (on * gain_ref[:, vs] * (g * _sigmoid(g))).astype(o_ref.dtype)

        def intra(q, kt, v):
            sc = jnp.dot(q, kt, preferred_element_type=F32) * dmat
            return jnp.dot(sc.astype(BF16), v, preferred_element_type=F32)

        def state_add(kt, dk, v):
            return jnp.dot((kt.astype(F32) * dk).astype(BF16), v, preferred_element_type=F32)

        qc, ktc, vc = chunk(nc)
        if n_out > SEQ:
            readout(intra(qc, ktc, vc), nc, SEQ)
        s_f = state_add(ktc, dk_f, vc)
        s_b = state_add(ktc, dk_b, vc)
        for c in range(nc - 1, -1, -1):
            sb_ref[c] = s_b
            _, kt, v = chunk(c)
            s_b = s_b * dc_b + state_add(kt, dk_b, v)
        for c in range(nc):
            q, kt, v = chunk(c)
            qf = q.astype(F32)
            o = (intra(q, kt, v)
                 + jnp.dot((qf * dq_f).astype(BF16), s_f.astype(BF16), preferred_element_type=F32)
                 + jnp.dot((qf * dq_b).astype(BF16), sb_ref[c].astype(BF16), preferred_element_type=F32))
            readout(o, c, c * L)
            s_f = s_f * dc_f + state_add(kt, dk_f, v)


def retention(rq, rkt, rv, rg, gain, lgf, lgb, n_out):
    B = rq.shape[0]
    smem = pl.BlockSpec(memory_space=pltpu.SMEM)
    return pl.pallas_call(
        functools.partial(_ret_kernel, n_out=n_out),
        grid=(B, RET_HEADS // 2),
        in_specs=[smem, smem,
                  pl.BlockSpec((1, S_ALL, 128), lambda b, p: (b, 0, p)),
                  pl.BlockSpec((1, 128, S_ALL), lambda b, p: (b, p, 0)),
                  pl.BlockSpec((1, S_ALL, 256), lambda b, p: (b, 0, p)),
                  pl.BlockSpec((1, S_ALL, 256), lambda b, p: (b, 0, p)),
                  pl.BlockSpec((1, 256), lambda b, p: (0, p))],
        out_specs=pl.BlockSpec((1, n_out, 256), lambda b, p: (b, 0, p)),
        out_shape=jax.ShapeDtypeStruct((B, n_out, RET_HEADS * RET_V_DIM), BF16),
        scratch_shapes=[pltpu.VMEM((SEQ // RET_CHUNK, RET_QK_DIM, RET_V_DIM), F32)],
        compiler_params=_cp(("parallel", "parallel")),
        name="retention",
    )(lgf, lgb, rq, rkt, rv, rg, gain.reshape(1, -1))


def _s5_kernel(u_ref, bb_ref, lam_ref, cm_ref, y_ref, bu_ref, pw_ref, st_ref):
    M = S5_MODES
    chunk = pl.program_id(2)
    lam_re = lam_ref[0, 0]
    lam_im = lam_ref[0, 1]

    @pl.when(chunk == 0)
    def _():
        st_ref[...] = jnp.zeros_like(st_ref)
        p_re, p_im = lam_re, lam_im
        for j in range(S5_STEPS):
            pw_ref[0, j] = p_re
            pw_ref[1, j] = p_im
            p_re, p_im = p_re * lam_re - p_im * lam_im, p_re * lam_im + p_im * lam_re

    r_i = lax.broadcasted_iota(jnp.int32, (ROW_TILE, ROW_TILE), 0)
    c_i = lax.broadcasted_iota(jnp.int32, (ROW_TILE, ROW_TILE), 1)
    flip = pl.program_id(0) == 1

    def scan_time(r):
        t = (r % S5_SEG) * S5_STEPS + r // S5_SEG
        return jnp.where(flip, ROW_TILE - 1 - t, t)

    to_scan = (c_i == scan_time(r_i)).astype(BF16)
    to_time = (r_i == scan_time(c_i)).astype(BF16)
    u = jnp.dot(to_scan, u_ref[...].astype(BF16), preferred_element_type=F32).astype(BF16)

    slab = BRANCH_W // S5_SLABS
    ms = M // S5_SLABS
    for s in range(S5_SLABS):
        part = jnp.dot(u[:, s * slab:(s + 1) * slab], bb_ref[0, s], preferred_element_type=F32)
        bu_ref[:, s * ms:(s + 1) * ms] = part[:, :ms]
        bu_ref[:, M + s * ms:M + (s + 1) * ms] = part[:, ms:]

    tile = 512
    for t in range(M // tile):
        re_sl = slice(t * tile, (t + 1) * tile)
        im_sl = slice(M + t * tile, M + (t + 1) * tile)
        lr, li = lam_re[:, re_sl], lam_im[:, re_sl]

        def step(j, carry):
            s_re, s_im = carry
            rows = pl.ds(pl.multiple_of(j * S5_SEG, S5_SEG), S5_SEG)
            n_re = lr * s_re - li * s_im + bu_ref[rows, re_sl]
            n_im = lr * s_im + li * s_re + bu_ref[rows, im_sl]
            bu_ref[rows, re_sl] = n_re
            bu_ref[rows, im_sl] = n_im
            return n_re, n_im

        z = jnp.zeros((S5_SEG, tile), F32)
        lax.fori_loop(0, S5_STEPS, step, (z, z))

    last = slice((S5_STEPS - 1) * S5_SEG, S5_STEPS * S5_SEG)
    e_re, e_im = bu_ref[last, 0:M], bu_ref[last, M:2 * M]
    pl_re, pl_im = pw_ref[0, S5_STEPS - 1][0:1], pw_ref[1, S5_STEPS - 1][0:1]
    c_re, c_im = st_ref[0:1, :], st_ref[1:2, :]
    rows_re, rows_im = [], []
    for k in range(S5_SEG):
        rows_re.append(c_re)
        rows_im.append(c_im)
        c_re, c_im = (e_re[k:k + 1] + pl_re * c_re - pl_im * c_im,
                      e_im[k:k + 1] + pl_re * c_im + pl_im * c_re)
    st_ref[0:1, :] = c_re
    st_ref[1:2, :] = c_im
    car_re = jnp.concatenate(rows_re, 0)
    car_im = jnp.concatenate(rows_im, 0)

    def fix(j, _):
        rows = pl.ds(pl.multiple_of(j * S5_SEG, S5_SEG), S5_SEG)
        p_re, p_im = pw_ref[0, j], pw_ref[1, j]
        bu_ref[rows, 0:M] = bu_ref[rows, 0:M] + p_re * car_re - p_im * car_im
        bu_ref[rows, M:2 * M] = bu_ref[rows, M:2 * M] + p_re * car_im + p_im * car_re
        return 0

    lax.fori_loop(0, S5_STEPS, fix, 0)

    ys = []
    for s in range(S5_SLABS):
        hs = jnp.concatenate([bu_ref[:, s * ms:(s + 1) * ms], bu_ref[:, M + s * ms:M + (s + 1) * ms]], -1)
        ys.append(jnp.dot(hs.astype(BF16), cm_ref[s], preferred_element_type=F32))
    y = jnp.concatenate(ys, -1)
    out = None
    for _ in range(2):
        piece = y.astype(BF16)
        y = y - piece.astype(F32)
        term = jnp.dot(to_time, piece, preferred_element_type=F32)
        out = term if out is None else out + term
    y_ref[0] = out


def _s5_block(d, c):
    fwd = (c + LAT_BLOCKS) % BLOCKS_PER_SAMPLE
    bwd = jnp.where(c == 0, LAT_BLOCKS, LAT_BLOCKS - c)
    return jnp.where(d == 0, fwd, bwd)


def s5_scan(u, bb, lam, cmat):
    R = u.shape[0]
    B = R // S_ALL
    W = BRANCH_W
    row_block = lambda d, b, c: b * BLOCKS_PER_SAMPLE + _s5_block(d, c)
    return pl.pallas_call(
        _s5_kernel,
        grid=(2, B, BLOCKS_PER_SAMPLE),
        in_specs=[pl.BlockSpec((ROW_TILE, W), lambda d, b, c: (row_block(d, b, c), 0)),
                  pl.BlockSpec((1,) + bb.shape[1:], lambda d, b, c: (d, 0, 0, 0)),
                  pl.BlockSpec((1, 2, S5_SEG, S5_MODES), lambda d, b, c: (d, 0, 0, 0)),
                  pl.BlockSpec(cmat.shape, lambda d, b, c: (0, 0, 0))],
        out_specs=pl.BlockSpec((1, ROW_TILE, W), lambda d, b, c: (d, row_block(d, b, c), 0)),
        out_shape=jax.ShapeDtypeStruct((2, R, W), F32),
        scratch_shapes=[pltpu.VMEM((ROW_TILE, 2 * S5_MODES), F32),
                        pltpu.VMEM((2, S5_STEPS, S5_SEG, S5_MODES), F32),
                        pltpu.VMEM((8, S5_MODES), F32)],
        compiler_params=_cp(("arbitrary", "arbitrary", "arbitrary")),
        name="s5_scan",
    )(u, bb, lam, cmat)


def _s5_out_kernel(yf_ref, yb_ref, p_ref, d_ref, w_ref, o_ref):
    y = yf_ref[0] + yb_ref[0] + d_ref[...] * p_ref[...]
    z = 0.5 * y * (1.0 + jnp.tanh(math.sqrt(2.0 / math.pi) * (y + 0.044715 * (y * y * y))))
    gate = _sigmoid(jnp.dot(z.astype(BF16), w_ref[...], preferred_element_type=F32))
    o_ref[...] = (z * gate).astype(o_ref.dtype)


def s5_output(y_dirs, u, d, w_glu):
    R = y_dirs.shape[1]
    return pl.pallas_call(
        _s5_out_kernel,
        grid=(R // ROW_TILE,),
        in_specs=[pl.BlockSpec((1, ROW_TILE, 512), lambda i: (0, i, 0)),
                  pl.BlockSpec((1, ROW_TILE, 512), lambda i: (1, i, 0)),
                  pl.BlockSpec((ROW_TILE, 512), lambda i: (i, 0)),
                  pl.BlockSpec((1, 512), lambda i: (0, 0)),
                  pl.BlockSpec((512, 512), lambda i: (0, 0))],
        out_specs=pl.BlockSpec((ROW_TILE, 512), lambda i: (i, 0)),
        out_shape=jax.ShapeDtypeStruct((R, 512), BF16),
        compiler_params=_cp(("parallel",)),
        name="s5_output",
    )(y_dirs, y_dirs, u, d.reshape(1, -1), w_glu)


def _merge_kernel(h_ref, o0_ref, o1_ref, o2_ref, o3_ref, g0_ref, g1_ref, g2_ref, g3_ref,
                  bg_ref, wb_ref, out_ref, wg_ref):
    @pl.when(pl.program_id(1) == 0)
    def _():
        for k, g in enumerate((g0_ref, g1_ref, g2_ref, g3_ref)):
            wg_ref[k] = g[0].astype(BF16)

    h = h_ref[...]
    acc = None
    for k, o in enumerate((o0_ref, o1_ref, o2_ref, o3_ref)):
        gate = _sigmoid(jnp.dot(h, wg_ref[k], preferred_element_type=F32) + bg_ref[k])
        term = gate * jnp.dot(o[...], wb_ref[k], preferred_element_type=F32)
        acc = term if acc is None else acc + term
    out_ref[...] = acc.astype(out_ref.dtype)


def merge_branches(h, outs, w_gate_all, layer, b_gate, w_branch, tn=256, tm=MERGE_ROW_TILE):
    R, D = h.shape
    nb = D // tn
    gate_spec = lambda k: pl.BlockSpec((1, D, tn), lambda n, m: (layer, 0, k * nb + n))
    bg = b_gate.reshape(N_BRANCH, 1, D)
    return pl.pallas_call(
        _merge_kernel,
        grid=(nb, R // tm),
        in_specs=[pl.BlockSpec((tm, D), lambda n, m: (m, 0))]
                 + [pl.BlockSpec((tm, BRANCH_W), lambda n, m: (m, 0))] * N_BRANCH
                 + [gate_spec(k) for k in range(N_BRANCH)]
                 + [pl.BlockSpec((N_BRANCH, 1, tn), lambda n, m: (0, 0, n)),
                    pl.BlockSpec((N_BRANCH, BRANCH_W, tn), lambda n, m: (0, 0, n))],
        out_specs=pl.BlockSpec((tm, tn), lambda n, m: (m, n)),
        out_shape=jax.ShapeDtypeStruct((R, D), BF16),
        scratch_shapes=[pltpu.VMEM((N_BRANCH, D, tn), BF16)],
        compiler_params=_cp(("arbitrary", "arbitrary")),
        name="merge_branches",
    )(h, *outs, w_gate_all, w_gate_all, w_gate_all, w_gate_all, bg, w_branch)


def _out_kernel(m_ref, x_ref, w_ref, g1_ref, lg_ref, lb_ref, sh_ref, sc_ref, rw_ref,
                x1_ref, h2_ref, lo_ref):
    y = jnp.dot(m_ref[...], w_ref[...], preferred_element_type=F32)
    x1 = _standardize(DEEPNORM_ALPHA * x_ref[...] + g1_ref[0] * y) * lg_ref[...] + lb_ref[...]
    x1_ref[...] = x1
    h2 = _standardize(x1) * (1.0 + sc_ref[0]) + sh_ref[0]
    _store_token_tiles(h2_ref, h2)
    rw = rw_ref[...]
    h_hi, rw_hi = h2.astype(BF16), rw.astype(BF16)
    h_lo, rw_lo = (h2 - h_hi.astype(F32)).astype(BF16), (rw - rw_hi.astype(F32)).astype(BF16)
    lo_ref[...] = (jnp.dot(h_hi, rw_hi, preferred_element_type=F32)
                   + jnp.dot(h_lo, rw_hi, preferred_element_type=F32)
                   + jnp.dot(h_hi, rw_lo, preferred_element_type=F32))


def out_proj_norm(merged, x, w_out, modblk, ln_g, ln_b, router_pad):
    R, D = x.shape
    tt = D // LANES
    row = lambda w: pl.BlockSpec((ROW_TILE, w), lambda i: (i, 0))
    mod = lambda part: pl.BlockSpec((1, 1, D), lambda i: (i, 0, part))
    vec = pl.BlockSpec((1, D), lambda i: (0, 0))
    return pl.pallas_call(
        _out_kernel,
        grid=(R // ROW_TILE,),
        in_specs=[row(D), row(D), pl.BlockSpec((D, D), lambda i: (0, 0)), mod(2), vec, vec, mod(3), mod(4),
                  pl.BlockSpec((D, 128), lambda i: (0, 0))],
        out_specs=[row(D), pl.BlockSpec((ROW_TILE * tt, LANES), lambda i: (i, 0)), row(128)],
        out_shape=[jax.ShapeDtypeStruct((R, D), F32), jax.ShapeDtypeStruct((R * tt, LANES), F32),
                   jax.ShapeDtypeStruct((R, 128), F32)],
        compiler_params=_cp(("parallel",)),
        name="out_proj_norm",
    )(merged, x, w_out, modblk, ln_g.reshape(1, -1), ln_b.reshape(1, -1), modblk, modblk, router_pad)


ROUTE_GEOMETRIC_STEPS = 40
ROUTE_BISECT_STEPS = ROUTE_GEOMETRIC_STEPS + 8


def _route_kernel(lt_ref, idx_ref, gate_ref, slot_ref, aff_ref, *, n, cap):
    E = N_EXPERTS
    lt = lt_ref[0]
    ex = jnp.exp(lt - jnp.max(lt, 0, keepdims=True))
    aff = ex / jnp.sum(ex, 0, keepdims=True)

    def count_ge(v):
        return jnp.sum((aff >= v).astype(F32), 1, keepdims=True)

    tiny = jnp.full((E, 1), float(np.finfo(np.float32).tiny), F32)
    normal = count_ge(tiny) >= cap
    lo0 = jnp.where(normal, tiny, 0.0)
    hi0 = jnp.where(normal, 2.0, tiny)

    def bisect(i, lo_hi):
        lo, hi = lo_hi
        geo = jnp.clip(jnp.sqrt(lo) * jnp.sqrt(hi), lo, hi)
        mid = jnp.where(jnp.logical_and(i < ROUTE_GEOMETRIC_STEPS, lo > 0.0), geo, lo + 0.5 * (hi - lo))
        ok = count_ge(mid) >= cap
        return jnp.where(ok, mid, lo), jnp.where(ok, hi, mid)

    lo, _ = lax.fori_loop(0, ROUTE_BISECT_STEPS, bisect, (lo0, hi0))
    thr = jnp.min(jnp.where(aff >= lo, aff, 2.0), 1, keepdims=True)
    gt = aff > thr
    eq = aff == thr
    need = cap - jnp.sum(gt.astype(F32), 1, keepdims=True)

    blk = min(n, 512)
    upper = (lax.broadcasted_iota(jnp.int32, (blk, blk), 0)
             < lax.broadcasted_iota(jnp.int32, (blk, blk), 1)).astype(BF16)

    def excl_cumsum(mask):
        parts, carry = [], jnp.zeros((E, 1), F32)
        for j in range(n // blk):
            m = mask[:, j * blk:(j + 1) * blk].astype(F32)
            parts.append(jnp.dot(m.astype(BF16), upper, preferred_element_type=F32) + carry)
            carry = carry + jnp.sum(m, 1, keepdims=True)
        return parts[0] if len(parts) == 1 else jnp.concatenate(parts, 1)

    sel = gt | (eq & (excl_cumsum(eq) < need))
    slot_ref[...] = jnp.where(sel, excl_cumsum(sel), -1.0)
    idx_ref[0] = jnp.zeros((cap, 128), F32)
    gate_ref[0] = jnp.zeros((cap, 128), F32)

    aff_ref[...] = aff
    rows = min(cap, 64)
    lane_e = lax.broadcasted_iota(jnp.int32, (rows, 128), 1)
    tok = lax.broadcasted_iota(jnp.int32, (1, 128), 1).astype(F32)

    for e in range(E):
        def per_rows(c, _, e=e):
            r0 = pl.multiple_of(c * rows, rows)
            s_col = (lax.broadcasted_iota(jnp.int32, (rows, 1), 0) + r0).astype(F32)
            acc_i = jnp.zeros((rows, 128), F32)
            acc_g = jnp.zeros((rows, 128), F32)
            for j in range(n // 128):
                hit = slot_ref[e:e + 1, j * 128:(j + 1) * 128] == s_col
                acc_i = acc_i + jnp.where(hit, tok + float(j * 128), 0.0)
                acc_g = acc_g + jnp.where(hit, aff_ref[e:e + 1, j * 128:(j + 1) * 128], 0.0)
            icol = jnp.sum(acc_i, 1, keepdims=True)
            gcol = jnp.sum(acc_g, 1, keepdims=True)
            idx_ref[0, pl.ds(r0, rows), :] = jnp.where(lane_e == e, icol, idx_ref[0, pl.ds(r0, rows), :])
            gate_ref[0, pl.ds(r0, rows), :] = jnp.where(lane_e == e, gcol, gate_ref[0, pl.ds(r0, rows), :])
            return 0

        lax.fori_loop(0, cap // rows, per_rows, 0)


def route(logits_t, off, n):
    B = logits_t.shape[0]
    cap = EC_CAPACITY_FACTOR * n // N_EXPERTS
    out = pl.BlockSpec((1, cap, 128), lambda b: (b, 0, 0))
    return pl.pallas_call(
        functools.partial(_route_kernel, n=n, cap=cap),
        grid=(B,),
        in_specs=[pl.BlockSpec((1, N_EXPERTS, n), lambda b: (b, 0, off // n))],
        out_specs=[out, out],
        out_shape=[jax.ShapeDtypeStruct((B, cap, 128), F32)] * 2,
        scratch_shapes=[pltpu.VMEM((N_EXPERTS, n), F32), pltpu.VMEM((N_EXPERTS, n), F32)],
        compiler_params=_cp(("parallel",)),
        name="route",
    )(logits_t)


GATHER_UNROLL = 8


def _gather_kernel(rows_ref, h_hbm, o_ref, land_ref, sem, *, T, tt):
    def tile_copy(s, r):
        src = h_hbm.at[pl.ds(pl.multiple_of(r * tt, tt), tt), :]
        dst = land_ref.at[pl.ds(pl.multiple_of(s * tt, tt), tt), :]
        return pltpu.make_async_copy(src, dst, sem)

    def issue(g, _):
        for i in range(GATHER_UNROLL):
            s = g * GATHER_UNROLL + i
            tile_copy(s, rows_ref[0, 0, s]).start()
        return 0

    lax.fori_loop(0, T // GATHER_UNROLL, issue, 0)

    def drain(g, _):
        for i in range(GATHER_UNROLL):
            tile_copy(g * GATHER_UNROLL + i, 0).wait()
        return 0

    lax.fori_loop(0, T // GATHER_UNROLL, drain, 0)
    o_ref[0] = _load_token_tiles(land_ref, T, tt * LANES, BF16)


def gather_rows(rows, h_tiles, D):
    E, _, T = rows.shape
    tt = D // LANES
    return pl.pallas_call(
        functools.partial(_gather_kernel, T=T, tt=tt),
        grid=(E,),
        in_specs=[pl.BlockSpec((1, 1, T), lambda e: (e, 0, 0), memory_space=pltpu.SMEM),
                  pl.BlockSpec(memory_space=pl.ANY)],
        out_specs=pl.BlockSpec((1, T, D), lambda e: (e, 0, 0)),
        out_shape=jax.ShapeDtypeStruct((E, T, D), BF16),
        scratch_shapes=[pltpu.VMEM((T * tt, LANES), F32), pltpu.SemaphoreType.DMA],
        compiler_params=_cp(("arbitrary",)),
        name="gather_rows",
    )(rows, h_tiles)


def _expert_kernel(x_ref, gate_ref, wg_ref, wu_ref, wd_ref, o_ref):
    last = pl.num_programs(1) - 1
    x = x_ref[0]
    a = jnp.dot(x, wg_ref[0, 0].astype(BF16), preferred_element_type=F32)
    u = jnp.dot(x, wu_ref[0, 0].astype(BF16), preferred_element_type=F32)
    hid = (a * _sigmoid(a) * u).astype(BF16)
    part = jnp.dot(hid, wd_ref[0, 0].astype(BF16), preferred_element_type=F32)

    @pl.when(pl.program_id(1) == 0)
    def _():
        o_ref[0] = part

    @pl.when(jnp.logical_and(pl.program_id(1) != 0, pl.program_id(1) != last))
    def _():
        o_ref[0] += part

    @pl.when(pl.program_id(1) == last)
    def _():
        o_ref[0] = (o_ref[0] + part) * gate_ref[0]


def expert_ffn(xs, gate, w_gate, w_up, w_down, layer, tf=256):
    E, T, D = xs.shape
    FF = w_gate.shape[-1]
    return pl.pallas_call(
        _expert_kernel,
        grid=(E, FF // tf),
        in_specs=[pl.BlockSpec((1, T, D), lambda e, f: (e, 0, 0)),
                  pl.BlockSpec((1, T, 1), lambda e, f: (e, 0, 0)),
                  pl.BlockSpec((1, 1, D, tf), lambda e, f: (layer, e, 0, f)),
                  pl.BlockSpec((1, 1, D, tf), lambda e, f: (layer, e, 0, f)),
                  pl.BlockSpec((1, 1, tf, D), lambda e, f: (layer, e, f, 0))],
        out_specs=pl.BlockSpec((1, T, D), lambda e, f: (e, 0, 0)),
        out_shape=jax.ShapeDtypeStruct((E, T, D), F32),
        compiler_params=_cp(("parallel", "arbitrary")),
        name="expert_ffn",
    )(xs, gate, w_gate, w_up, w_down)


COMBINE_UNROLL = 8


def _combine_kernel(*refs, caps, bases, dh):
    n_sets = len(caps)
    idx_refs, y_refs = refs[:n_sets], refs[n_sets:2 * n_sets]
    o_ref, yt_ref = refs[2 * n_sets], refs[2 * n_sets + 1]
    tt = dh // LANES

    @pl.when(pl.program_id(2) == 0)
    def _():
        o_ref[...] = jnp.zeros_like(o_ref)

    for idx_ref, y_ref, cap, base in zip(idx_refs, y_refs, caps, bases):
        for j in range(tt):
            yt_ref[pl.ds(j, cap, stride=tt), :] = y_ref[0, :, j * LANES:(j + 1) * LANES]

        def body(g, _, idx_ref=idx_ref, base=base):
            s0 = g * COMBINE_UNROLL
            toks = [idx_ref[0, 0, 0, s0 + i] + base for i in range(COMBINE_UNROLL)]
            sums = [o_ref[0, toks[i], 0] + yt_ref[pl.ds(pl.multiple_of((s0 + i) * tt, tt), tt), :]
                    for i in range(COMBINE_UNROLL)]
            for i in range(COMBINE_UNROLL):
                o_ref[0, toks[i], 0] = sums[i]
            return 0

        lax.fori_loop(0, cap // COMBINE_UNROLL, body, 0)


def combine(ys, idx_sets, bases, dh=SUBLANES * LANES):
    E, T, D = ys.shape
    B = idx_sets[0].shape[0]
    caps = tuple(int(i.shape[-1]) for i in idx_sets)
    starts = np.concatenate([[0], np.cumsum([B * c for c in caps])[:-1]])
    idx_specs = [pl.BlockSpec((1, 1, 1, c), lambda b, hf, e: (b, e, 0, 0), memory_space=pltpu.SMEM) for c in caps]
    y_specs = [pl.BlockSpec((1, c, dh), lambda b, hf, e, blk0=int(st) // c: (e, blk0 + b, hf))
               for c, st in zip(caps, starts)]
    return pl.pallas_call(
        functools.partial(_combine_kernel, caps=caps, bases=tuple(bases), dh=dh),
        grid=(B, D // dh, E),
        in_specs=idx_specs + y_specs,
        out_specs=pl.BlockSpec((1, S_ALL, 1, SUBLANES, LANES), lambda b, hf, e: (b, 0, hf, 0, 0)),
        out_shape=jax.ShapeDtypeStruct((B, S_ALL, D // dh, SUBLANES, LANES), F32),
        scratch_shapes=[pltpu.VMEM((max(caps) * dh // LANES, LANES), F32)],
        compiler_params=_cp(("parallel", "parallel", "arbitrary")),
        name="combine",
    )(*[i[:, :, None, :] for i in idx_sets], *([ys] * len(caps)))


def _post_kernel(x_ref, y_ref, g2_ref, lg_ref, lb_ref, sh_ref, sc_ref, x2_ref, h_ref):
    y = _load_token_tiles(y_ref, ROW_TILE, x_ref.shape[1])
    x2 = _standardize(DEEPNORM_ALPHA * x_ref[...] + g2_ref[0] * y) * lg_ref[...] + lb_ref[...]
    x2_ref[...] = x2
    h_ref[...] = (_standardize(x2) * (1.0 + sc_ref[0]) + sh_ref[0]).astype(BF16)


def post_moe_norm(x1, y_tiles, modblk, ln_g, ln_b, modblk_next):
    R, D = x1.shape
    row = pl.BlockSpec((ROW_TILE, D), lambda i: (i, 0))
    mod = lambda part: pl.BlockSpec((1, 1, D), lambda i: (i, 0, part))
    vec = pl.BlockSpec((1, D), lambda i: (0, 0))
    y = pl.BlockSpec((ROW_TILE * D // LANES, LANES), lambda i: (i, 0))
    return pl.pallas_call(
        _post_kernel,
        grid=(R // ROW_TILE,),
        in_specs=[row, y, mod(5), vec, vec, mod(0), mod(1)],
        out_specs=[row, row],
        out_shape=[jax.ShapeDtypeStruct((R, D), F32), jax.ShapeDtypeStruct((R, D), BF16)],
        compiler_params=_cp(("parallel",)),
        name="post_moe_norm",
    )(x1, y_tiles, modblk, ln_g.reshape(1, -1), ln_b.reshape(1, -1), modblk_next, modblk_next)


def _final_kernel(x_ref, y_ref, g2_ref, lg_ref, lb_ref, x2_ref):
    y = _load_token_tiles(y_ref, ROW_TILE, x_ref.shape[1])
    x2_ref[...] = _standardize(DEEPNORM_ALPHA * x_ref[...] + g2_ref[0] * y) * lg_ref[...] + lb_ref[...]


def final_norm(x1, y_tiles, modblk, ln_g, ln_b):
    R, D = x1.shape
    n_lat = R // S_ALL * LAT_BLOCKS
    blk = lambda i: i + i // LAT_BLOCKS
    vec = pl.BlockSpec((1, D), lambda i: (0, 0))
    return pl.pallas_call(
        _final_kernel,
        grid=(n_lat,),
        in_specs=[pl.BlockSpec((ROW_TILE, D), lambda i: (blk(i), 0)),
                  pl.BlockSpec((ROW_TILE * D // LANES, LANES), lambda i: (blk(i), 0)),
                  pl.BlockSpec((1, 1, D), lambda i: (blk(i), 0, 5)), vec, vec],
        out_specs=pl.BlockSpec((ROW_TILE, D), lambda i: (i, 0)),
        out_shape=jax.ShapeDtypeStruct((n_lat * ROW_TILE, D), F32),
        compiler_params=_cp(("parallel",)),
        name="final_norm",
    )(x1, y_tiles, modblk, ln_g.reshape(1, -1), ln_b.reshape(1, -1))


def _s5_matrices(a_re, a_im, log_dt, b_re, b_im):
    dt = jnp.exp(log_dt)[:, None]
    mag = jnp.exp(a_re * dt)
    ab_re, ab_im = mag * jnp.cos(a_im * dt), mag * jnp.sin(a_im * dt)
    den = a_re * a_re + a_im * a_im
    num_re, num_im = ab_re - 1.0, ab_im
    coef_re = (num_re * a_re + num_im * a_im) / den
    coef_im = (num_im * a_re - num_re * a_im) / den
    bb_re = coef_re[..., None] * b_re - coef_im[..., None] * b_im
    bb_im = coef_re[..., None] * b_im + coef_im[..., None] * b_re
    gs = S5_GROUPS // S5_SLABS
    eye = jnp.eye(gs, dtype=F32)

    def slabs(t):
        t = t.reshape(S5_SLABS, gs, S5_STATE, S5_GROUP_CH)
        return jnp.einsum('sgpi,gh->sgihp', t, eye).reshape(S5_SLABS, gs * S5_GROUP_CH, gs * S5_STATE)

    lam = jnp.stack([ab_re.reshape(-1), ab_im.reshape(-1)], 0)
    return lam, jnp.concatenate([slabs(bb_re), slabs(bb_im)], -1)


def _s5_readout_matrix(c_re, c_im):
    gs = S5_GROUPS // S5_SLABS
    eye = jnp.eye(gs, dtype=F32)

    def slabs(t):
        t = t.reshape(S5_SLABS, gs, S5_GROUP_CH, S5_STATE)
        return jnp.einsum('sgip,gh->sgphi', t, eye).reshape(S5_SLABS, gs * S5_STATE, gs * S5_GROUP_CH)

    return jnp.concatenate([slabs(c_re), -slabs(c_im)], 1)


def _mla_weights(w_uq, w_ukv):
    qk = MLA_NOPE_DIM + MLA_ROPE_DIM
    wq = w_uq.reshape(MLA_Q_LORA, MLA_HEADS, qk)
    wq = jnp.pad(wq, ((0, 0), (0, 0), (0, MLA_QK_PAD - qk))).reshape(MLA_Q_LORA, MLA_HEADS * MLA_QK_PAD)
    wkv = w_ukv.reshape(MLA_KV_LORA, MLA_HEADS, MLA_NOPE_DIM + MLA_V_DIM)
    wk = jnp.pad(wkv[..., :MLA_NOPE_DIM], ((0, 0), (0, 0), (0, MLA_QK_PAD - MLA_NOPE_DIM)))
    wv = wkv[..., MLA_NOPE_DIM:]
    wkv_pad = jnp.concatenate([wk.reshape(MLA_KV_LORA, -1), wv.reshape(MLA_KV_LORA, -1)], 1)
    return wq.astype(BF16), wkv_pad.astype(BF16)


def kernel(x, c, ctx, c_ctx, ada_w, ada_b, w_in, s5_a_re_f, s5_a_im_f, s5_log_dt_f, s5_a_re_b, s5_a_im_b,
           s5_log_dt_b, s5_b_re, s5_b_im, s5_c_re, s5_c_im, s5_d, s5_w_glu, gqa_q_norm, gqa_k_norm,
           ret_decay_f, ret_decay_b, ret_norm, mla_q_norm, mla_kv_norm, mla_w_uq, mla_w_ukv,
           w_branch, w_gate, b_gate, w_out, ln1_g, ln1_b, router_w, moe_w_gate, moe_w_up, moe_w_down,
           ln2_g, ln2_b):
    B, N, D = x.shape
    R = B * S_ALL
    assert (B, N, D) == (BATCH, SEQ, D_MODEL) and RET_CHUNK == CTX_LEN == ROW_TILE

    cc = jnp.zeros((16, D), F32).at[:B].set(c).at[B].set(c_ctx)
    mod = ada_modulation(cc, ada_w, ada_b)
    sel = np.concatenate([np.r_[np.full(LAT_BLOCKS, b), B] for b in range(B)]).astype(np.int32)
    modblks = [mod[l][sel].reshape(R // ROW_TILE, 1, 6 * D) for l in range(DEPTH)]

    X = jnp.concatenate([x, ctx], 1).reshape(R, D)
    h = modulate_rows(X, modblks[0], 0, 1)

    for l in range(DEPTH):
        need_ctx = l < DEPTH - 1
        modblk = modblks[l]
        w_in_p = jnp.pad(w_in[l], ((0, 0), (0, IN_PAD - IN_TOTAL))).astype(BF16)
        wq_pad, wkv_pad = _mla_weights(mla_w_uq[l], mla_w_ukv[l])
        u, gq, gk, gv, rq, rk, rv, rg, mq, mk, mv = in_proj_branches(
            h, w_in_p, gqa_q_norm[l], gqa_k_norm[l], mla_q_norm[l], mla_kv_norm[l], wq_pad, wkv_pad)
        to3 = lambda t: t.reshape(B, S_ALL, t.shape[-1])

        lam_f, bb_f = _s5_matrices(s5_a_re_f[l], s5_a_im_f[l], s5_log_dt_f[l], s5_b_re[l], s5_b_im[l])
        lam_b, bb_b = _s5_matrices(s5_a_re_b[l], s5_a_im_b[l], s5_log_dt_b[l], s5_b_re[l], s5_b_im[l])
        lam = jnp.broadcast_to(jnp.stack([lam_f, lam_b], 0)[:, :, None, :], (2, 2, S5_SEG, S5_MODES))
        bb = jnp.stack([bb_f, bb_b], 0).astype(BF16)
        cmat = _s5_readout_matrix(s5_c_re[l], s5_c_im[l]).astype(BF16)
        o_s5 = s5_output(s5_scan(u, bb, lam, cmat), u, s5_d[l], s5_w_glu[l].astype(BF16))

        gq3, gk3, gv3 = to3(gq), to3(gk), to3(gv)
        att = functools.partial(attention, kv_heads=GQA_KV_HEADS, groups=GQA_HEADS // GQA_KV_HEADS,
                                dk=GQA_HEAD_DIM, dv=GQA_HEAD_DIM)
        o_lat = att(gq3, gk3, gv3, q_rows=SEQ, q_off=0, kv_rows=S_ALL, kv_off=0, tq=256)
        no_ctx = jnp.zeros((B, CTX_LEN, BRANCH_W), BF16)
        o_ctx = (att(gq3, gk3, gv3, q_rows=CTX_LEN, q_off=SEQ, kv_rows=CTX_LEN, kv_off=SEQ, tq=256)
                 if need_ctx else no_ctx)
        o_gqa = jnp.concatenate([o_lat, o_ctx], 1).reshape(R, BRANCH_W)

        lgf = -jnp.exp(ret_decay_f[l])
        lgb = -jnp.exp(ret_decay_b[l])
        rkt = jnp.swapaxes(to3(rk), 1, 2)
        o_ret = retention(to3(rq), rkt, to3(rv), to3(rg), ret_norm[l], lgf, lgb, S_ALL).reshape(R, BRANCH_W)

        matt = functools.partial(attention, kv_heads=MLA_HEADS, groups=1, dk=MLA_QK_PAD, dv=MLA_V_DIM)
        m_lat = matt(to3(mq), to3(mk), to3(mv), q_rows=SEQ, q_off=0, kv_rows=S_ALL, kv_off=0, tq=512)
        m_ctx = (matt(to3(mq), to3(mk), to3(mv), q_rows=CTX_LEN, q_off=SEQ, kv_rows=CTX_LEN, kv_off=SEQ, tq=256)
                 if need_ctx else no_ctx)
        o_mla = jnp.concatenate([m_lat, m_ctx], 1).reshape(R, BRANCH_W)

        merged = merge_branches(h, (o_s5, o_gqa, o_ret, o_mla), w_gate, l, b_gate[l], w_branch[l].astype(BF16))
        router_pad = jnp.pad(router_w[l], ((0, 0), (0, 128 - N_EXPERTS)))
        x1, h2, logits = out_proj_norm(merged, X, w_out[l].astype(BF16), modblk, ln1_g[l], ln1_b[l], router_pad)

        logits_t = jnp.swapaxes(logits.reshape(B, S_ALL, 128)[:, :, :N_EXPERTS], 1, 2)
        sets = [(0, SEQ)] + ([(SEQ, CTX_LEN)] if need_ctx else [])
        idx_sets, row_parts, gate_parts = [], [], []
        sample_row0 = (jnp.arange(B, dtype=jnp.int32) * S_ALL)[:, None, None]
        for off, n in sets:
            idx_f, gate_f = route(logits_t, off, n)
            idx = jnp.swapaxes(idx_f[:, :, :N_EXPERTS], 1, 2).astype(jnp.int32)
            gate = jnp.swapaxes(gate_f[:, :, :N_EXPERTS], 1, 2)
            idx_sets.append(idx)
            row_parts.append(jnp.swapaxes(idx + sample_row0 + off, 0, 1).reshape(N_EXPERTS, -1))
            gate_parts.append(jnp.swapaxes(gate, 0, 1).reshape(N_EXPERTS, -1))
        rows = jnp.concatenate(row_parts, 1)[:, None, :]
        gates = jnp.concatenate(gate_parts, 1)[:, :, None]
        xs = gather_rows(rows, h2, D)
        ys = expert_ffn(xs, gates, moe_w_gate, moe_w_up, moe_w_down, l)
        moe = combine(ys, idx_sets, [off for off, _ in sets]).reshape(R * D // LANES, LANES)
        if l == DEPTH - 1:
            return final_norm(x1, moe, modblk, ln2_g[l], ln2_b[l]).reshape(B, SEQ, D)
        X, h = post_moe_norm(x1, moe, modblk, ln2_g[l], ln2_b[l], modblks[l + 1])
```

```python
import functools
import math

import numpy as np
import jax
import jax.numpy as jnp
from jax import lax
from jax.experimental import pallas as pl
from jax.experimental.pallas import tpu as pltpu

F32 = jnp.float32
BF16 = jnp.bfloat16

D_MODEL = 2048
BATCH = 2
SEQ = 4096
DEPTH = 2
GRID_W = 64
CTX_LEN = 256
S_ALL = SEQ + CTX_LEN
N_BRANCH = 4
BRANCH_W = D_MODEL // 4
S5_GROUP_CH = 16
S5_GROUPS = BRANCH_W // S5_GROUP_CH
S5_STATE = 64
S5_MODES = S5_GROUPS * S5_STATE
GQA_HEAD_DIM = 128
GQA_HEADS = 4
GQA_KV_HEADS = 2
RET_HEADS = 4
RET_V_DIM = 128
RET_QK_DIM = 64
MLA_HEADS = 4
MLA_Q_LORA = 512
MLA_KV_LORA = 256
MLA_NOPE_DIM = 128
MLA_ROPE_DIM = 64
MLA_V_DIM = 128
MLA_QK_PAD = 256
N_EXPERTS = 16
EXPERT_FF = D_MODEL // 2
EC_CAPACITY_FACTOR = 2
ROPE_BASE = 10000.0
NORM_EPS = 1e-6
LOG2E = math.log2(math.e)
DEEPNORM_ALPHA = (2 * DEPTH) ** 0.25
IN_WIDTHS = (512, 512, 256, 256, 256, 256, 512, 512, 512, 256, 64)
IN_TOTAL = sum(IN_WIDTHS)
IN_PAD = 4096

LANES = 128
SUBLANES = 8
ROW_TILE = 256
MM_ROW_TILE = 512
MERGE_ROW_TILE = 1088
ATTN_SUB = 256
BLOCKS_PER_SAMPLE = S_ALL // ROW_TILE
LAT_BLOCKS = SEQ // ROW_TILE
S5_SLABS = 4
S5_SEG = 8
S5_STEPS = ROW_TILE // S5_SEG
RET_CHUNK = 256
VMEM_LIMIT = 56 * 1024 * 1024


def _cp(sem, vmem=VMEM_LIMIT):
    return pltpu.CompilerParams(dimension_semantics=sem, vmem_limit_bytes=vmem)


def _standardize(x):
    xc = x - jnp.mean(x, -1, keepdims=True)
    return xc * lax.rsqrt(jnp.mean(xc * xc, -1, keepdims=True) + NORM_EPS)


def _sigmoid(x):
    return 1.0 / (1.0 + jnp.exp(-x))


def _store_token_tiles(ref, x, row0=0):
    rows, w = x.shape
    tt = w // LANES
    for j in range(tt):
        ref[pl.ds(row0 * tt + j, rows, stride=tt), :] = x[:, j * LANES:(j + 1) * LANES]


def _load_token_tiles(ref, rows, w, dtype=F32):
    tt = w // LANES
    return jnp.concatenate([ref[pl.ds(j, rows, stride=tt), :].astype(dtype) for j in range(tt)], -1)


def _ada_kernel(c_ref, w_ref, b_ref, o_ref):
    c = c_ref[...]
    cs = (c * _sigmoid(c)).astype(BF16)
    o_ref[0] = jnp.dot(cs, w_ref[0].astype(BF16), preferred_element_type=F32) + b_ref[0]


def ada_modulation(cc, ada_w, ada_b, tn=1024):
    L, D, N = ada_w.shape
    return pl.pallas_call(
        _ada_kernel,
        grid=(L, N // tn),
        in_specs=[pl.BlockSpec((16, D), lambda l, n: (0, 0)),
                  pl.BlockSpec((1, D, tn), lambda l, n: (l, 0, n)),
                  pl.BlockSpec((1, 1, tn), lambda l, n: (l, 0, n))],
        out_specs=pl.BlockSpec((1, 16, tn), lambda l, n: (l, 0, n)),
        out_shape=jax.ShapeDtypeStruct((L, 16, N), F32),
        compiler_params=_cp(("arbitrary", "arbitrary")),
        name="ada_modulation",
    )(cc, ada_w, ada_b.reshape(L, 1, N))


def _modulate_kernel(x_ref, sh_ref, sc_ref, o_ref):
    o_ref[...] = (_standardize(x_ref[...]) * (1.0 + sc_ref[0]) + sh_ref[0]).astype(o_ref.dtype)


def modulate_rows(x, modblk, shift_part, scale_part):
    R, D = x.shape
    return pl.pallas_call(
        _modulate_kernel,
        grid=(R // ROW_TILE,),
        in_specs=[pl.BlockSpec((ROW_TILE, D), lambda i: (i, 0)),
                  pl.BlockSpec((1, 1, D), lambda i: (i, 0, shift_part)),
                  pl.BlockSpec((1, 1, D), lambda i: (i, 0, scale_part))],
        out_specs=pl.BlockSpec((ROW_TILE, D), lambda i: (i, 0)),
        out_shape=jax.ShapeDtypeStruct((R, D), BF16),
        compiler_params=_cp(("parallel",)),
        name="modulate_rows",
    )(x, modblk, modblk)


def _rope_tables(head_dim, width):
    half = head_dim // 2
    n = half // 2
    inv = ROPE_BASE ** (-np.arange(n, dtype=np.float64) / n)
    t = np.arange(SEQ)
    pos = np.stack([t // GRID_W, t % GRID_W], 0).astype(np.float64)
    lane = np.arange(head_dim)
    which = lane // half
    m = lane % half
    ang = (pos[which, :].T.astype(np.float32) * inv[m % n].astype(np.float32)[None, :]).astype(np.float64)
    cos = np.cos(ang)
    sin = np.where(m < n, -np.sin(ang), np.sin(ang))
    cos = np.concatenate([cos, np.ones((CTX_LEN, head_dim))], 0)
    sin = np.concatenate([sin, np.zeros((CTX_LEN, head_dim))], 0)
    reps = width // head_dim
    return (np.tile(cos, (1, reps)).astype(np.float32), np.tile(sin, (1, reps)).astype(np.float32))


def _mla_q_tables():
    cos64, sin64 = _rope_tables(MLA_ROPE_DIM, MLA_ROPE_DIM)
    ones = np.ones((S_ALL, MLA_NOPE_DIM), np.float32)
    zeros = np.zeros((S_ALL, MLA_NOPE_DIM), np.float32)
    pad1 = np.ones((S_ALL, MLA_QK_PAD - MLA_NOPE_DIM - MLA_ROPE_DIM), np.float32)
    cos = np.concatenate([ones, cos64, pad1], 1)
    sin = np.concatenate([zeros, sin64, 0 * pad1], 1)
    return cos, sin


def _rope(x, cos, sin, quarter):
    w = x.shape[-1]
    lane = lax.broadcasted_iota(jnp.int32, x.shape, 1)
    first = (lane % (2 * quarter)) < quarter
    partner = jnp.where(first, pltpu.roll(x, w - quarter, 1), pltpu.roll(x, quarter, 1))
    return x * cos + partner * sin


def _rms_heads(x, gain, head_dim):
    outs = []
    for h in range(x.shape[-1] // head_dim):
        xh = x[:, h * head_dim:(h + 1) * head_dim]
        outs.append(xh * lax.rsqrt(jnp.mean(xh * xh, -1, keepdims=True) + NORM_EPS) * gain)
    return outs[0] if len(outs) == 1 else jnp.concatenate(outs, -1)


PROJ_TILE = 1024


def _tab(ref_a, ref_b):
    return jnp.concatenate([ref_a[...], ref_b[...]], 0)


def _proj_s5_gq_kernel(h_ref, w_ref, ca, cb, sa, sb, gqn_ref, u_ref, gq_ref):
    p = jnp.dot(h_ref[...], w_ref[...], preferred_element_type=F32)
    u_ref[...] = p[:, :512]
    c128, s128 = _tab(ca, cb), _tab(sa, sb)
    q = _rms_heads(p[:, 512:], gqn_ref[...], GQA_HEAD_DIM)
    q = _rope(q, jnp.concatenate([c128] * 4, -1), jnp.concatenate([s128] * 4, -1), GQA_HEAD_DIM // 4)
    gq_ref[...] = (q * (GQA_HEAD_DIM ** -0.5 * LOG2E)).astype(BF16)


def _proj_kv_ret_kernel(h_ref, w_ref, ca, cb, sa, sb, c6a, c6b, s6a, s6b, gkn_ref,
                        gk_ref, gv_ref, rq_ref, rk_ref):
    p = jnp.dot(h_ref[...], w_ref[...], preferred_element_type=F32)
    tile2 = lambda t: jnp.concatenate([t, t], -1)
    c128, s128, c64, s64 = _tab(ca, cb), _tab(sa, sb), _tab(c6a, c6b), _tab(s6a, s6b)
    k = _rms_heads(p[:, :256], gkn_ref[...], GQA_HEAD_DIM)
    gk_ref[...] = _rope(k, tile2(c128), tile2(s128), GQA_HEAD_DIM // 4).astype(BF16)
    gv_ref[...] = p[:, 256:512].astype(BF16)
    rq_ref[...] = _rope(p[:, 512:768], tile2(c64), tile2(s64), RET_QK_DIM // 4).astype(BF16)
    rk = _rope(p[:, 768:], tile2(c64), tile2(s64), RET_QK_DIM // 4)
    rk_ref[...] = (rk * (RET_QK_DIM ** -0.5)).astype(BF16)


def _proj_ret_vg_kernel(h_ref, w_ref, rv_ref, rg_ref):
    p = jnp.dot(h_ref[...], w_ref[...], preferred_element_type=F32)
    rv_ref[...] = p[:, :512].astype(BF16)
    rg_ref[...] = p[:, 512:]


def _proj_mla_kernel(h_ref, w_ref, c6a, c6b, s6a, s6b, cqa, cqb, sqa, sqb, mqn_ref, mkvn_ref, wq_ref, wkv_ref,
                     q_ref, k_ref, v_ref):
    p = jnp.dot(h_ref[...], w_ref[...], preferred_element_type=F32)
    tm = p.shape[0]
    cqn = _rms_heads(p[:, :512], mqn_ref[...], MLA_Q_LORA).astype(BF16)
    ckvn = _rms_heads(p[:, 512:768], mkvn_ref[...], MLA_KV_LORA).astype(BF16)
    lane = lax.broadcasted_iota(jnp.int32, (tm, 128), 1)
    kr = jnp.where(lane < MLA_ROPE_DIM, _rope(p[:, 768:896], _tab(c6a, c6b), _tab(s6a, s6b), MLA_ROPE_DIM // 4), 0.0)
    scale = (MLA_NOPE_DIM + MLA_ROPE_DIM) ** -0.5 * LOG2E
    q = jnp.dot(cqn, wq_ref[...], preferred_element_type=F32)
    kv = jnp.dot(ckvn, wkv_ref[...], preferred_element_type=F32)
    cos, sin = _tab(cqa, cqb), _tab(sqa, sqb)
    krp = jnp.concatenate([jnp.zeros((tm, MLA_NOPE_DIM), F32), kr], -1)
    for h in range(MLA_HEADS):
        sl = slice(h * MLA_QK_PAD, (h + 1) * MLA_QK_PAD)
        q_ref[:, sl] = (_rope(q[:, sl], cos, sin, MLA_ROPE_DIM // 4) * scale).astype(BF16)
        k_ref[:, sl] = (kv[:, sl] + krp).astype(BF16)
    v_ref[...] = kv[:, MLA_HEADS * MLA_QK_PAD:].astype(BF16)


def in_proj_branches(h, w_in_p, gqa_q_norm, gqa_k_norm, mla_q_norm, mla_kv_norm, wq_pad, wkv_pad,
                     tm=MM_ROW_TILE):
    R, D = h.shape
    half = tm // 2
    assert half == ROW_TILE
    c128, s128 = (jnp.asarray(t) for t in _rope_tables(GQA_HEAD_DIM, 128))
    c64, s64 = (jnp.asarray(t) for t in _rope_tables(RET_QK_DIM, 128))
    cq, sq = (jnp.asarray(t) for t in _mla_q_tables())
    x_spec = pl.BlockSpec((tm, D), lambda i: (i, 0))
    w_spec = lambda n: pl.BlockSpec((D, PROJ_TILE), lambda i: (0, n))
    row = lambda w: pl.BlockSpec((tm, w), lambda i: (i, 0))
    vec = lambda w: pl.BlockSpec((1, w), lambda i: (0, 0))
    full = lambda a: pl.BlockSpec(a.shape, lambda i: (0, 0))

    def tabs(t):
        w = t.shape[1]
        return ([pl.BlockSpec((half, w), lambda i: ((2 * i) % BLOCKS_PER_SAMPLE, 0)),
                 pl.BlockSpec((half, w), lambda i: ((2 * i + 1) % BLOCKS_PER_SAMPLE, 0))], [t, t])

    def call(kernel, n, extra_specs, extra_args, outs, name):
        return pl.pallas_call(
            kernel, grid=(R // tm,),
            in_specs=[x_spec, w_spec(n)] + extra_specs,
            out_specs=[row(w) for w, _ in outs],
            out_shape=[jax.ShapeDtypeStruct((R, w), dt) for w, dt in outs],
            compiler_params=_cp(("parallel",)), name=name,
        )(h, w_in_p, *extra_args)

    def gather_tabs(*ts):
        specs, args = [], []
        for t in ts:
            s, a = tabs(t)
            specs += s
            args += a
        return specs, args

    s0, a0 = gather_tabs(c128, s128)
    u, gq = call(_proj_s5_gq_kernel, 0, s0 + [vec(128)], a0 + [gqa_q_norm.reshape(1, -1)],
                 [(512, F32), (512, BF16)], "proj_s5_gq")
    s1, a1 = gather_tabs(c128, s128, c64, s64)
    gk, gv, rq, rk = call(_proj_kv_ret_kernel, 1, s1 + [vec(128)], a1 + [gqa_k_norm.reshape(1, -1)],
                          [(256, BF16)] * 4, "proj_kv_ret")
    rv, rg = call(_proj_ret_vg_kernel, 2, [], [], [(512, BF16), (512, F32)], "proj_ret_vg")
    s3, a3 = gather_tabs(c64, s64, cq, sq)
    hq = MLA_HEADS * MLA_QK_PAD
    mq, mk, mv = call(_proj_mla_kernel, 3, s3 + [vec(512), vec(256), full(wq_pad), full(wkv_pad)],
                      a3 + [mla_q_norm.reshape(1, -1), mla_kv_norm.reshape(1, -1), wq_pad, wkv_pad],
                      [(hq, BF16), (hq, BF16), (MLA_HEADS * MLA_V_DIM, BF16)], "proj_mla")
    return u, gq, gk, gv, rq, rk, rv, rg, mq, mk, mv


def _attn_kernel(q_ref, k_ref, v_ref, o_ref, *, groups, dk, dv):
    k = k_ref[0]
    v = v_ref[0]
    v1 = jnp.concatenate([v, jnp.ones_like(v)], -1)
    tq = q_ref.shape[1]
    q = jnp.concatenate([q_ref[0, :, g * dk:(g + 1) * dk] for g in range(groups)], 0)
    s = lax.dot_general(q, k, (((1,), (1,)), ((), ())), preferred_element_type=F32)
    for g in range(groups):
        for half in range(tq // ATTN_SUB):
            r0 = g * tq + half * ATTN_SUB
            sh = s[r0:r0 + ATTN_SUB]
            p = jnp.exp2(sh - jnp.max(sh, -1, keepdims=True)).astype(BF16)
            o = jnp.dot(p, v1, preferred_element_type=F32)
            rows = slice(half * ATTN_SUB, (half + 1) * ATTN_SUB)
            o_ref[0, rows, g * dv:(g + 1) * dv] = (o[:, :dv] / o[:, dv:]).astype(o_ref.dtype)


def attention(q, k, v, *, kv_heads, groups, dk, dv, q_rows, q_off, kv_rows, kv_off, tq):
    B = q.shape[0]
    assert tq % ATTN_SUB == 0 and q_off % tq == 0 and q_rows % tq == 0
    qb0, kb0 = q_off // tq, kv_off // kv_rows
    return pl.pallas_call(
        functools.partial(_attn_kernel, groups=groups, dk=dk, dv=dv),
        grid=(B, kv_heads, q_rows // tq),
        in_specs=[pl.BlockSpec((1, tq, groups * dk), lambda b, h, i: (b, qb0 + i, h)),
                  pl.BlockSpec((1, kv_rows, dk), lambda b, h, i: (b, kb0, h)),
                  pl.BlockSpec((1, kv_rows, dv), lambda b, h, i: (b, kb0, h))],
        out_specs=pl.BlockSpec((1, tq, groups * dv), lambda b, h, i: (b, i, h)),
        out_shape=jax.ShapeDtypeStruct((B, q_rows, kv_heads * groups * dv), BF16),
        compiler_params=_cp(("parallel", "parallel", "arbitrary")),
        name="attention",
    )(q, k, v)


def _ret_kernel(lgf_ref, lgb_ref, q_ref, kt_ref, v_ref, g_ref, gain_ref, o_ref, sb_ref, *, n_out):
    L = RET_CHUNK
    nc = SEQ // L
    pair = pl.program_id(1)
    r_i = lax.broadcasted_iota(jnp.int32, (L, L), 0)
    c_i = lax.broadcasted_iota(jnp.int32, (L, L), 1)
    diff = (r_i - c_i).astype(F32)
    pos_col = lax.broadcasted_iota(jnp.int32, (L, 1), 0).astype(F32)
    pos_row = lax.broadcasted_iota(jnp.int32, (1, L), 1).astype(F32)
    for j in range(2):
        lgf = lgf_ref[pair * 2 + j]
        lgb = lgb_ref[pair * 2 + j]
        dmat = jnp.where(diff >= 0, jnp.exp(lgf * jnp.maximum(diff, 0.0)), jnp.exp(lgb * jnp.maximum(-diff, 0.0)))
        dq_f = jnp.exp(lgf * (pos_col + 1.0))
        dq_b = jnp.exp(lgb * (L - pos_col))
        dk_f = jnp.exp(lgf * (L - 1.0 - pos_row))
        dk_b = jnp.exp(lgb * pos_row)
        dc_f = jnp.exp(lgf * L)
        dc_b = jnp.exp(lgb * L)
        qs = slice(j * RET_QK_DIM, (j + 1) * RET_QK_DIM)
        vs = slice(j * RET_V_DIM, (j + 1) * RET_V_DIM)

        def chunk(c):
            rows = slice(c * L, (c + 1) * L)
            return q_ref[0, rows, qs], kt_ref[0, qs, rows], v_ref[0, rows, vs]

        def readout(o, c, out_row0):
            oc = o - jnp.mean(o, -1, keepdims=True)
            on = oc * lax.rsqrt(jnp.mean(oc * oc, -1, keepdims=True) + NORM_EPS)
            g = g_ref[0, c * L:(c + 1) * L, vs]
            o_ref[0, out_row0:out_row0 + L, vs] = (on * gain_ref[:, vs] * (g * _sigmoid(g))).astype(o_ref.dtype)

        def intra(q, kt, v):
            sc = jnp.dot(q, kt, preferred_element_type=F32) * dmat
            return jnp.dot(sc.astype(BF16), v, preferred_element_type=F32)

        def state_add(kt, dk, v):
            return jnp.dot((kt.astype(F32) * dk).astype(BF16), v, preferred_element_type=F32)

        qc, ktc, vc = chunk(nc)
        if n_out > SEQ:
            readout(intra(qc, ktc, vc), nc, SEQ)
        s_f = state_add(ktc, dk_f, vc)
        s_b = state_add(ktc, dk_b, vc)
        for c in range(nc - 1, -1, -1):
            sb_ref[c] = s_b
            _, kt, v = chunk(c)
            s_b = s_b * dc_b + state_add(kt, dk_b, v)
        for c in range(nc):
            q, kt, v = chunk(c)
            qf = q.astype(F32)
            o = (intra(q, kt, v)
                 + jnp.dot((qf * dq_f).astype(BF16), s_f.astype(BF16), preferred_element_type=F32)
                 + jnp.dot((qf * dq_b).astype(BF16), sb_ref[c].astype(BF16), preferred_element_type=F32))
            readout(o, c, c * L)
            s_f = s_f * dc_f + state_add(kt, dk_f, v)


def retention(rq, rkt, rv, rg, gain, lgf, lgb, n_out):
    B = rq.shape[0]
    smem = pl.BlockSpec(memory_space=pltpu.SMEM)
    return pl.pallas_call(
        functools.partial(_ret_kernel, n_out=n_out),
        grid=(B, RET_HEADS // 2),
        in_specs=[smem, smem,
                  pl.BlockSpec((1, S_ALL, 128), lambda b, p: (b, 0, p)),
                  pl.BlockSpec((1, 128, S_ALL), lambda b, p: (b, p, 0)),
                  pl.BlockSpec((1, S_ALL, 256), lambda b, p: (b, 0, p)),
                  pl.BlockSpec((1, S_ALL, 256), lambda b, p: (b, 0, p)),
                  pl.BlockSpec((1, 256), lambda b, p: (0, p))],
        out_specs=pl.BlockSpec((1, n_out, 256), lambda b, p: (b, 0, p)),
        out_shape=jax.ShapeDtypeStruct((B, n_out, RET_HEADS * RET_V_DIM), BF16),
        scratch_shapes=[pltpu.VMEM((SEQ // RET_CHUNK, RET_QK_DIM, RET_V_DIM), F32)],
        compiler_params=_cp(("parallel", "parallel")),
        name="retention",
    )(lgf, lgb, rq, rkt, rv, rg, gain.reshape(1, -1))


def _s5_kernel(u_ref, bb_ref, lam_ref, cm_ref, y_ref, bu_ref, pw_ref, st_ref):
    M = S5_MODES
    chunk = pl.program_id(2)
    lam_re = lam_ref[0, 0]
    lam_im = lam_ref[0, 1]

    @pl.when(chunk == 0)
    def _():
        st_ref[...] = jnp.zeros_like(st_ref)
        p_re, p_im = lam_re, lam_im
        for j in range(S5_STEPS):
            pw_ref[0, j] = p_re
            pw_ref[1, j] = p_im
            p_re, p_im = p_re * lam_re - p_im * lam_im, p_re * lam_im + p_im * lam_re

    r_i = lax.broadcasted_iota(jnp.int32, (ROW_TILE, ROW_TILE), 0)
    c_i = lax.broadcasted_iota(jnp.int32, (ROW_TILE, ROW_TILE), 1)
    flip = pl.program_id(0) == 1

    def scan_time(r):
        t = (r % S5_SEG) * S5_STEPS + r // S5_SEG
        return jnp.where(flip, ROW_TILE - 1 - t, t)

    to_scan = (c_i == scan_time(r_i)).astype(BF16)
    to_time = (r_i == scan_time(c_i)).astype(BF16)
    u = jnp.dot(to_scan, u_ref[...].astype(BF16), preferred_element_type=F32).astype(BF16)

    slab = BRANCH_W // S5_SLABS
    ms = M // S5_SLABS
    for s in range(S5_SLABS):
        part = jnp.dot(u[:, s * slab:(s + 1) * slab], bb_ref[0, s], preferred_element_type=F32)
        bu_ref[:, s * ms:(s + 1) * ms] = part[:, :ms]
        bu_ref[:, M + s * ms:M + (s + 1) * ms] = part[:, ms:]

    tile = 512
    for t in range(M // tile):
        re_sl = slice(t * tile, (t + 1) * tile)
        im_sl = slice(M + t * tile, M + (t + 1) * tile)
        lr, li = lam_re[:, re_sl], lam_im[:, re_sl]

        def step(j, carry):
            s_re, s_im = carry
            rows = pl.ds(pl.multiple_of(j * S5_SEG, S5_SEG), S5_SEG)
            n_re = lr * s_re - li * s_im + bu_ref[rows, re_sl]
            n_im = lr * s_im + li * s_re + bu_ref[rows, im_sl]
            bu_ref[rows, re_sl] = n_re
            bu_ref[rows, im_sl] = n_im
            return n_re, n_im

        z = jnp.zeros((S5_SEG, tile), F32)
        lax.fori_loop(0, S5_STEPS, step, (z, z))

    last = slice((S5_STEPS - 1) * S5_SEG, S5_STEPS * S5_SEG)
    e_re, e_im = bu_ref[last, 0:M], bu_ref[last, M:2 * M]
    pl_re, pl_im = pw_ref[0, S5_STEPS - 1][0:1], pw_ref[1, S5_STEPS - 1][0:1]
    c_re, c_im = st_ref[0:1, :], st_ref[1:2, :]
    rows_re, rows_im = [], []
    for k in range(S5_SEG):
        rows_re.append(c_re)
        rows_im.append(c_im)
        c_re, c_im = (e_re[k:k + 1] + pl_re * c_re - pl_im * c_im,
                      e_im[k:k + 1] + pl_re * c_im + pl_im * c_re)
    st_ref[0:1, :] = c_re
    st_ref[1:2, :] = c_im
    car_re = jnp.concatenate(rows_re, 0)
    car_im = jnp.concatenate(rows_im, 0)

    def fix(j, _):
        rows = pl.ds(pl.multiple_of(j * S5_SEG, S5_SEG), S5_SEG)
        p_re, p_im = pw_ref[0, j], pw_ref[1, j]
        bu_ref[rows, 0:M] = bu_ref[rows, 0:M] + p_re * car_re - p_im * car_im
        bu_ref[rows, M:2 * M] = bu_ref[rows, M:2 * M] + p_re * car_im + p_im * car_re
        return 0

    lax.fori_loop(0, S5_STEPS, fix, 0)

    ys = []
    for s in range(S5_SLABS):
        hs = jnp.concatenate([bu_ref[:, s * ms:(s + 1) * ms], bu_ref[:, M + s * ms:M + (s + 1) * ms]], -1)
        ys.append(jnp.dot(hs.astype(BF16), cm_ref[s], preferred_element_type=F32))
    y = jnp.concatenate(ys, -1)
    out = None
    for _ in range(2):
        piece = y.astype(BF16)
        y = y - piece.astype(F32)
        term = jnp.dot(to_time, piece, preferred_element_type=F32)
        out = term if out is None else out + term
    y_ref[0] = out


def _s5_block(d, c):
    fwd = (c + LAT_BLOCKS) % BLOCKS_PER_SAMPLE
    bwd = jnp.where(c == 0, LAT_BLOCKS, LAT_BLOCKS - c)
    return jnp.where(d == 0, fwd, bwd)


def s5_scan(u, bb, lam, cmat):
    R = u.shape[0]
    B = R // S_ALL
    W = BRANCH_W
    row_block = lambda d, b, c: b * BLOCKS_PER_SAMPLE + _s5_block(d, c)
    return pl.pallas_call(
        _s5_kernel,
        grid=(2, B, BLOCKS_PER_SAMPLE),
        in_specs=[pl.BlockSpec((ROW_TILE, W), lambda d, b, c: (row_block(d, b, c), 0)),
                  pl.BlockSpec((1,) + bb.shape[1:], lambda d, b, c: (d, 0, 0, 0)),
                  pl.BlockSpec((1, 2, S5_SEG, S5_MODES), lambda d, b, c: (d, 0, 0, 0)),
                  pl.BlockSpec(cmat.shape, lambda d, b, c: (0, 0, 0))],
        out_specs=pl.BlockSpec((1, ROW_TILE, W), lambda d, b, c: (d, row_block(d, b, c), 0)),
        out_shape=jax.ShapeDtypeStruct((2, R, W), F32),
        scratch_shapes=[pltpu.VMEM((ROW_TILE, 2 * S5_MODES), F32),
                        pltpu.VMEM((2, S5_STEPS, S5_SEG, S5_MODES), F32),
                        pltpu.VMEM((8, S5_MODES), F32)],
        compiler_params=_cp(("arbitrary", "arbitrary", "arbitrary")),
        name="s5_scan",
    )(u, bb, lam, cmat)


def _s5_out_kernel(yf_ref, yb_ref, p_ref, d_ref, w_ref, o_ref):
    y = yf_ref[0] + yb_ref[0] + d_ref[...] * p_ref[...]
    z = 0.5 * y * (1.0 + jnp.tanh(math.sqrt(2.0 / math.pi) * (y + 0.044715 * (y * y * y))))
    gate = _sigmoid(jnp.dot(z.astype(BF16), w_ref[...], preferred_element_type=F32))
    o_ref[...] = (z * gate).astype(o_ref.dtype)


def s5_output(y_dirs, u, d, w_glu):
    R = y_dirs.shape[1]
    return pl.pallas_call(
        _s5_out_kernel,
        grid=(R // ROW_TILE,),
        in_specs=[pl.BlockSpec((1, ROW_TILE, 512), lambda i: (0, i, 0)),
                  pl.BlockSpec((1, ROW_TILE, 512), lambda i: (1, i, 0)),
                  pl.BlockSpec((ROW_TILE, 512), lambda i: (i, 0)),
                  pl.BlockSpec((1, 512), lambda i: (0, 0)),
                  pl.BlockSpec((512, 512), lambda i: (0, 0))],
        out_specs=pl.BlockSpec((ROW_TILE, 512), lambda i: (i, 0)),
        out_shape=jax.ShapeDtypeStruct((R, 512), BF16),
        compiler_params=_cp(("parallel",)),
        name="s5_output",
    )(y_dirs, y_dirs, u, d.reshape(1, -1), w_glu)


def _merge_kernel(h_ref, o0_ref, o1_ref, o2_ref, o3_ref, g0_ref, g1_ref, g2_ref, g3_ref,
                  bg_ref, wb_ref, out_ref, wg_ref):
    @pl.when(pl.program_id(1) == 0)
    def _():
        for k, g in enumerate((g0_ref, g1_ref, g2_ref, g3_ref)):
            wg_ref[k] = g[0].astype(BF16)

    h = h_ref[...]
    acc = None
    for k, o in enumerate((o0_ref, o1_ref, o2_ref, o3_ref)):
        gate = _sigmoid(jnp.dot(h, wg_ref[k], preferred_element_type=F32) + bg_ref[k])
        term = gate * jnp.dot(o[...], wb_ref[k], preferred_element_type=F32)
        acc = term if acc is None else acc + term
    out_ref[...] = acc.astype(out_ref.dtype)


def merge_branches(h, outs, w_gate_all, layer, b_gate, w_branch, tn=256, tm=MERGE_ROW_TILE):
    R, D = h.shape
    nb = D // tn
    gate_spec = lambda k: pl.BlockSpec((1, D, tn), lambda n, m: (layer, 0, k * nb + n))
    bg = b_gate.reshape(N_BRANCH, 1, D)
    return pl.pallas_call(
        _merge_kernel,
        grid=(nb, R // tm),
        in_specs=[pl.BlockSpec((tm, D), lambda n, m: (m, 0))]
                 + [pl.BlockSpec((tm, BRANCH_W), lambda n, m: (m, 0))] * N_BRANCH
                 + [gate_spec(k) for k in range(N_BRANCH)]
                 + [pl.BlockSpec((N_BRANCH, 1, tn), lambda n, m: (0, 0, n)),
                    pl.BlockSpec((N_BRANCH, BRANCH_W, tn), lambda n, m: (0, 0, n))],
        out_specs=pl.BlockSpec((tm, tn), lambda n, m: (m, n)),
        out_shape=jax.ShapeDtypeStruct((R, D), BF16),
        scratch_shapes=[pltpu.VMEM((N_BRANCH, D, tn), BF16)],
        compiler_params=_cp(("arbitrary", "arbitrary")),
        name="merge_branches",
    )(h, *outs, w_gate_all, w_gate_all, w_gate_all, w_gate_all, bg, w_branch)


def _out_kernel(m_ref, x_ref, w_ref, g1_ref, lg_ref, lb_ref, sh_ref, sc_ref, rw_ref,
                x1_ref, h2_ref, lo_ref):
    y = jnp.dot(m_ref[...], w_ref[...], preferred_element_type=F32)
    x1 = _standardize(DEEPNORM_ALPHA * x_ref[...] + g1_ref[0] * y) * lg_ref[...] + lb_ref[...]
    x1_ref[...] = x1
    h2 = _standardize(x1) * (1.0 + sc_ref[0]) + sh_ref[0]
    _store_token_tiles(h2_ref, h2)
    rw = rw_ref[...]
    h_hi, rw_hi = h2.astype(BF16), rw.astype(BF16)
    h_lo, rw_lo = (h2 - h_hi.astype(F32)).astype(BF16), (rw - rw_hi.astype(F32)).astype(BF16)
    lo_ref[...] = (jnp.dot(h_hi, rw_hi, preferred_element_type=F32)
                   + jnp.dot(h_lo, rw_hi, preferred_element_type=F32)
                   + jnp.dot(h_hi, rw_lo, preferred_element_type=F32))


def out_proj_norm(merged, x, w_out, modblk, ln_g, ln_b, router_pad):
    R, D = x.shape
    tt = D // LANES
    row = lambda w: pl.BlockSpec((ROW_TILE, w), lambda i: (i, 0))
    mod = lambda part: pl.BlockSpec((1, 1, D), lambda i: (i, 0, part))
    vec = pl.BlockSpec((1, D), lambda i: (0, 0))
    return pl.pallas_call(
        _out_kernel,
        grid=(R // ROW_TILE,),
        in_specs=[row(D), row(D), pl.BlockSpec((D, D), lambda i: (0, 0)), mod(2), vec, vec, mod(3), mod(4),
                  pl.BlockSpec((D, 128), lambda i: (0, 0))],
        out_specs=[row(D), pl.BlockSpec((ROW_TILE * tt, LANES), lambda i: (i, 0)), row(128)],
        out_shape=[jax.ShapeDtypeStruct((R, D), F32), jax.ShapeDtypeStruct((R * tt, LANES), F32),
                   jax.ShapeDtypeStruct((R, 128), F32)],
        compiler_params=_cp(("parallel",)),
        name="out_proj_norm",
    )(merged, x, w_out, modblk, ln_g.reshape(1, -1), ln_b.reshape(1, -1), modblk, modblk, router_pad)


ROUTE_GEOMETRIC_STEPS = 40
ROUTE_BISECT_STEPS = ROUTE_GEOMETRIC_STEPS + 8


def _route_kernel(lt_ref, idx_ref, gate_ref, slot_ref, aff_ref, *, n, cap):
    E = N_EXPERTS
    lt = lt_ref[0]
    ex = jnp.exp(lt - jnp.max(lt, 0, keepdims=True))
    aff = ex / jnp.sum(ex, 0, keepdims=True)

    def count_ge(v):
        return jnp.sum((aff >= v).astype(F32), 1, keepdims=True)

    tiny = jnp.full((E, 1), float(np.finfo(np.float32).tiny), F32)
    normal = count_ge(tiny) >= cap
    lo0 = jnp.where(normal, tiny, 0.0)
    hi0 = jnp.where(normal, 2.0, tiny)

    def bisect(i, lo_hi):
        lo, hi = lo_hi
        geo = jnp.clip(jnp.sqrt(lo) * jnp.sqrt(hi), lo, hi)
        mid = jnp.where(jnp.logical_and(i < ROUTE_GEOMETRIC_STEPS, lo > 0.0), geo, lo + 0.5 * (hi - lo))
        ok = count_ge(mid) >= cap
        return jnp.where(ok, mid, lo), jnp.where(ok, hi, mid)

    lo, _ = lax.fori_loop(0, ROUTE_BISECT_STEPS, bisect, (lo0, hi0))
    thr = jnp.min(jnp.where(aff >= lo, aff, 2.0), 1, keepdims=True)
    gt = aff > thr
    eq = aff == thr
    need = cap - jnp.sum(gt.astype(F32), 1, keepdims=True)

    blk = min(n, 512)
    upper = (lax.broadcasted_iota(jnp.int32, (blk, blk), 0)
             < lax.broadcasted_iota(jnp.int32, (blk, blk), 1)).astype(BF16)

    def excl_cumsum(mask):
        parts, carry = [], jnp.zeros((E, 1), F32)
        for j in range(n // blk):
            m = mask[:, j * blk:(j + 1) * blk].astype(F32)
            parts.append(jnp.dot(m.astype(BF16), upper, preferred_element_type=F32) + carry)
            carry = carry + jnp.sum(m, 1, keepdims=True)
        return parts[0] if len(parts) == 1 else jnp.concatenate(parts, 1)

    sel = gt | (eq & (excl_cumsum(eq) < need))
    slot_ref[...] = jnp.where(sel, excl_cumsum(sel), -1.0)
    idx_ref[0] = jnp.zeros((cap, 128), F32)
    gate_ref[0] = jnp.zeros((cap, 128), F32)

    aff_ref[...] = aff
    rows = min(cap, 64)
    lane_e = lax.broadcasted_iota(jnp.int32, (rows, 128), 1)
    tok = lax.broadcasted_iota(jnp.int32, (1, 128), 1).astype(F32)

    for e in range(E):
        def per_rows(c, _, e=e):
            r0 = pl.multiple_of(c * rows, rows)
            s_col = (lax.broadcasted_iota(jnp.int32, (rows, 1), 0) + r0).astype(F32)
            acc_i = jnp.zeros((rows, 128), F32)
            acc_g = jnp.zeros((rows, 128), F32)
            for j in range(n // 128):
                hit = slot_ref[e:e + 1, j * 128:(j + 1) * 128] == s_col
                acc_i = acc_i + jnp.where(hit, tok + float(j * 128), 0.0)
                acc_g = acc_g + jnp.where(hit, aff_ref[e:e + 1, j * 128:(j + 1) * 128], 0.0)
            icol = jnp.sum(acc_i, 1, keepdims=True)
            gcol = jnp.sum(acc_g, 1, keepdims=True)
            idx_ref[0, pl.ds(r0, rows), :] = jnp.where(lane_e == e, icol, idx_ref[0, pl.ds(r0, rows), :])
            gate_ref[0, pl.ds(r0, rows), :] = jnp.where(lane_e == e, gcol, gate_ref[0, pl.ds(r0, rows), :])
            return 0

        lax.fori_loop(0, cap // rows, per_rows, 0)


def route(logits_t, off, n):
    B = logits_t.shape[0]
    cap = EC_CAPACITY_FACTOR * n // N_EXPERTS
    out = pl.BlockSpec((1, cap, 128), lambda b: (b, 0, 0))
    return pl.pallas_call(
        functools.partial(_route_kernel, n=n, cap=cap),
        grid=(B,),
        in_specs=[pl.BlockSpec((1, N_EXPERTS, n), lambda b: (b, 0, off // n))],
        out_specs=[out, out],
        out_shape=[jax.ShapeDtypeStruct((B, cap, 128), F32)] * 2,
        scratch_shapes=[pltpu.VMEM((N_EXPERTS, n), F32), pltpu.VMEM((N_EXPERTS, n), F32)],
        compiler_params=_cp(("parallel",)),
        name="route",
    )(logits_t)


GATHER_UNROLL = 8


def _gather_kernel(rows_ref, next_ref, h_hbm, o_ref, land0_ref, land1_ref, sems, *, T, tt):
    e = pl.program_id(0)
    lands = (land0_ref, land1_ref)

    def tile_copy(slot, s, r):
        src = h_hbm.at[pl.ds(pl.multiple_of(r * tt, tt), tt), :]
        dst = lands[slot].at[pl.ds(pl.multiple_of(s * tt, tt), tt), :]
        return pltpu.make_async_copy(src, dst, sems.at[slot])

    def issue(slot, ids_ref):
        def body(g, _):
            for i in range(GATHER_UNROLL):
                s = g * GATHER_UNROLL + i
                tile_copy(slot, s, ids_ref[0, 0, s]).start()
            return 0

        lax.fori_loop(0, T // GATHER_UNROLL, body, 0)

    def drain(slot):
        def body(g, _):
            for i in range(GATHER_UNROLL):
                tile_copy(slot, g * GATHER_UNROLL + i, 0).wait()
            return 0

        lax.fori_loop(0, T // GATHER_UNROLL, body, 0)

    @pl.when(e == 0)
    def _():
        issue(0, rows_ref)

    for slot in range(2):
        @pl.when(e % 2 == slot)
        def _(slot=slot):
            @pl.when(e + 1 < pl.num_programs(0))
            def _():
                issue(1 - slot, next_ref)

            drain(slot)
            o_ref[0] = _load_token_tiles(lands[slot], T, tt * LANES, BF16)


def gather_rows(rows, h_tiles, D):
    E, _, T = rows.shape
    tt = D // LANES
    ids = lambda off: pl.BlockSpec((1, 1, T), lambda e: (jnp.minimum(e + off, E - 1), 0, 0),
                                   memory_space=pltpu.SMEM)
    return pl.pallas_call(
        functools.partial(_gather_kernel, T=T, tt=tt),
        grid=(E,),
        in_specs=[ids(0), ids(1), pl.BlockSpec(memory_space=pl.ANY)],
        out_specs=pl.BlockSpec((1, T, D), lambda e: (e, 0, 0)),
        out_shape=jax.ShapeDtypeStruct((E, T, D), BF16),
        scratch_shapes=[pltpu.VMEM((T * tt, LANES), F32), pltpu.VMEM((T * tt, LANES), F32),
                        pltpu.SemaphoreType.DMA((2,))],
        compiler_params=_cp(("arbitrary",)),
        name="gather_rows",
    )(rows, rows, h_tiles)


def _expert_kernel(x_ref, gate_ref, wg_ref, wu_ref, wd_ref, o_ref):
    last = pl.num_programs(1) - 1
    x = x_ref[0]
    a = jnp.dot(x, wg_ref[0, 0].astype(BF16), preferred_element_type=F32)
    u = jnp.dot(x, wu_ref[0, 0].astype(BF16), preferred_element_type=F32)
    hid = (a * _sigmoid(a) * u).astype(BF16)
    part = jnp.dot(hid, wd_ref[0, 0].astype(BF16), preferred_element_type=F32)

    @pl.when(pl.program_id(1) == 0)
    def _():
        o_ref[0] = part

    @pl.when(jnp.logical_and(pl.program_id(1) != 0, pl.program_id(1) != last))
    def _():
        o_ref[0] += part

    @pl.when(pl.program_id(1) == last)
    def _():
        o_ref[0] = (o_ref[0] + part) * gate_ref[0]


def expert_ffn(xs, gate, w_gate, w_up, w_down, layer, tf=256):
    E, T, D = xs.shape
    FF = w_gate.shape[-1]
    return pl.pallas_call(
        _expert_kernel,
        grid=(E, FF // tf),
        in_specs=[pl.BlockSpec((1, T, D), lambda e, f: (e, 0, 0)),
                  pl.BlockSpec((1, T, 1), lambda e, f: (e, 0, 0)),
                  pl.BlockSpec((1, 1, D, tf), lambda e, f: (layer, e, 0, f)),
                  pl.BlockSpec((1, 1, D, tf), lambda e, f: (layer, e, 0, f)),
                  pl.BlockSpec((1, 1, tf, D), lambda e, f: (layer, e, f, 0))],
        out_specs=pl.BlockSpec((1, T, D), lambda e, f: (e, 0, 0)),
        out_shape=jax.ShapeDtypeStruct((E, T, D), F32),
        compiler_params=_cp(("parallel", "arbitrary")),
        name="expert_ffn",
    )(xs, gate, w_gate, w_up, w_down)


COMBINE_UNROLL = 8


def _combine_kernel(*refs, caps, bases, dh):
    n_sets = len(caps)
    idx_refs, y_refs = refs[:n_sets], refs[n_sets:2 * n_sets]
    o_ref, yt_ref = refs[2 * n_sets], refs[2 * n_sets + 1]
    tt = dh // LANES

    @pl.when(pl.program_id(2) == 0)
    def _():
        o_ref[...] = jnp.zeros_like(o_ref)

    for idx_ref, y_ref, cap, base in zip(idx_refs, y_refs, caps, bases):
        for j in range(tt):
            yt_ref[pl.ds(j, cap, stride=tt), :] = y_ref[0, :, j * LANES:(j + 1) * LANES]

        def body(g, _, idx_ref=idx_ref, base=base):
            s0 = g * COMBINE_UNROLL
            toks = [idx_ref[0, 0, 0, s0 + i] + base for i in range(COMBINE_UNROLL)]
            sums = [o_ref[0, toks[i], 0] + yt_ref[pl.ds(pl.multiple_of((s0 + i) * tt, tt), tt), :]
                    for i in range(COMBINE_UNROLL)]
            for i in range(COMBINE_UNROLL):
                o_ref[0, toks[i], 0] = sums[i]
            return 0

        lax.fori_loop(0, cap // COMBINE_UNROLL, body, 0)


def combine(ys, idx_sets, bases, dh=SUBLANES * LANES):
    E, T, D = ys.shape
    B = idx_sets[0].shape[0]
    caps = tuple(int(i.shape[-1]) for i in idx_sets)
    starts = np.concatenate([[0], np.cumsum([B * c for c in caps])[:-1]])
    idx_specs = [pl.BlockSpec((1, 1, 1, c), lambda b, hf, e: (b, e, 0, 0), memory_space=pltpu.SMEM) for c in caps]
    y_specs = [pl.BlockSpec((1, c, dh), lambda b, hf, e, blk0=int(st) // c: (e, blk0 + b, hf))
               for c, st in zip(caps, starts)]
    return pl.pallas_call(
        functools.partial(_combine_kernel, caps=caps, bases=tuple(bases), dh=dh),
        grid=(B, D // dh, E),
        in_specs=idx_specs + y_specs,
        out_specs=pl.BlockSpec((1, S_ALL, 1, SUBLANES, LANES), lambda b, hf, e: (b, 0, hf, 0, 0)),
        out_shape=jax.ShapeDtypeStruct((B, S_ALL, D // dh, SUBLANES, LANES), F32),
        scratch_shapes=[pltpu.VMEM((max(caps) * dh // LANES, LANES), F32)],
        compiler_params=_cp(("parallel", "parallel", "arbitrary")),
        name="combine",
    )(*[i[:, :, None, :] for i in idx_sets], *([ys] * len(caps)))


def _post_kernel(x_ref, y_ref, g2_ref, lg_ref, lb_ref, sh_ref, sc_ref, x2_ref, h_ref):
    y = _load_token_tiles(y_ref, ROW_TILE, x_ref.shape[1])
    x2 = _standardize(DEEPNORM_ALPHA * x_ref[...] + g2_ref[0] * y) * lg_ref[...] + lb_ref[...]
    x2_ref[...] = x2
    h_ref[...] = (_standardize(x2) * (1.0 + sc_ref[0]) + sh_ref[0]).astype(BF16)


def post_moe_norm(x1, y_tiles, modblk, ln_g, ln_b, modblk_next):
    R, D = x1.shape
    row = pl.BlockSpec((ROW_TILE, D), lambda i: (i, 0))
    mod = lambda part: pl.BlockSpec((1, 1, D), lambda i: (i, 0, part))
    vec = pl.BlockSpec((1, D), lambda i: (0, 0))
    y = pl.BlockSpec((ROW_TILE * D // LANES, LANES), lambda i: (i, 0))
    return pl.pallas_call(
        _post_kernel,
        grid=(R // ROW_TILE,),
        in_specs=[row, y, mod(5), vec, vec, mod(0), mod(1)],
        out_specs=[row, row],
        out_shape=[jax.ShapeDtypeStruct((R, D), F32), jax.ShapeDtypeStruct((R, D), BF16)],
        compiler_params=_cp(("parallel",)),
        name="post_moe_norm",
    )(x1, y_tiles, modblk, ln_g.reshape(1, -1), ln_b.reshape(1, -1), modblk_next, modblk_next)


def _final_kernel(x_ref, y_ref, g2_ref, lg_ref, lb_ref, x2_ref):
    y = _load_token_tiles(y_ref, ROW_TILE, x_ref.shape[1])
    x2_ref[...] = _standardize(DEEPNORM_ALPHA * x_ref[...] + g2_ref[0] * y) * lg_ref[...] + lb_ref[...]


def final_norm(x1, y_tiles, modblk, ln_g, ln_b):
    R, D = x1.shape
    n_lat = R // S_ALL * LAT_BLOCKS
    blk = lambda i: i + i // LAT_BLOCKS
    vec = pl.BlockSpec((1, D), lambda i: (0, 0))
    return pl.pallas_call(
        _final_kernel,
        grid=(n_lat,),
        in_specs=[pl.BlockSpec((ROW_TILE, D), lambda i: (blk(i), 0)),
                  pl.BlockSpec((ROW_TILE * D // LANES, LANES), lambda i: (blk(i), 0)),
                  pl.BlockSpec((1, 1, D), lambda i: (blk(i), 0, 5)), vec, vec],
        out_specs=pl.BlockSpec((ROW_TILE, D), lambda i: (i, 0)),
        out_shape=jax.ShapeDtypeStruct((n_lat * ROW_TILE, D), F32),
        compiler_params=_cp(("parallel",)),
        name="final_norm",
    )(x1, y_tiles, modblk, ln_g.reshape(1, -1), ln_b.reshape(1, -1))


def _s5_matrices(a_re, a_im, log_dt, b_re, b_im):
    dt = jnp.exp(log_dt)[:, None]
    mag = jnp.exp(a_re * dt)
    ab_re, ab_im = mag * jnp.cos(a_im * dt), mag * jnp.sin(a_im * dt)
    den = a_re * a_re + a_im * a_im
    num_re, num_im = ab_re - 1.0, ab_im
    coef_re = (num_re * a_re + num_im * a_im) / den
    coef_im = (num_im * a_re - num_re * a_im) / den
    bb_re = coef_re[..., None] * b_re - coef_im[..., None] * b_im
    bb_im = coef_re[..., None] * b_im + coef_im[..., None] * b_re
    gs = S5_GROUPS // S5_SLABS
    eye = jnp.eye(gs, dtype=F32)

    def slabs(t):
        t = t.reshape(S5_SLABS, gs, S5_STATE, S5_GROUP_CH)
        return jnp.einsum('sgpi,gh->sgihp', t, eye).reshape(S5_SLABS, gs * S5_GROUP_CH, gs * S5_STATE)

    lam = jnp.stack([ab_re.reshape(-1), ab_im.reshape(-1)], 0)
    return lam, jnp.concatenate([slabs(bb_re), slabs(bb_im)], -1)


def _s5_readout_matrix(c_re, c_im):
    gs = S5_GROUPS // S5_SLABS
    eye = jnp.eye(gs, dtype=F32)

    def slabs(t):
        t = t.reshape(S5_SLABS, gs, S5_GROUP_CH, S5_STATE)
        return jnp.einsum('sgip,gh->sgphi', t, eye).reshape(S5_SLABS, gs * S5_STATE, gs * S5_GROUP_CH)

    return jnp.concatenate([slabs(c_re), -slabs(c_im)], 1)


def _mla_weights(w_uq, w_ukv):
    qk = MLA_NOPE_DIM + MLA_ROPE_DIM
    wq = w_uq.reshape(MLA_Q_LORA, MLA_HEADS, qk)
    wq = jnp.pad(wq, ((0, 0), (0, 0), (0, MLA_QK_PAD - qk))).reshape(MLA_Q_LORA, MLA_HEADS * MLA_QK_PAD)
    wkv = w_ukv.reshape(MLA_KV_LORA, MLA_HEADS, MLA_NOPE_DIM + MLA_V_DIM)
    wk = jnp.pad(wkv[..., :MLA_NOPE_DIM], ((0, 0), (0, 0), (0, MLA_QK_PAD - MLA_NOPE_DIM)))
    wv = wkv[..., MLA_NOPE_DIM:]
    wkv_pad = jnp.concatenate([wk.reshape(MLA_KV_LORA, -1), wv.reshape(MLA_KV_LORA, -1)], 1)
    return wq.astype(BF16), wkv_pad.astype(BF16)


def kernel(x, c, ctx, c_ctx, ada_w, ada_b, w_in, s5_a_re_f, s5_a_im_f, s5_log_dt_f, s5_a_re_b, s5_a_im_b,
           s5_log_dt_b, s5_b_re, s5_b_im, s5_c_re, s5_c_im, s5_d, s5_w_glu, gqa_q_norm, gqa_k_norm,
           ret_decay_f, ret_decay_b, ret_norm, mla_q_norm, mla_kv_norm, mla_w_uq, mla_w_ukv,
           w_branch, w_gate, b_gate, w_out, ln1_g, ln1_b, router_w, moe_w_gate, moe_w_up, moe_w_down,
           ln2_g, ln2_b):
    B, N, D = x.shape
    R = B * S_ALL
    assert (B, N, D) == (BATCH, SEQ, D_MODEL) and RET_CHUNK == CTX_LEN == ROW_TILE

    cc = jnp.zeros((16, D), F32).at[:B].set(c).at[B].set(c_ctx)
    mod = ada_modulation(cc, ada_w, ada_b)
    sel = np.concatenate([np.r_[np.full(LAT_BLOCKS, b), B] for b in range(B)]).astype(np.int32)
    modblks = [mod[l][sel].reshape(R // ROW_TILE, 1, 6 * D) for l in range(DEPTH)]

    X = jnp.concatenate([x, ctx], 1).reshape(R, D)
    h = modulate_rows(X, modblks[0], 0, 1)

    for l in range(DEPTH):
        need_ctx = l < DEPTH - 1
        modblk = modblks[l]
        w_in_p = jnp.pad(w_in[l], ((0, 0), (0, IN_PAD - IN_TOTAL))).astype(BF16)
        wq_pad, wkv_pad = _mla_weights(mla_w_uq[l], mla_w_ukv[l])
        u, gq, gk, gv, rq, rk, rv, rg, mq, mk, mv = in_proj_branches(
            h, w_in_p, gqa_q_norm[l], gqa_k_norm[l], mla_q_norm[l], mla_kv_norm[l], wq_pad, wkv_pad)
        to3 = lambda t: t.reshape(B, S_ALL, t.shape[-1])

        lam_f, bb_f = _s5_matrices(s5_a_re_f[l], s5_a_im_f[l], s5_log_dt_f[l], s5_b_re[l], s5_b_im[l])
        lam_b, bb_b = _s5_matrices(s5_a_re_b[l], s5_a_im_b[l], s5_log_dt_b[l], s5_b_re[l], s5_b_im[l])
        lam = jnp.broadcast_to(jnp.stack([lam_f, lam_b], 0)[:, :, None, :], (2, 2, S5_SEG, S5_MODES))
        bb = jnp.stack([bb_f, bb_b], 0).astype(BF16)
        cmat = _s5_readout_matrix(s5_c_re[l], s5_c_im[l]).astype(BF16)
        o_s5 = s5_output(s5_scan(u, bb, lam, cmat), u, s5_d[l], s5_w_glu[l].astype(BF16))

        gq3, gk3, gv3 = to3(gq), to3(gk), to3(gv)
        att = functools.partial(attention, kv_heads=GQA_KV_HEADS, groups=GQA_HEADS // GQA_KV_HEADS,
                                dk=GQA_HEAD_DIM, dv=GQA_HEAD_DIM)
        o_lat = att(gq3, gk3, gv3, q_rows=SEQ, q_off=0, kv_rows=S_ALL, kv_off=0, tq=256)
        no_ctx = jnp.zeros((B, CTX_LEN, BRANCH_W), BF16)
        o_ctx = (att(gq3, gk3, gv3, q_rows=CTX_LEN, q_off=SEQ, kv_rows=CTX_LEN, kv_off=SEQ, tq=256)
                 if need_ctx else no_ctx)
        o_gqa = jnp.concatenate([o_lat, o_ctx], 1).reshape(R, BRANCH_W)

        lgf = -jnp.exp(ret_decay_f[l])
        lgb = -jnp.exp(ret_decay_b[l])
        rkt = jnp.swapaxes(to3(rk), 1, 2)
        o_ret = retention(to3(rq), rkt, to3(rv), to3(rg), ret_norm[l], lgf, lgb, S_ALL).reshape(R, BRANCH_W)

        matt = functools.partial(attention, kv_heads=MLA_HEADS, groups=1, dk=MLA_QK_PAD, dv=MLA_V_DIM)
        m_lat = matt(to3(mq), to3(mk), to3(mv), q_rows=SEQ, q_off=0, kv_rows=S_ALL, kv_off=0, tq=512)
        m_ctx = (matt(to3(mq), to3(mk), to3(mv), q_rows=CTX_LEN, q_off=SEQ, kv_rows=CTX_LEN, kv_off=SEQ, tq=256)
                 if need_ctx else no_ctx)
        o_mla = jnp.concatenate([m_lat, m_ctx], 1).reshape(R, BRANCH_W)

        merged = merge_branches(h, (o_s5, o_gqa, o_ret, o_mla), w_gate, l, b_gate[l], w_branch[l].astype(BF16))
        router_pad = jnp.pad(router_w[l], ((0, 0), (0, 128 - N_EXPERTS)))
        x1, h2, logits = out_proj_norm(merged, X, w_out[l].astype(BF16), modblk, ln1_g[l], ln1_b[l], router_pad)

        logits_t = jnp.swapaxes(logits.reshape(B, S_ALL, 128)[:, :, :N_EXPERTS], 1, 2)
        sets = [(0, SEQ)] + ([(SEQ, CTX_LEN)] if need_ctx else [])
        idx_sets, row_parts, gate_parts = [], [], []
        sample_row0 = (jnp.arange(B, dtype=jnp.int32) * S_ALL)[:, None, None]
        for off, n in sets:
            idx_f, gate_f = route(logits_t, off, n)
            idx = jnp.swapaxes(idx_f[:, :, :N_EXPERTS], 1, 2).astype(jnp.int32)
            gate = jnp.swapaxes(gate_f[:, :, :N_EXPERTS], 1, 2)
            idx_sets.append(idx)
            row_parts.append(jnp.swapaxes(idx + sample_row0 + off, 0, 1).reshape(N_EXPERTS, -1))
            gate_parts.append(jnp.swapaxes(gate, 0, 1).reshape(N_EXPERTS, -1))
        rows = jnp.concatenate(row_parts, 1)[:, None, :]
        gates = jnp.concatenate(gate_parts, 1)[:, :, None]
        xs = gather_rows(rows, h2, D)
        ys = expert_ffn(xs, gates, moe_w_gate, moe_w_up, moe_w_down, l)
        moe = combine(ys, idx_sets, [off for off, _ in sets]).reshape(R * D // LANES, LANES)
        if l == DEPTH - 1:
            return final_norm(x1, moe, modblk, ln2_g[l], ln2_b[l]).reshape(B, SEQ, D)
        X, h = post_moe_norm(x1, moe, modblk, ln2_g[l], ln2_b[l], modblks[l + 1])
```

```python
import functools
import math

import numpy as np
import jax
import jax.numpy as jnp
from jax import lax
from jax.experimental import pallas as pl
from jax.experimental.pallas import tpu as pltpu

F32 = jnp.float32
BF16 = jnp.bfloat16

D_MODEL = 2048
BATCH = 2
SEQ = 4096
DEPTH = 2
GRID_W = 64
CTX_LEN = 256
S_ALL = SEQ + CTX_LEN
N_BRANCH = 4
BRANCH_W = D_MODEL // 4
S5_GROUP_CH = 16
S5_GROUPS = BRANCH_W // S5_GROUP_CH
S5_STATE = 64
S5_MODES = S5_GROUPS * S5_STATE
GQA_HEAD_DIM = 128
GQA_HEADS = 4
GQA_KV_HEADS = 2
RET_HEADS = 4
RET_V_DIM = 128
RET_QK_DIM = 64
MLA_HEADS = 4
MLA_Q_LORA = 512
MLA_KV_LORA = 256
MLA_NOPE_DIM = 128
MLA_ROPE_DIM = 64
MLA_V_DIM = 128
MLA_QK_PAD = 256
N_EXPERTS = 16
EXPERT_FF = D_MODEL // 2
EC_CAPACITY_FACTOR = 2
ROPE_BASE = 10000.0
NORM_EPS = 1e-6
LOG2E = math.log2(math.e)
DEEPNORM_ALPHA = (2 * DEPTH) ** 0.25
IN_WIDTHS = (512, 512, 256, 256, 256, 256, 512, 512, 512, 256, 64)
IN_TOTAL = sum(IN_WIDTHS)
IN_PAD = 4096

LANES = 128
SUBLANES = 8
ROW_TILE = 256
MM_ROW_TILE = 512
MERGE_ROW_TILE = 1088
ATTN_SUB = 256
BLOCKS_PER_SAMPLE = S_ALL // ROW_TILE
LAT_BLOCKS = SEQ // ROW_TILE
S5_SLABS = 4
S5_SEG = 8
S5_STEPS = ROW_TILE // S5_SEG
RET_CHUNK = 256
VMEM_LIMIT = 56 * 1024 * 1024


def _cp(sem, vmem=VMEM_LIMIT):
    return pltpu.CompilerParams(dimension_semantics=sem, vmem_limit_bytes=vmem)


def _standardize(x):
    xc = x - jnp.mean(x, -1, keepdims=True)
    return xc * lax.rsqrt(jnp.mean(xc * xc, -1, keepdims=True) + NORM_EPS)


def _sigmoid(x):
    return 1.0 / (1.0 + jnp.exp(-x))


def _store_token_tiles(ref, x, row0=0):
    rows, w = x.shape
    tt = w // LANES
    for j in range(tt):
        ref[pl.ds(row0 * tt + j, rows, stride=tt), :] = x[:, j * LANES:(j + 1) * LANES]


def _load_token_tiles(ref, rows, w, dtype=F32):
    tt = w // LANES
    return jnp.concatenate([ref[pl.ds(j, rows, stride=tt), :].astype(dtype) for j in range(tt)], -1)


def _ada_kernel(c_ref, w_ref, b_ref, o_ref):
    c = c_ref[...]
    cs = (c * _sigmoid(c)).astype(BF16)
    o_ref[0] = jnp.dot(cs, w_ref[0].astype(BF16), preferred_element_type=F32) + b_ref[0]


def ada_modulation(cc, ada_w, ada_b, tn=1024):
    L, D, N = ada_w.shape
    return pl.pallas_call(
        _ada_kernel,
        grid=(L, N // tn),
        in_specs=[pl.BlockSpec((16, D), lambda l, n: (0, 0)),
                  pl.BlockSpec((1, D, tn), lambda l, n: (l, 0, n)),
                  pl.BlockSpec((1, 1, tn), lambda l, n: (l, 0, n))],
        out_specs=pl.BlockSpec((1, 16, tn), lambda l, n: (l, 0, n)),
        out_shape=jax.ShapeDtypeStruct((L, 16, N), F32),
        compiler_params=_cp(("arbitrary", "arbitrary")),
        name="ada_modulation",
    )(cc, ada_w, ada_b.reshape(L, 1, N))


def _modulate_kernel(x_ref, sh_ref, sc_ref, o_ref):
    o_ref[...] = (_standardize(x_ref[...]) * (1.0 + sc_ref[0]) + sh_ref[0]).astype(o_ref.dtype)


def modulate_rows(x, modblk, shift_part, scale_part):
    R, D = x.shape
    return pl.pallas_call(
        _modulate_kernel,
        grid=(R // ROW_TILE,),
        in_specs=[pl.BlockSpec((ROW_TILE, D), lambda i: (i, 0)),
                  pl.BlockSpec((1, 1, D), lambda i: (i, 0, shift_part)),
                  pl.BlockSpec((1, 1, D), lambda i: (i, 0, scale_part))],
        out_specs=pl.BlockSpec((ROW_TILE, D), lambda i: (i, 0)),
        out_shape=jax.ShapeDtypeStruct((R, D), BF16),
        compiler_params=_cp(("parallel",)),
        name="modulate_rows",
    )(x, modblk, modblk)


def _rope_tables(head_dim, width):
    half = head_dim // 2
    n = half // 2
    inv = ROPE_BASE ** (-np.arange(n, dtype=np.float64) / n)
    t = np.arange(SEQ)
    pos = np.stack([t // GRID_W, t % GRID_W], 0).astype(np.float64)
    lane = np.arange(head_dim)
    which = lane // half
    m = lane % half
    ang = (pos[which, :].T.astype(np.float32) * inv[m % n].astype(np.float32)[None, :]).astype(np.float64)
    cos = np.cos(ang)
    sin = np.where(m < n, -np.sin(ang), np.sin(ang))
    cos = np.concatenate([cos, np.ones((CTX_LEN, head_dim))], 0)
    sin = np.concatenate([sin, np.zeros((CTX_LEN, head_dim))], 0)
    reps = width // head_dim
    return (np.tile(cos, (1, reps)).astype(np.float32), np.tile(sin, (1, reps)).astype(np.float32))


def _mla_q_tables():
    cos64, sin64 = _rope_tables(MLA_ROPE_DIM, MLA_ROPE_DIM)
    ones = np.ones((S_ALL, MLA_NOPE_DIM), np.float32)
    zeros = np.zeros((S_ALL, MLA_NOPE_DIM), np.float32)
    pad1 = np.ones((S_ALL, MLA_QK_PAD - MLA_NOPE_DIM - MLA_ROPE_DIM), np.float32)
    cos = np.concatenate([ones, cos64, pad1], 1)
    sin = np.concatenate([zeros, sin64, 0 * pad1], 1)
    return cos, sin


def _rope(x, cos, sin, quarter):
    w = x.shape[-1]
    lane = lax.broadcasted_iota(jnp.int32, x.shape, 1)
    first = (lane % (2 * quarter)) < quarter
    partner = jnp.where(first, pltpu.roll(x, w - quarter, 1), pltpu.roll(x, quarter, 1))
    return x * cos + partner * sin


def _rms_heads(x, gain, head_dim):
    outs = []
    for h in range(x.shape[-1] // head_dim):
        xh = x[:, h * head_dim:(h + 1) * head_dim]
        outs.append(xh * lax.rsqrt(jnp.mean(xh * xh, -1, keepdims=True) + NORM_EPS) * gain)
    return outs[0] if len(outs) == 1 else jnp.concatenate(outs, -1)


PROJ_TILE = 1024


def _tab(ref_a, ref_b):
    return jnp.concatenate([ref_a[...], ref_b[...]], 0)


def _proj_s5_gq_kernel(h_ref, w_ref, ca, cb, sa, sb, gqn_ref, u_ref, gq_ref):
    p = jnp.dot(h_ref[...], w_ref[0], preferred_element_type=F32)
    u_ref[...] = p[:, :512]
    c128, s128 = _tab(ca, cb), _tab(sa, sb)
    q = _rms_heads(p[:, 512:], gqn_ref[...], GQA_HEAD_DIM)
    q = _rope(q, jnp.concatenate([c128] * 4, -1), jnp.concatenate([s128] * 4, -1), GQA_HEAD_DIM // 4)
    gq_ref[...] = (q * (GQA_HEAD_DIM ** -0.5 * LOG2E)).astype(BF16)


def _proj_kv_ret_kernel(h_ref, w_ref, ca, cb, sa, sb, c6a, c6b, s6a, s6b, gkn_ref,
                        gk_ref, gv_ref, rq_ref, rk_ref):
    p = jnp.dot(h_ref[...], w_ref[0], preferred_element_type=F32)
    tile2 = lambda t: jnp.concatenate([t, t], -1)
    c128, s128, c64, s64 = _tab(ca, cb), _tab(sa, sb), _tab(c6a, c6b), _tab(s6a, s6b)
    k = _rms_heads(p[:, :256], gkn_ref[...], GQA_HEAD_DIM)
    gk_ref[...] = _rope(k, tile2(c128), tile2(s128), GQA_HEAD_DIM // 4).astype(BF16)
    gv_ref[...] = p[:, 256:512].astype(BF16)
    rq_ref[...] = _rope(p[:, 512:768], tile2(c64), tile2(s64), RET_QK_DIM // 4).astype(BF16)
    rk = _rope(p[:, 768:], tile2(c64), tile2(s64), RET_QK_DIM // 4)
    rk_ref[...] = (rk * (RET_QK_DIM ** -0.5)).astype(BF16)


def _proj_ret_vg_kernel(h_ref, w_ref, rv_ref, rg_ref):
    p = jnp.dot(h_ref[...], w_ref[0], preferred_element_type=F32)
    rv_ref[...] = p[:, :512].astype(BF16)
    rg_ref[...] = p[:, 512:]


def _proj_mla_kernel(h_ref, w_ref, c6a, c6b, s6a, s6b, cqa, cqb, sqa, sqb, mqn_ref, mkvn_ref, wq_ref, wkv_ref,
                     q_ref, k_ref, v_ref):
    p = jnp.dot(h_ref[...], w_ref[0], preferred_element_type=F32)
    tm = p.shape[0]
    cqn = _rms_heads(p[:, :512], mqn_ref[...], MLA_Q_LORA).astype(BF16)
    ckvn = _rms_heads(p[:, 512:768], mkvn_ref[...], MLA_KV_LORA).astype(BF16)
    lane = lax.broadcasted_iota(jnp.int32, (tm, 128), 1)
    kr = jnp.where(lane < MLA_ROPE_DIM, _rope(p[:, 768:896], _tab(c6a, c6b), _tab(s6a, s6b), MLA_ROPE_DIM // 4), 0.0)
    scale = (MLA_NOPE_DIM + MLA_ROPE_DIM) ** -0.5 * LOG2E
    q = jnp.dot(cqn, wq_ref[...], preferred_element_type=F32)
    kv = jnp.dot(ckvn, wkv_ref[...], preferred_element_type=F32)
    cos, sin = _tab(cqa, cqb), _tab(sqa, sqb)
    krp = jnp.concatenate([jnp.zeros((tm, MLA_NOPE_DIM), F32), kr], -1)
    for h in range(MLA_HEADS):
        sl = slice(h * MLA_QK_PAD, (h + 1) * MLA_QK_PAD)
        q_ref[:, sl] = (_rope(q[:, sl], cos, sin, MLA_ROPE_DIM // 4) * scale).astype(BF16)
        k_ref[:, sl] = (kv[:, sl] + krp).astype(BF16)
    v_ref[...] = kv[:, MLA_HEADS * MLA_QK_PAD:].astype(BF16)


def in_proj_branches(h, w_in_p, layer, gqa_q_norm, gqa_k_norm, mla_q_norm, mla_kv_norm, wq_pad, wkv_pad,
                     tm=MM_ROW_TILE):
    R, D = h.shape
    half = tm // 2
    assert half == ROW_TILE
    c128, s128 = (jnp.asarray(t) for t in _rope_tables(GQA_HEAD_DIM, 128))
    c64, s64 = (jnp.asarray(t) for t in _rope_tables(RET_QK_DIM, 128))
    cq, sq = (jnp.asarray(t) for t in _mla_q_tables())
    x_spec = pl.BlockSpec((tm, D), lambda i: (i, 0))
    w_spec = lambda n: pl.BlockSpec((1, D, PROJ_TILE), lambda i: (layer, 0, n))
    row = lambda w: pl.BlockSpec((tm, w), lambda i: (i, 0))
    vec = lambda w: pl.BlockSpec((1, w), lambda i: (0, 0))
    full = lambda a: pl.BlockSpec(a.shape, lambda i: (0, 0))

    def tabs(t):
        w = t.shape[1]
        return ([pl.BlockSpec((half, w), lambda i: ((2 * i) % BLOCKS_PER_SAMPLE, 0)),
                 pl.BlockSpec((half, w), lambda i: ((2 * i + 1) % BLOCKS_PER_SAMPLE, 0))], [t, t])

    def call(kernel, n, extra_specs, extra_args, outs, name):
        return pl.pallas_call(
            kernel, grid=(R // tm,),
            in_specs=[x_spec, w_spec(n)] + extra_specs,
            out_specs=[row(w) for w, _ in outs],
            out_shape=[jax.ShapeDtypeStruct((R, w), dt) for w, dt in outs],
            compiler_params=_cp(("parallel",)), name=name,
        )(h, w_in_p, *extra_args)

    def gather_tabs(*ts):
        specs, args = [], []
        for t in ts:
            s, a = tabs(t)
            specs += s
            args += a
        return specs, args

    s0, a0 = gather_tabs(c128, s128)
    u, gq = call(_proj_s5_gq_kernel, 0, s0 + [vec(128)], a0 + [gqa_q_norm.reshape(1, -1)],
                 [(512, F32), (512, BF16)], "proj_s5_gq")
    s1, a1 = gather_tabs(c128, s128, c64, s64)
    gk, gv, rq, rk = call(_proj_kv_ret_kernel, 1, s1 + [vec(128)], a1 + [gqa_k_norm.reshape(1, -1)],
                          [(256, BF16)] * 4, "proj_kv_ret")
    rv, rg = call(_proj_ret_vg_kernel, 2, [], [], [(512, BF16), (512, F32)], "proj_ret_vg")
    s3, a3 = gather_tabs(c64, s64, cq, sq)
    hq = MLA_HEADS * MLA_QK_PAD
    mq, mk, mv = call(_proj_mla_kernel, 3, s3 + [vec(512), vec(256), full(wq_pad), full(wkv_pad)],
                      a3 + [mla_q_norm.reshape(1, -1), mla_kv_norm.reshape(1, -1), wq_pad, wkv_pad],
                      [(hq, BF16), (hq, BF16), (MLA_HEADS * MLA_V_DIM, BF16)], "proj_mla")
    return u, gq, gk, gv, rq, rk, rv, rg, mq, mk, mv


def _attn_kernel(q_ref, k_ref, v_ref, o_ref, *, groups, dk, dv):
    k = k_ref[0]
    v = v_ref[0]
    v1 = jnp.concatenate([v, jnp.ones_like(v)], -1)
    tq = q_ref.shape[1]
    q = jnp.concatenate([q_ref[0, :, g * dk:(g + 1) * dk] for g in range(groups)], 0)
    s = lax.dot_general(q, k, (((1,), (1,)), ((), ())), preferred_element_type=F32)
    for g in range(groups):
        for half in range(tq // ATTN_SUB):
            r0 = g * tq + half * ATTN_SUB
            sh = s[r0:r0 + ATTN_SUB]
            p = jnp.exp2(sh - jnp.max(sh, -1, keepdims=True)).astype(BF16)
            o = jnp.dot(p, v1, preferred_element_type=F32)
            rows = slice(half * ATTN_SUB, (half + 1) * ATTN_SUB)
            o_ref[0, rows, g * dv:(g + 1) * dv] = (o[:, :dv] / o[:, dv:]).astype(o_ref.dtype)


def attention(q, k, v, *, kv_heads, groups, dk, dv, q_rows, q_off, kv_rows, kv_off, tq):
    B = q.shape[0]
    assert tq % ATTN_SUB == 0 and q_off % tq == 0 and q_rows % tq == 0
    qb0, kb0 = q_off // tq, kv_off // kv_rows
    return pl.pallas_call(
        functools.partial(_attn_kernel, groups=groups, dk=dk, dv=dv),
        grid=(B, kv_heads, q_rows // tq),
        in_specs=[pl.BlockSpec((1, tq, groups * dk), lambda b, h, i: (b, qb0 + i, h)),
                  pl.BlockSpec((1, kv_rows, dk), lambda b, h, i: (b, kb0, h)),
                  pl.BlockSpec((1, kv_rows, dv), lambda b, h, i: (b, kb0, h))],
        out_specs=pl.BlockSpec((1, tq, groups * dv), lambda b, h, i: (b, i, h)),
        out_shape=jax.ShapeDtypeStruct((B, q_rows, kv_heads * groups * dv), BF16),
        compiler_params=_cp(("parallel", "parallel", "arbitrary")),
        name="attention",
    )(q, k, v)


def _ret_kernel(lgf_ref, lgb_ref, q_ref, kt_ref, v_ref, g_ref, gain_ref, o_ref, sb_ref, *, n_out):
    L = RET_CHUNK
    nc = SEQ // L
    pair = pl.program_id(1)
    r_i = lax.broadcasted_iota(jnp.int32, (L, L), 0)
    c_i = lax.broadcasted_iota(jnp.int32, (L, L), 1)
    diff = (r_i - c_i).astype(F32)
    pos_col = lax.broadcasted_iota(jnp.int32, (L, 1), 0).astype(F32)
    pos_row = lax.broadcasted_iota(jnp.int32, (1, L), 1).astype(F32)
    for j in range(2):
        lgf = lgf_ref[pair * 2 + j]
        lgb = lgb_ref[pair * 2 + j]
        dmat = jnp.where(diff >= 0, jnp.exp(lgf * jnp.maximum(diff, 0.0)), jnp.exp(lgb * jnp.maximum(-diff, 0.0)))
        dq_f = jnp.exp(lgf * (pos_col + 1.0))
        dq_b = jnp.exp(lgb * (L - pos_col))
        dk_f = jnp.exp(lgf * (L - 1.0 - pos_row))
        dk_b = jnp.exp(lgb * pos_row)
        dc_f = jnp.exp(lgf * L)
        dc_b = jnp.exp(lgb * L)
        qs = slice(j * RET_QK_DIM, (j + 1) * RET_QK_DIM)
        vs = slice(j * RET_V_DIM, (j + 1) * RET_V_DIM)

        def chunk(c):
            rows = slice(c * L, (c + 1) * L)
            return q_ref[0, rows, qs], kt_ref[0, qs, rows], v_ref[0, rows, vs]

        def readout(o, c, out_row0):
            oc = o - jnp.mean(o, -1, keepdims=True)
            on = oc * lax.rsqrt(jnp.mean(oc * oc, -1, keepdims=True) + NORM_EPS)
            g = g_ref[0, c * L:(c + 1) * L, vs]
            o_ref[0, out_row0:out_row0 + L, vs] = (on * gain_ref[:, vs] * (g * _sigmoid(g))).astype(o_ref.dtype)

        def intra(q, kt, v):
            sc = jnp.dot(q, kt, preferred_element_type=F32) * dmat
            return jnp.dot(sc.astype(BF16), v, preferred_element_type=F32)

        def state_add(kt, dk, v):
            return jnp.dot((kt.astype(F32) * dk).astype(BF16), v, preferred_element_type=F32)

        qc, ktc, vc = chunk(nc)
        if n_out > SEQ:
            readout(intra(qc, ktc, vc), nc, SEQ)
        s_f = state_add(ktc, dk_f, vc)
        s_b = state_add(ktc, dk_b, vc)
        for c in range(nc - 1, -1, -1):
            sb_ref[c] = s_b
            _, kt, v = chunk(c)
            s_b = s_b * dc_b + state_add(kt, dk_b, v)
        for c in range(nc):
            q, kt, v = chunk(c)
            qf = q.astype(F32)
            o = (intra(q, kt, v)
                 + jnp.dot((qf * dq_f).astype(BF16), s_f.astype(BF16), preferred_element_type=F32)
                 + jnp.dot((qf * dq_b).astype(BF16), sb_ref[c].astype(BF16), preferred_element_type=F32))
            readout(o, c, c * L)
            s_f = s_f * dc_f + state_add(kt, dk_f, v)


def retention(rq, rkt, rv, rg, gain, lgf, lgb, n_out):
    B = rq.shape[0]
    smem = pl.BlockSpec(memory_space=pltpu.SMEM)
    return pl.pallas_call(
        functools.partial(_ret_kernel, n_out=n_out),
        grid=(B, RET_HEADS // 2),
        in_specs=[smem, smem,
                  pl.BlockSpec((1, S_ALL, 128), lambda b, p: (b, 0, p)),
                  pl.BlockSpec((1, 128, S_ALL), lambda b, p: (b, p, 0)),
                  pl.BlockSpec((1, S_ALL, 256), lambda b, p: (b, 0, p)),
                  pl.BlockSpec((1, S_ALL, 256), lambda b, p: (b, 0, p)),
                  pl.BlockSpec((1, 256), lambda b, p: (0, p))],
        out_specs=pl.BlockSpec((1, n_out, 256), lambda b, p: (b, 0, p)),
        out_shape=jax.ShapeDtypeStruct((B, n_out, RET_HEADS * RET_V_DIM), BF16),
        scratch_shapes=[pltpu.VMEM((SEQ // RET_CHUNK, RET_QK_DIM, RET_V_DIM), F32)],
        compiler_params=_cp(("parallel", "parallel")),
        name="retention",
    )(lgf, lgb, rq, rkt, rv, rg, gain.reshape(1, -1))


def _s5_permutations():
    r = np.arange(ROW_TILE)
    t = (r % S5_SEG) * S5_STEPS + r // S5_SEG
    out = np.zeros((2, 2, ROW_TILE, ROW_TILE), np.float32)
    for d, td in enumerate((t, ROW_TILE - 1 - t)):
        out[d, 0, r, td] = 1.0
        out[d, 1, td, r] = 1.0
    return out


def _s5_kernel(u_ref, perm_ref, bb_ref, lam_ref, cm_ref, y_ref, bu_ref, pw_ref, st_ref):
    M = S5_MODES
    chunk = pl.program_id(2)
    lam_re = lam_ref[0, 0]
    lam_im = lam_ref[0, 1]

    @pl.when(chunk == 0)
    def _():
        st_ref[...] = jnp.zeros_like(st_ref)
        p_re, p_im = lam_re, lam_im
        for j in range(S5_STEPS):
            pw_ref[0, j] = p_re
            pw_ref[1, j] = p_im
            p_re, p_im = p_re * lam_re - p_im * lam_im, p_re * lam_im + p_im * lam_re

    to_scan = perm_ref[0, 0]
    to_time = perm_ref[0, 1]
    u = jnp.dot(to_scan, u_ref[...].astype(BF16), preferred_element_type=F32).astype(BF16)

    slab = BRANCH_W // S5_SLABS
    ms = M // S5_SLABS
    for s in range(S5_SLABS):
        part = jnp.dot(u[:, s * slab:(s + 1) * slab], bb_ref[0, s], preferred_element_type=F32)
        bu_ref[:, s * ms:(s + 1) * ms] = part[:, :ms]
        bu_ref[:, M + s * ms:M + (s + 1) * ms] = part[:, ms:]

    tile = 512
    for t in range(M // tile):
        re_sl = slice(t * tile, (t + 1) * tile)
        im_sl = slice(M + t * tile, M + (t + 1) * tile)
        lr, li = lam_re[:, re_sl], lam_im[:, re_sl]

        def step(j, carry):
            s_re, s_im = carry
            rows = pl.ds(pl.multiple_of(j * S5_SEG, S5_SEG), S5_SEG)
            n_re = lr * s_re - li * s_im + bu_ref[rows, re_sl]
            n_im = lr * s_im + li * s_re + bu_ref[rows, im_sl]
            bu_ref[rows, re_sl] = n_re
            bu_ref[rows, im_sl] = n_im
            return n_re, n_im

        z = jnp.zeros((S5_SEG, tile), F32)
        lax.fori_loop(0, S5_STEPS, step, (z, z))

    last = slice((S5_STEPS - 1) * S5_SEG, S5_STEPS * S5_SEG)
    e_re, e_im = bu_ref[last, 0:M], bu_ref[last, M:2 * M]
    pl_re, pl_im = pw_ref[0, S5_STEPS - 1][0:1], pw_ref[1, S5_STEPS - 1][0:1]
    c_re, c_im = st_ref[0:1, :], st_ref[1:2, :]
    rows_re, rows_im = [], []
    for k in range(S5_SEG):
        rows_re.append(c_re)
        rows_im.append(c_im)
        c_re, c_im = (e_re[k:k + 1] + pl_re * c_re - pl_im * c_im,
                      e_im[k:k + 1] + pl_re * c_im + pl_im * c_re)
    st_ref[0:1, :] = c_re
    st_ref[1:2, :] = c_im
    car_re = jnp.concatenate(rows_re, 0)
    car_im = jnp.concatenate(rows_im, 0)

    def fix(j, _):
        rows = pl.ds(pl.multiple_of(j * S5_SEG, S5_SEG), S5_SEG)
        p_re, p_im = pw_ref[0, j], pw_ref[1, j]
        bu_ref[rows, 0:M] = bu_ref[rows, 0:M] + p_re * car_re - p_im * car_im
        bu_ref[rows, M:2 * M] = bu_ref[rows, M:2 * M] + p_re * car_im + p_im * car_re
        return 0

    lax.fori_loop(0, S5_STEPS, fix, 0)

    ys = []
    for s in range(S5_SLABS):
        hs = jnp.concatenate([bu_ref[:, s * ms:(s + 1) * ms], bu_ref[:, M + s * ms:M + (s + 1) * ms]], -1)
        ys.append(jnp.dot(hs.astype(BF16), cm_ref[s], preferred_element_type=F32))
    y = jnp.concatenate(ys, -1)
    out = None
    for _ in range(2):
        piece = y.astype(BF16)
        y = y - piece.astype(F32)
        term = jnp.dot(to_time, piece, preferred_element_type=F32)
        out = term if out is None else out + term
    y_ref[0] = out


def _s5_block(d, c):
    fwd = (c + LAT_BLOCKS) % BLOCKS_PER_SAMPLE
    bwd = jnp.where(c == 0, LAT_BLOCKS, LAT_BLOCKS - c)
    return jnp.where(d == 0, fwd, bwd)


def s5_scan(u, bb, lam, cmat):
    R = u.shape[0]
    B = R // S_ALL
    W = BRANCH_W
    row_block = lambda d, b, c: b * BLOCKS_PER_SAMPLE + _s5_block(d, c)
    return pl.pallas_call(
        _s5_kernel,
        grid=(2, B, BLOCKS_PER_SAMPLE),
        in_specs=[pl.BlockSpec((ROW_TILE, W), lambda d, b, c: (row_block(d, b, c), 0)),
                  pl.BlockSpec((1, 2, ROW_TILE, ROW_TILE), lambda d, b, c: (d, 0, 0, 0)),
                  pl.BlockSpec((1,) + bb.shape[1:], lambda d, b, c: (d, 0, 0, 0)),
                  pl.BlockSpec((1, 2, S5_SEG, S5_MODES), lambda d, b, c: (d, 0, 0, 0)),
                  pl.BlockSpec(cmat.shape, lambda d, b, c: (0, 0, 0))],
        out_specs=pl.BlockSpec((1, ROW_TILE, W), lambda d, b, c: (d, row_block(d, b, c), 0)),
        out_shape=jax.ShapeDtypeStruct((2, R, W), F32),
        scratch_shapes=[pltpu.VMEM((ROW_TILE, 2 * S5_MODES), F32),
                        pltpu.VMEM((2, S5_STEPS, S5_SEG, S5_MODES), F32),
                        pltpu.VMEM((8, S5_MODES), F32)],
        compiler_params=_cp(("arbitrary", "arbitrary", "arbitrary")),
        name="s5_scan",
    )(u, jnp.asarray(_s5_permutations(), BF16), bb, lam, cmat)


def _s5_out_kernel(yf_ref, yb_ref, p_ref, d_ref, w_ref, o_ref):
    y = yf_ref[0] + yb_ref[0] + d_ref[...] * p_ref[...]
    z = 0.5 * y * (1.0 + jnp.tanh(math.sqrt(2.0 / math.pi) * (y + 0.044715 * (y * y * y))))
    gate = _sigmoid(jnp.dot(z.astype(BF16), w_ref[...], preferred_element_type=F32))
    o_ref[...] = (z * gate).astype(o_ref.dtype)


def s5_output(y_dirs, u, d, w_glu):
    R = y_dirs.shape[1]
    return pl.pallas_call(
        _s5_out_kernel,
        grid=(R // ROW_TILE,),
        in_specs=[pl.BlockSpec((1, ROW_TILE, 512), lambda i: (0, i, 0)),
                  pl.BlockSpec((1, ROW_TILE, 512), lambda i: (1, i, 0)),
                  pl.BlockSpec((ROW_TILE, 512), lambda i: (i, 0)),
                  pl.BlockSpec((1, 512), lambda i: (0, 0)),
                  pl.BlockSpec((512, 512), lambda i: (0, 0))],
        out_specs=pl.BlockSpec((ROW_TILE, 512), lambda i: (i, 0)),
        out_shape=jax.ShapeDtypeStruct((R, 512), BF16),
        compiler_params=_cp(("parallel",)),
        name="s5_output",
    )(y_dirs, y_dirs, u, d.reshape(1, -1), w_glu)


def _merge_kernel(h_ref, o0_ref, o1_ref, o2_ref, o3_ref, g0_ref, g1_ref, g2_ref, g3_ref,
                  bg_ref, wb_ref, out_ref, wg_ref):
    @pl.when(pl.program_id(1) == 0)
    def _():
        for k, g in enumerate((g0_ref, g1_ref, g2_ref, g3_ref)):
            wg_ref[k] = g[0].astype(BF16)

    h = h_ref[...]
    acc = None
    for k, o in enumerate((o0_ref, o1_ref, o2_ref, o3_ref)):
        gate = _sigmoid(jnp.dot(h, wg_ref[k], preferred_element_type=F32) + bg_ref[k])
        term = gate * jnp.dot(o[...], wb_ref[k], preferred_element_type=F32)
        acc = term if acc is None else acc + term
    out_ref[...] = acc.astype(out_ref.dtype)


def merge_branches(h, outs, w_gate_all, layer, b_gate, w_branch, tn=256, tm=MERGE_ROW_TILE):
    R, D = h.shape
    nb = D // tn
    gate_spec = lambda k: pl.BlockSpec((1, D, tn), lambda n, m: (layer, 0, k * nb + n))
    bg = b_gate.reshape(N_BRANCH, 1, D)
    return pl.pallas_call(
        _merge_kernel,
        grid=(nb, R // tm),
        in_specs=[pl.BlockSpec((tm, D), lambda n, m: (m, 0))]
                 + [pl.BlockSpec((tm, BRANCH_W), lambda n, m: (m, 0))] * N_BRANCH
                 + [gate_spec(k) for k in range(N_BRANCH)]
                 + [pl.BlockSpec((N_BRANCH, 1, tn), lambda n, m: (0, 0, n)),
                    pl.BlockSpec((N_BRANCH, BRANCH_W, tn), lambda n, m: (0, 0, n))],
        out_specs=pl.BlockSpec((tm, tn), lambda n, m: (m, n)),
        out_shape=jax.ShapeDtypeStruct((R, D), BF16),
        scratch_shapes=[pltpu.VMEM((N_BRANCH, D, tn), BF16)],
        compiler_params=_cp(("arbitrary", "arbitrary")),
        name="merge_branches",
    )(h, *outs, w_gate_all, w_gate_all, w_gate_all, w_gate_all, bg, w_branch)


def _out_kernel(m_ref, x_ref, w_ref, g1_ref, lg_ref, lb_ref, sh_ref, sc_ref, rw_ref,
                x1_ref, h2_ref, lo_ref):
    y = jnp.dot(m_ref[...], w_ref[...], preferred_element_type=F32)
    x1 = _standardize(DEEPNORM_ALPHA * x_ref[...] + g1_ref[0] * y) * lg_ref[...] + lb_ref[...]
    x1_ref[...] = x1
    h2 = _standardize(x1) * (1.0 + sc_ref[0]) + sh_ref[0]
    _store_token_tiles(h2_ref, h2)
    rw = rw_ref[...]
    h_hi, rw_hi = h2.astype(BF16), rw.astype(BF16)
    h_lo, rw_lo = (h2 - h_hi.astype(F32)).astype(BF16), (rw - rw_hi.astype(F32)).astype(BF16)
    lo_ref[...] = (jnp.dot(h_hi, rw_hi, preferred_element_type=F32)
                   + jnp.dot(h_lo, rw_hi, preferred_element_type=F32)
                   + jnp.dot(h_hi, rw_lo, preferred_element_type=F32))


def out_proj_norm(merged, x, w_out, modblk, ln_g, ln_b, router_pad):
    R, D = x.shape
    tt = D // LANES
    row = lambda w: pl.BlockSpec((ROW_TILE, w), lambda i: (i, 0))
    mod = lambda part: pl.BlockSpec((1, 1, D), lambda i: (i, 0, part))
    vec = pl.BlockSpec((1, D), lambda i: (0, 0))
    return pl.pallas_call(
        _out_kernel,
        grid=(R // ROW_TILE,),
        in_specs=[row(D), row(D), pl.BlockSpec((D, D), lambda i: (0, 0)), mod(2), vec, vec, mod(3), mod(4),
                  pl.BlockSpec((D, 128), lambda i: (0, 0))],
        out_specs=[row(D), pl.BlockSpec((ROW_TILE * tt, LANES), lambda i: (i, 0)), row(128)],
        out_shape=[jax.ShapeDtypeStruct((R, D), F32), jax.ShapeDtypeStruct((R * tt, LANES), F32),
                   jax.ShapeDtypeStruct((R, 128), F32)],
        compiler_params=_cp(("parallel",)),
        name="out_proj_norm",
    )(merged, x, w_out, modblk, ln_g.reshape(1, -1), ln_b.reshape(1, -1), modblk, modblk, router_pad)


ROUTE_GEOMETRIC_STEPS = 40
ROUTE_BISECT_STEPS = ROUTE_GEOMETRIC_STEPS + 8


def _route_kernel(lt_ref, idx_ref, gate_ref, slot_ref, aff_ref, *, n, cap):
    E = N_EXPERTS
    lt = lt_ref[0]
    ex = jnp.exp(lt - jnp.max(lt, 0, keepdims=True))
    aff = ex / jnp.sum(ex, 0, keepdims=True)

    def count_ge(v):
        return jnp.sum((aff >= v).astype(F32), 1, keepdims=True)

    tiny = jnp.full((E, 1), float(np.finfo(np.float32).tiny), F32)
    normal = count_ge(tiny) >= cap
    lo0 = jnp.where(normal, tiny, 0.0)
    hi0 = jnp.where(normal, 2.0, tiny)

    def bisect(i, lo_hi):
        lo, hi = lo_hi
        geo = jnp.clip(jnp.sqrt(lo) * jnp.sqrt(hi), lo, hi)
        mid = jnp.where(jnp.logical_and(i < ROUTE_GEOMETRIC_STEPS, lo > 0.0), geo, lo + 0.5 * (hi - lo))
        ok = count_ge(mid) >= cap
        return jnp.where(ok, mid, lo), jnp.where(ok, hi, mid)

    lo, _ = lax.fori_loop(0, ROUTE_BISECT_STEPS, bisect, (lo0, hi0))
    thr = jnp.min(jnp.where(aff >= lo, aff, 2.0), 1, keepdims=True)
    gt = aff > thr
    eq = aff == thr
    need = cap - jnp.sum(gt.astype(F32), 1, keepdims=True)

    blk = min(n, 512)
    upper = (lax.broadcasted_iota(jnp.int32, (blk, blk), 0)
             < lax.broadcasted_iota(jnp.int32, (blk, blk), 1)).astype(BF16)

    def excl_cumsum(mask):
        parts, carry = [], jnp.zeros((E, 1), F32)
        for j in range(n // blk):
            m = mask[:, j * blk:(j + 1) * blk].astype(F32)
            parts.append(jnp.dot(m.astype(BF16), upper, preferred_element_type=F32) + carry)
            carry = carry + jnp.sum(m, 1, keepdims=True)
        return parts[0] if len(parts) == 1 else jnp.concatenate(parts, 1)

    sel = gt | (eq & (excl_cumsum(eq) < need))
    slot_ref[...] = jnp.where(sel, excl_cumsum(sel), -1.0)
    idx_ref[0] = jnp.zeros((cap, 128), F32)
    gate_ref[0] = jnp.zeros((cap, 128), F32)

    aff_ref[...] = aff
    rows = min(cap, 64)
    lane_e = lax.broadcasted_iota(jnp.int32, (rows, 128), 1)
    tok = lax.broadcasted_iota(jnp.int32, (1, 128), 1).astype(F32)

    for e in range(E):
        def per_rows(c, _, e=e):
            r0 = pl.multiple_of(c * rows, rows)
            s_col = (lax.broadcasted_iota(jnp.int32, (rows, 1), 0) + r0).astype(F32)
            acc_i = jnp.zeros((rows, 128), F32)
            acc_g = jnp.zeros((rows, 128), F32)
            for j in range(n // 128):
                hit = slot_ref[e:e + 1, j * 128:(j + 1) * 128] == s_col
                acc_i = acc_i + jnp.where(hit, tok + float(j * 128), 0.0)
                acc_g = acc_g + jnp.where(hit, aff_ref[e:e + 1, j * 128:(j + 1) * 128], 0.0)
            icol = jnp.sum(acc_i, 1, keepdims=True)
            gcol = jnp.sum(acc_g, 1, keepdims=True)
            idx_ref[0, pl.ds(r0, rows), :] = jnp.where(lane_e == e, icol, idx_ref[0, pl.ds(r0, rows), :])
            gate_ref[0, pl.ds(r0, rows), :] = jnp.where(lane_e == e, gcol, gate_ref[0, pl.ds(r0, rows), :])
            return 0

        lax.fori_loop(0, cap // rows, per_rows, 0)


def route(logits_t, off, n):
    B = logits_t.shape[0]
    cap = EC_CAPACITY_FACTOR * n // N_EXPERTS
    out = pl.BlockSpec((1, cap, 128), lambda b: (b, 0, 0))
    return pl.pallas_call(
        functools.partial(_route_kernel, n=n, cap=cap),
        grid=(B,),
        in_specs=[pl.BlockSpec((1, N_EXPERTS, n), lambda b: (b, 0, off // n))],
        out_specs=[out, out],
        out_shape=[jax.ShapeDtypeStruct((B, cap, 128), F32)] * 2,
        scratch_shapes=[pltpu.VMEM((N_EXPERTS, n), F32), pltpu.VMEM((N_EXPERTS, n), F32)],
        compiler_params=_cp(("parallel",)),
        name="route",
    )(logits_t)


GATHER_UNROLL = 8


def _gather_kernel(rows_ref, next_ref, h_hbm, o_ref, land0_ref, land1_ref, sems, *, T, tt):
    e = pl.program_id(0)
    lands = (land0_ref, land1_ref)

    def tile_copy(slot, s, r):
        src = h_hbm.at[pl.ds(pl.multiple_of(r * tt, tt), tt), :]
        dst = lands[slot].at[pl.ds(pl.multiple_of(s * tt, tt), tt), :]
        return pltpu.make_async_copy(src, dst, sems.at[slot])

    def issue(slot, ids_ref):
        def body(g, _):
            for i in range(GATHER_UNROLL):
                s = g * GATHER_UNROLL + i
                tile_copy(slot, s, ids_ref[0, 0, s]).start()
            return 0

        lax.fori_loop(0, T // GATHER_UNROLL, body, 0)

    def drain(slot):
        def body(g, _):
            for i in range(GATHER_UNROLL):
                tile_copy(slot, g * GATHER_UNROLL + i, 0).wait()
            return 0

        lax.fori_loop(0, T // GATHER_UNROLL, body, 0)

    @pl.when(e == 0)
    def _():
        issue(0, rows_ref)

    for slot in range(2):
        @pl.when(e % 2 == slot)
        def _(slot=slot):
            @pl.when(e + 1 < pl.num_programs(0))
            def _():
                issue(1 - slot, next_ref)

            drain(slot)
            o_ref[0] = _load_token_tiles(lands[slot], T, tt * LANES, BF16)


def gather_rows(rows, h_tiles, D):
    E, _, T = rows.shape
    tt = D // LANES
    ids = lambda off: pl.BlockSpec((1, 1, T), lambda e: (jnp.minimum(e + off, E - 1), 0, 0),
                                   memory_space=pltpu.SMEM)
    return pl.pallas_call(
        functools.partial(_gather_kernel, T=T, tt=tt),
        grid=(E,),
        in_specs=[ids(0), ids(1), pl.BlockSpec(memory_space=pl.ANY)],
        out_specs=pl.BlockSpec((1, T, D), lambda e: (e, 0, 0)),
        out_shape=jax.ShapeDtypeStruct((E, T, D), BF16),
        scratch_shapes=[pltpu.VMEM((T * tt, LANES), F32), pltpu.VMEM((T * tt, LANES), F32),
                        pltpu.SemaphoreType.DMA((2,))],
        compiler_params=_cp(("arbitrary",)),
        name="gather_rows",
    )(rows, rows, h_tiles)


def _expert_kernel(x_ref, gate_ref, wg_ref, wu_ref, wd_ref, o_ref):
    last = pl.num_programs(1) - 1
    x = x_ref[0]
    a = jnp.dot(x, wg_ref[0, 0].astype(BF16), preferred_element_type=F32)
    u = jnp.dot(x, wu_ref[0, 0].astype(BF16), preferred_element_type=F32)
    hid = (a * _sigmoid(a) * u).astype(BF16)
    part = jnp.dot(hid, wd_ref[0, 0].astype(BF16), preferred_element_type=F32)

    @pl.when(pl.program_id(1) == 0)
    def _():
        o_ref[0] = part

    @pl.when(jnp.logical_and(pl.program_id(1) != 0, pl.program_id(1) != last))
    def _():
        o_ref[0] += part

    @pl.when(pl.program_id(1) == last)
    def _():
        o_ref[0] = (o_ref[0] + part) * gate_ref[0]


def expert_ffn(xs, gate, w_gate, w_up, w_down, layer, tf=256):
    E, T, D = xs.shape
    FF = w_gate.shape[-1]
    return pl.pallas_call(
        _expert_kernel,
        grid=(E, FF // tf),
        in_specs=[pl.BlockSpec((1, T, D), lambda e, f: (e, 0, 0)),
                  pl.BlockSpec((1, T, 1), lambda e, f: (e, 0, 0)),
                  pl.BlockSpec((1, 1, D, tf), lambda e, f: (layer, e, 0, f)),
                  pl.BlockSpec((1, 1, D, tf), lambda e, f: (layer, e, 0, f)),
                  pl.BlockSpec((1, 1, tf, D), lambda e, f: (layer, e, f, 0))],
        out_specs=pl.BlockSpec((1, T, D), lambda e, f: (e, 0, 0)),
        out_shape=jax.ShapeDtypeStruct((E, T, D), F32),
        compiler_params=_cp(("parallel", "arbitrary")),
        name="expert_ffn",
    )(xs, gate, w_gate, w_up, w_down)


COMBINE_UNROLL = 8


def _combine_kernel(*refs, caps, bases, dh):
    n_sets = len(caps)
    idx_refs, y_refs = refs[:n_sets], refs[n_sets:2 * n_sets]
    o_ref, yt_ref = refs[2 * n_sets], refs[2 * n_sets + 1]
    tt = dh // LANES

    @pl.when(pl.program_id(2) == 0)
    def _():
        o_ref[...] = jnp.zeros_like(o_ref)

    for idx_ref, y_ref, cap, base in zip(idx_refs, y_refs, caps, bases):
        for j in range(tt):
            yt_ref[pl.ds(j, cap, stride=tt), :] = y_ref[0, :, j * LANES:(j + 1) * LANES]

        def body(g, _, idx_ref=idx_ref, base=base):
            s0 = g * COMBINE_UNROLL
            toks = [idx_ref[0, 0, 0, s0 + i] + base for i in range(COMBINE_UNROLL)]
            sums = [o_ref[0, toks[i], 0] + yt_ref[pl.ds(pl.multiple_of((s0 + i) * tt, tt), tt), :]
                    for i in range(COMBINE_UNROLL)]
            for i in range(COMBINE_UNROLL):
                o_ref[0, toks[i], 0] = sums[i]
            return 0

        lax.fori_loop(0, cap // COMBINE_UNROLL, body, 0)


def combine(ys, idx_sets, bases, dh=SUBLANES * LANES):
    E, T, D = ys.shape
    B = idx_sets[0].shape[0]
    caps = tuple(int(i.shape[-1]) for i in idx_sets)
    starts = np.concatenate([[0], np.cumsum([B * c for c in caps])[:-1]])
    idx_specs = [pl.BlockSpec((1, 1, 1, c), lambda b, hf, e: (b, e, 0, 0), memory_space=pltpu.SMEM) for c in caps]
    y_specs = [pl.BlockSpec((1, c, dh), lambda b, hf, e, blk0=int(st) // c: (e, blk0 + b, hf))
               for c, st in zip(caps, starts)]
    return pl.pallas_call(
        functools.partial(_combine_kernel, caps=caps, bases=tuple(bases), dh=dh),
        grid=(B, D // dh, E),
        in_specs=idx_specs + y_specs,
        out_specs=pl.BlockSpec((1, S_ALL, 1, SUBLANES, LANES), lambda b, hf, e: (b, 0, hf, 0, 0)),
        out_shape=jax.ShapeDtypeStruct((B, S_ALL, D // dh, SUBLANES, LANES), F32),
        scratch_shapes=[pltpu.VMEM((max(caps) * dh // LANES, LANES), F32)],
        compiler_params=_cp(("parallel", "parallel", "arbitrary")),
        name="combine",
    )(*[i[:, :, None, :] for i in idx_sets], *([ys] * len(caps)))


def _post_kernel(x_ref, y_ref, g2_ref, lg_ref, lb_ref, sh_ref, sc_ref, x2_ref, h_ref):
    y = _load_token_tiles(y_ref, ROW_TILE, x_ref.shape[1])
    x2 = _standardize(DEEPNORM_ALPHA * x_ref[...] + g2_ref[0] * y) * lg_ref[...] + lb_ref[...]
    x2_ref[...] = x2
    h_ref[...] = (_standardize(x2) * (1.0 + sc_ref[0]) + sh_ref[0]).astype(BF16)


def post_moe_norm(x1, y_tiles, modblk, ln_g, ln_b, modblk_next):
    R, D = x1.shape
    row = pl.BlockSpec((ROW_TILE, D), lambda i: (i, 0))
    mod = lambda part: pl.BlockSpec((1, 1, D), lambda i: (i, 0, part))
    vec = pl.BlockSpec((1, D), lambda i: (0, 0))
    y = pl.BlockSpec((ROW_TILE * D // LANES, LANES), lambda i: (i, 0))
    return pl.pallas_call(
        _post_kernel,
        grid=(R // ROW_TILE,),
        in_specs=[row, y, mod(5), vec, vec, mod(0), mod(1)],
        out_specs=[row, row],
        out_shape=[jax.ShapeDtypeStruct((R, D), F32), jax.ShapeDtypeStruct((R, D), BF16)],
        compiler_params=_cp(("parallel",)),
        name="post_moe_norm",
    )(x1, y_tiles, modblk, ln_g.reshape(1, -1), ln_b.reshape(1, -1), modblk_next, modblk_next)


def _final_kernel(x_ref, y_ref, g2_ref, lg_ref, lb_ref, x2_ref):
    y = _load_token_tiles(y_ref, ROW_TILE, x_ref.shape[1])
    x2_ref[...] = _standardize(DEEPNORM_ALPHA * x_ref[...] + g2_ref[0] * y) * lg_ref[...] + lb_ref[...]


def final_norm(x1, y_tiles, modblk, ln_g, ln_b):
    R, D = x1.shape
    n_lat = R // S_ALL * LAT_BLOCKS
    blk = lambda i: i + i // LAT_BLOCKS
    vec = pl.BlockSpec((1, D), lambda i: (0, 0))
    return pl.pallas_call(
        _final_kernel,
        grid=(n_lat,),
        in_specs=[pl.BlockSpec((ROW_TILE, D), lambda i: (blk(i), 0)),
                  pl.BlockSpec((ROW_TILE * D // LANES, LANES), lambda i: (blk(i), 0)),
                  pl.BlockSpec((1, 1, D), lambda i: (blk(i), 0, 5)), vec, vec],
        out_specs=pl.BlockSpec((ROW_TILE, D), lambda i: (i, 0)),
        out_shape=jax.ShapeDtypeStruct((n_lat * ROW_TILE, D), F32),
        compiler_params=_cp(("parallel",)),
        name="final_norm",
    )(x1, y_tiles, modblk, ln_g.reshape(1, -1), ln_b.reshape(1, -1))


def _s5_matrices(a_re, a_im, log_dt, b_re, b_im):
    dt = jnp.exp(log_dt)[:, None]
    mag = jnp.exp(a_re * dt)
    ab_re, ab_im = mag * jnp.cos(a_im * dt), mag * jnp.sin(a_im * dt)
    den = a_re * a_re + a_im * a_im
    num_re, num_im = ab_re - 1.0, ab_im
    coef_re = (num_re * a_re + num_im * a_im) / den
    coef_im = (num_im * a_re - num_re * a_im) / den
    bb_re = coef_re[..., None] * b_re - coef_im[..., None] * b_im
    bb_im = coef_re[..., None] * b_im + coef_im[..., None] * b_re
    gs = S5_GROUPS // S5_SLABS
    eye = jnp.eye(gs, dtype=F32)

    def slabs(t):
        t = t.reshape(S5_SLABS, gs, S5_STATE, S5_GROUP_CH)
        return jnp.einsum('sgpi,gh->sgihp', t, eye).reshape(S5_SLABS, gs * S5_GROUP_CH, gs * S5_STATE)

    lam = jnp.stack([ab_re.reshape(-1), ab_im.reshape(-1)], 0)
    return lam, jnp.concatenate([slabs(bb_re), slabs(bb_im)], -1)


def _s5_readout_matrix(c_re, c_im):
    gs = S5_GROUPS // S5_SLABS
    eye = jnp.eye(gs, dtype=F32)

    def slabs(t):
        t = t.reshape(S5_SLABS, gs, S5_GROUP_CH, S5_STATE)
        return jnp.einsum('sgip,gh->sgphi', t, eye).reshape(S5_SLABS, gs * S5_STATE, gs * S5_GROUP_CH)

    return jnp.concatenate([slabs(c_re), -slabs(c_im)], 1)


def _mla_weights(w_uq, w_ukv):
    qk = MLA_NOPE_DIM + MLA_ROPE_DIM
    wq = w_uq.reshape(MLA_Q_LORA, MLA_HEADS, qk)
    wq = jnp.pad(wq, ((0, 0), (0, 0), (0, MLA_QK_PAD - qk))).reshape(MLA_Q_LORA, MLA_HEADS * MLA_QK_PAD)
    wkv = w_ukv.reshape(MLA_KV_LORA, MLA_HEADS, MLA_NOPE_DIM + MLA_V_DIM)
    wk = jnp.pad(wkv[..., :MLA_NOPE_DIM], ((0, 0), (0, 0), (0, MLA_QK_PAD - MLA_NOPE_DIM)))
    wv = wkv[..., MLA_NOPE_DIM:]
    wkv_pad = jnp.concatenate([wk.reshape(MLA_KV_LORA, -1), wv.reshape(MLA_KV_LORA, -1)], 1)
    return wq.astype(BF16), wkv_pad.astype(BF16)


def kernel(x, c, ctx, c_ctx, ada_w, ada_b, w_in, s5_a_re_f, s5_a_im_f, s5_log_dt_f, s5_a_re_b, s5_a_im_b,
           s5_log_dt_b, s5_b_re, s5_b_im, s5_c_re, s5_c_im, s5_d, s5_w_glu, gqa_q_norm, gqa_k_norm,
           ret_decay_f, ret_decay_b, ret_norm, mla_q_norm, mla_kv_norm, mla_w_uq, mla_w_ukv,
           w_branch, w_gate, b_gate, w_out, ln1_g, ln1_b, router_w, moe_w_gate, moe_w_up, moe_w_down,
           ln2_g, ln2_b):
    B, N, D = x.shape
    R = B * S_ALL
    assert (B, N, D) == (BATCH, SEQ, D_MODEL) and RET_CHUNK == CTX_LEN == ROW_TILE

    cc = jnp.zeros((16, D), F32).at[:B].set(c).at[B].set(c_ctx)
    mod = ada_modulation(cc, ada_w, ada_b)
    sel = np.concatenate([np.r_[np.full(LAT_BLOCKS, b), B] for b in range(B)]).astype(np.int32)
    modblks = [mod[l][sel].reshape(R // ROW_TILE, 1, 6 * D) for l in range(DEPTH)]

    X = jnp.concatenate([x, ctx], 1).reshape(R, D)
    h = modulate_rows(X, modblks[0], 0, 1)

    lam_f, bb_f = jax.vmap(_s5_matrices)(s5_a_re_f, s5_a_im_f, s5_log_dt_f, s5_b_re, s5_b_im)
    lam_b, bb_b = jax.vmap(_s5_matrices)(s5_a_re_b, s5_a_im_b, s5_log_dt_b, s5_b_re, s5_b_im)
    lam_all = jnp.broadcast_to(jnp.stack([lam_f, lam_b], 1)[:, :, :, None, :], (DEPTH, 2, 2, S5_SEG, S5_MODES))
    bb_all = jnp.stack([bb_f, bb_b], 1).astype(BF16)
    cmat_all = jax.vmap(_s5_readout_matrix)(s5_c_re, s5_c_im).astype(BF16)
    wq_all, wkv_all = jax.vmap(_mla_weights)(mla_w_uq, mla_w_ukv)
    w_in_all = jnp.pad(w_in, ((0, 0), (0, 0), (0, IN_PAD - IN_TOTAL))).astype(BF16)

    for l in range(DEPTH):
        need_ctx = l < DEPTH - 1
        modblk = modblks[l]
        u, gq, gk, gv, rq, rk, rv, rg, mq, mk, mv = in_proj_branches(
            h, w_in_all, l, gqa_q_norm[l], gqa_k_norm[l], mla_q_norm[l], mla_kv_norm[l], wq_all[l], wkv_all[l])
        to3 = lambda t: t.reshape(B, S_ALL, t.shape[-1])

        o_s5 = s5_output(s5_scan(u, bb_all[l], lam_all[l], cmat_all[l]), u, s5_d[l], s5_w_glu[l].astype(BF16))

        gq3, gk3, gv3 = to3(gq), to3(gk), to3(gv)
        att = functools.partial(attention, kv_heads=GQA_KV_HEADS, groups=GQA_HEADS // GQA_KV_HEADS,
                                dk=GQA_HEAD_DIM, dv=GQA_HEAD_DIM)
        o_lat = att(gq3, gk3, gv3, q_rows=SEQ, q_off=0, kv_rows=S_ALL, kv_off=0, tq=256)
        no_ctx = jnp.zeros((B, CTX_LEN, BRANCH_W), BF16)
        o_ctx = (att(gq3, gk3, gv3, q_rows=CTX_LEN, q_off=SEQ, kv_rows=CTX_LEN, kv_off=SEQ, tq=256)
                 if need_ctx else no_ctx)
        o_gqa = jnp.concatenate([o_lat, o_ctx], 1).reshape(R, BRANCH_W)

        lgf = -jnp.exp(ret_decay_f[l])
        lgb = -jnp.exp(ret_decay_b[l])
        rkt = jnp.swapaxes(to3(rk), 1, 2)
        o_ret = retention(to3(rq), rkt, to3(rv), to3(rg), ret_norm[l], lgf, lgb, S_ALL).reshape(R, BRANCH_W)

        matt = functools.partial(attention, kv_heads=MLA_HEADS, groups=1, dk=MLA_QK_PAD, dv=MLA_V_DIM)
        m_lat = matt(to3(mq), to3(mk), to3(mv), q_rows=SEQ, q_off=0, kv_rows=S_ALL, kv_off=0, tq=512)
        m_ctx = (matt(to3(mq), to3(mk), to3(mv), q_rows=CTX_LEN, q_off=SEQ, kv_rows=CTX_LEN, kv_off=SEQ, tq=256)
                 if need_ctx else no_ctx)
        o_mla = jnp.concatenate([m_lat, m_ctx], 1).reshape(R, BRANCH_W)

        merged = merge_branches(h, (o_s5, o_gqa, o_ret, o_mla), w_gate, l, b_gate[l], w_branch[l].astype(BF16))
        router_pad = jnp.pad(router_w[l], ((0, 0), (0, 128 - N_EXPERTS)))
        x1, h2, logits = out_proj_norm(merged, X, w_out[l].astype(BF16), modblk, ln1_g[l], ln1_b[l], router_pad)

        logits_t = jnp.swapaxes(logits.reshape(B, S_ALL, 128)[:, :, :N_EXPERTS], 1, 2)
        sets = [(0, SEQ)] + ([(SEQ, CTX_LEN)] if need_ctx else [])
        idx_sets, row_parts, gate_parts = [], [], []
        sample_row0 = (jnp.arange(B, dtype=jnp.int32) * S_ALL)[:, None, None]
        for off, n in sets:
            idx_f, gate_f = route(logits_t, off, n)
            idx = jnp.swapaxes(idx_f[:, :, :N_EXPERTS], 1, 2).astype(jnp.int32)
            gate = jnp.swapaxes(gate_f[:, :, :N_EXPERTS], 1, 2)
            idx_sets.append(idx)
            row_parts.append(jnp.swapaxes(idx + sample_row0 + off, 0, 1).reshape(N_EXPERTS, -1))
            gate_parts.append(jnp.swapaxes(gate, 0, 1).reshape(N_EXPERTS, -1))
        rows = jnp.concatenate(row_parts, 1)[:, None, :]
        gates = jnp.concatenate(gate_parts, 1)[:, :, None]
        xs = gather_rows(rows, h2, D)
        ys = expert_ffn(xs, gates, moe_w_gate, moe_w_up, moe_w_down, l)
        moe = combine(ys, idx_sets, [off for off, _ in sets]).reshape(R * D // LANES, LANES)
        if l == DEPTH - 1:
            return final_norm(x1, moe, modblk, ln2_g[l], ln2_b[l]).reshape(B, SEQ, D)
        X, h = post_moe_norm(x1, moe, modblk, ln2_g[l], ln2_b[l], modblks[l + 1])
```

```python
import functools
import math

import numpy as np
import jax
import jax.numpy as jnp
from jax import lax
from jax.experimental import pallas as pl
from jax.experimental.pallas import tpu as pltpu

F32 = jnp.float32
BF16 = jnp.bfloat16

D_MODEL = 2048
BATCH = 2
SEQ = 4096
DEPTH = 2
GRID_W = 64
CTX_LEN = 256
S_ALL = SEQ + CTX_LEN
N_BRANCH = 4
BRANCH_W = D_MODEL // 4
S5_GROUP_CH = 16
S5_GROUPS = BRANCH_W // S5_GROUP_CH
S5_STATE = 64
S5_MODES = S5_GROUPS * S5_STATE
GQA_HEAD_DIM = 128
GQA_HEADS = 4
GQA_KV_HEADS = 2
RET_HEADS = 4
RET_V_DIM = 128
RET_QK_DIM = 64
MLA_HEADS = 4
MLA_Q_LORA = 512
MLA_KV_LORA = 256
MLA_NOPE_DIM = 128
MLA_ROPE_DIM = 64
MLA_V_DIM = 128
MLA_QK_PAD = 256
N_EXPERTS = 16
EXPERT_FF = D_MODEL // 2
EC_CAPACITY_FACTOR = 2
ROPE_BASE = 10000.0
NORM_EPS = 1e-6
LOG2E = math.log2(math.e)
DEEPNORM_ALPHA = (2 * DEPTH) ** 0.25
IN_WIDTHS = (512, 512, 256, 256, 256, 256, 512, 512, 512, 256, 64)
IN_TOTAL = sum(IN_WIDTHS)
IN_PAD = 4096

LANES = 128
SUBLANES = 8
ROW_TILE = 256
MM_ROW_TILE = 512
MERGE_ROW_TILE = 1088
ATTN_SUB = 256
BLOCKS_PER_SAMPLE = S_ALL // ROW_TILE
LAT_BLOCKS = SEQ // ROW_TILE
S5_SLABS = 4
S5_SEG = 8
S5_STEPS = ROW_TILE // S5_SEG
RET_CHUNK = 256
VMEM_LIMIT = 56 * 1024 * 1024


def _cp(sem, vmem=VMEM_LIMIT):
    return pltpu.CompilerParams(dimension_semantics=sem, vmem_limit_bytes=vmem)


def _standardize(x):
    xc = x - jnp.mean(x, -1, keepdims=True)
    return xc * lax.rsqrt(jnp.mean(xc * xc, -1, keepdims=True) + NORM_EPS)


def _sigmoid(x):
    return 1.0 / (1.0 + jnp.exp(-x))


def _store_token_tiles(ref, x, row0=0):
    rows, w = x.shape
    tt = w // LANES
    for j in range(tt):
        ref[pl.ds(row0 * tt + j, rows, stride=tt), :] = x[:, j * LANES:(j + 1) * LANES]


def _load_token_tiles(ref, rows, w, dtype=F32):
    tt = w // LANES
    return jnp.concatenate([ref[pl.ds(j, rows, stride=tt), :].astype(dtype) for j in range(tt)], -1)


def _ada_kernel(c_ref, w_ref, b_ref, o_ref):
    c = c_ref[...]
    cs = (c * _sigmoid(c)).astype(BF16)
    o_ref[0] = jnp.dot(cs, w_ref[0].astype(BF16), preferred_element_type=F32) + b_ref[0]


def ada_modulation(cc, ada_w, ada_b, tn=1024):
    L, D, N = ada_w.shape
    return pl.pallas_call(
        _ada_kernel,
        grid=(L, N // tn),
        in_specs=[pl.BlockSpec((16, D), lambda l, n: (0, 0)),
                  pl.BlockSpec((1, D, tn), lambda l, n: (l, 0, n)),
                  pl.BlockSpec((1, 1, tn), lambda l, n: (l, 0, n))],
        out_specs=pl.BlockSpec((1, 16, tn), lambda l, n: (l, 0, n)),
        out_shape=jax.ShapeDtypeStruct((L, 16, N), F32),
        compiler_params=_cp(("arbitrary", "arbitrary")),
        name="ada_modulation",
    )(cc, ada_w, ada_b.reshape(L, 1, N))


def _modulate_kernel(x_ref, sh_ref, sc_ref, o_ref):
    o_ref[...] = (_standardize(x_ref[...]) * (1.0 + sc_ref[0]) + sh_ref[0]).astype(o_ref.dtype)


def modulate_rows(x, modblk, shift_part, scale_part):
    R, D = x.shape
    return pl.pallas_call(
        _modulate_kernel,
        grid=(R // ROW_TILE,),
        in_specs=[pl.BlockSpec((ROW_TILE, D), lambda i: (i, 0)),
                  pl.BlockSpec((1, 1, D), lambda i: (i, 0, shift_part)),
                  pl.BlockSpec((1, 1, D), lambda i: (i, 0, scale_part))],
        out_specs=pl.BlockSpec((ROW_TILE, D), lambda i: (i, 0)),
        out_shape=jax.ShapeDtypeStruct((R, D), BF16),
        compiler_params=_cp(("parallel",)),
        name="modulate_rows",
    )(x, modblk, modblk)


def _rope_tables(head_dim, width):
    half = head_dim // 2
    n = half // 2
    inv = ROPE_BASE ** (-np.arange(n, dtype=np.float64) / n)
    t = np.arange(SEQ)
    pos = np.stack([t // GRID_W, t % GRID_W], 0).astype(np.float64)
    lane = np.arange(head_dim)
    which = lane // half
    m = lane % half
    ang = (pos[which, :].T.astype(np.float32) * inv[m % n].astype(np.float32)[None, :]).astype(np.float64)
    cos = np.cos(ang)
    sin = np.where(m < n, -np.sin(ang), np.sin(ang))
    cos = np.concatenate([cos, np.ones((CTX_LEN, head_dim))], 0)
    sin = np.concatenate([sin, np.zeros((CTX_LEN, head_dim))], 0)
    reps = width // head_dim
    return (np.tile(cos, (1, reps)).astype(np.float32), np.tile(sin, (1, reps)).astype(np.float32))


def _mla_q_tables():
    cos64, sin64 = _rope_tables(MLA_ROPE_DIM, MLA_ROPE_DIM)
    ones = np.ones((S_ALL, MLA_NOPE_DIM), np.float32)
    zeros = np.zeros((S_ALL, MLA_NOPE_DIM), np.float32)
    pad1 = np.ones((S_ALL, MLA_QK_PAD - MLA_NOPE_DIM - MLA_ROPE_DIM), np.float32)
    cos = np.concatenate([ones, cos64, pad1], 1)
    sin = np.concatenate([zeros, sin64, 0 * pad1], 1)
    return cos, sin


def _rope(x, cos, sin, quarter):
    w = x.shape[-1]
    lane = lax.broadcasted_iota(jnp.int32, x.shape, 1)
    first = (lane % (2 * quarter)) < quarter
    partner = jnp.where(first, pltpu.roll(x, w - quarter, 1), pltpu.roll(x, quarter, 1))
    return x * cos + partner * sin


def _rms_heads(x, gain, head_dim):
    outs = []
    for h in range(x.shape[-1] // head_dim):
        xh = x[:, h * head_dim:(h + 1) * head_dim]
        outs.append(xh * lax.rsqrt(jnp.mean(xh * xh, -1, keepdims=True) + NORM_EPS) * gain)
    return outs[0] if len(outs) == 1 else jnp.concatenate(outs, -1)


PROJ_TILE = 1024


def _tab(ref_a, ref_b):
    return jnp.concatenate([ref_a[...], ref_b[...]], 0)


def _proj_s5_gq_kernel(h_ref, w_ref, ca, cb, sa, sb, gqn_ref, u_ref, gq_ref):
    p = jnp.dot(h_ref[...], w_ref[...], preferred_element_type=F32)
    u_ref[...] = p[:, :512]
    c128, s128 = _tab(ca, cb), _tab(sa, sb)
    q = _rms_heads(p[:, 512:], gqn_ref[...], GQA_HEAD_DIM)
    q = _rope(q, jnp.concatenate([c128] * 4, -1), jnp.concatenate([s128] * 4, -1), GQA_HEAD_DIM // 4)
    gq_ref[...] = (q * (GQA_HEAD_DIM ** -0.5 * LOG2E)).astype(BF16)


def _proj_kv_ret_kernel(h_ref, w_ref, ca, cb, sa, sb, c6a, c6b, s6a, s6b, gkn_ref,
                        gk_ref, gv_ref, rq_ref, rk_ref):
    p = jnp.dot(h_ref[...], w_ref[...], preferred_element_type=F32)
    tile2 = lambda t: jnp.concatenate([t, t], -1)
    c128, s128, c64, s64 = _tab(ca, cb), _tab(sa, sb), _tab(c6a, c6b), _tab(s6a, s6b)
    k = _rms_heads(p[:, :256], gkn_ref[...], GQA_HEAD_DIM)
    gk_ref[...] = _rope(k, tile2(c128), tile2(s128), GQA_HEAD_DIM // 4).astype(BF16)
    gv_ref[...] = p[:, 256:512].astype(BF16)
    rq_ref[...] = _rope(p[:, 512:768], tile2(c64), tile2(s64), RET_QK_DIM // 4).astype(BF16)
    rk = _rope(p[:, 768:], tile2(c64), tile2(s64), RET_QK_DIM // 4)
    rk_ref[...] = (rk * (RET_QK_DIM ** -0.5)).astype(BF16)


def _proj_ret_vg_kernel(h_ref, w_ref, rv_ref, rg_ref):
    p = jnp.dot(h_ref[...], w_ref[...], preferred_element_type=F32)
    rv_ref[...] = p[:, :512].astype(BF16)
    rg_ref[...] = p[:, 512:]


def _proj_mla_kernel(h_ref, w_ref, c6a, c6b, s6a, s6b, cqa, cqb, sqa, sqb, mqn_ref, mkvn_ref, wq_ref, wkv_ref,
                     q_ref, k_ref, v_ref):
    p = jnp.dot(h_ref[...], w_ref[...], preferred_element_type=F32)
    tm = p.shape[0]
    cqn = _rms_heads(p[:, :512], mqn_ref[...], MLA_Q_LORA).astype(BF16)
    ckvn = _rms_heads(p[:, 512:768], mkvn_ref[...], MLA_KV_LORA).astype(BF16)
    lane = lax.broadcasted_iota(jnp.int32, (tm, 128), 1)
    kr = jnp.where(lane < MLA_ROPE_DIM, _rope(p[:, 768:896], _tab(c6a, c6b), _tab(s6a, s6b), MLA_ROPE_DIM // 4), 0.0)
    scale = (MLA_NOPE_DIM + MLA_ROPE_DIM) ** -0.5 * LOG2E
    q = jnp.dot(cqn, wq_ref[...], preferred_element_type=F32)
    kv = jnp.dot(ckvn, wkv_ref[...], preferred_element_type=F32)
    cos, sin = _tab(cqa, cqb), _tab(sqa, sqb)
    krp = jnp.concatenate([jnp.zeros((tm, MLA_NOPE_DIM), F32), kr], -1)
    for h in range(MLA_HEADS):
        sl = slice(h * MLA_QK_PAD, (h + 1) * MLA_QK_PAD)
        q_ref[:, sl] = (_rope(q[:, sl], cos, sin, MLA_ROPE_DIM // 4) * scale).astype(BF16)
        k_ref[:, sl] = (kv[:, sl] + krp).astype(BF16)
    v_ref[...] = kv[:, MLA_HEADS * MLA_QK_PAD:].astype(BF16)


def in_proj_branches(h, w_in_p, gqa_q_norm, gqa_k_norm, mla_q_norm, mla_kv_norm, wq_pad, wkv_pad,
                     tm=MM_ROW_TILE):
    R, D = h.shape
    half = tm // 2
    assert half == ROW_TILE
    c128, s128 = (jnp.asarray(t) for t in _rope_tables(GQA_HEAD_DIM, 128))
    c64, s64 = (jnp.asarray(t) for t in _rope_tables(RET_QK_DIM, 128))
    cq, sq = (jnp.asarray(t) for t in _mla_q_tables())
    x_spec = pl.BlockSpec((tm, D), lambda i: (i, 0))
    w_spec = lambda n: pl.BlockSpec((D, PROJ_TILE), lambda i: (0, n))
    row = lambda w: pl.BlockSpec((tm, w), lambda i: (i, 0))
    vec = lambda w: pl.BlockSpec((1, w), lambda i: (0, 0))
    full = lambda a: pl.BlockSpec(a.shape, lambda i: (0, 0))

    def tabs(t):
        w = t.shape[1]
        return ([pl.BlockSpec((half, w), lambda i: ((2 * i) % BLOCKS_PER_SAMPLE, 0)),
                 pl.BlockSpec((half, w), lambda i: ((2 * i + 1) % BLOCKS_PER_SAMPLE, 0))], [t, t])

    def call(kernel, n, extra_specs, extra_args, outs, name):
        return pl.pallas_call(
            kernel, grid=(R // tm,),
            in_specs=[x_spec, w_spec(n)] + extra_specs,
            out_specs=[row(w) for w, _ in outs],
            out_shape=[jax.ShapeDtypeStruct((R, w), dt) for w, dt in outs],
            compiler_params=_cp(("parallel",)), name=name,
        )(h, w_in_p, *extra_args)

    def gather_tabs(*ts):
        specs, args = [], []
        for t in ts:
            s, a = tabs(t)
            specs += s
            args += a
        return specs, args

    s0, a0 = gather_tabs(c128, s128)
    u, gq = call(_proj_s5_gq_kernel, 0, s0 + [vec(128)], a0 + [gqa_q_norm.reshape(1, -1)],
                 [(512, F32), (512, BF16)], "proj_s5_gq")
    s1, a1 = gather_tabs(c128, s128, c64, s64)
    gk, gv, rq, rk = call(_proj_kv_ret_kernel, 1, s1 + [vec(128)], a1 + [gqa_k_norm.reshape(1, -1)],
                          [(256, BF16)] * 4, "proj_kv_ret")
    rv, rg = call(_proj_ret_vg_kernel, 2, [], [], [(512, BF16), (512, F32)], "proj_ret_vg")
    s3, a3 = gather_tabs(c64, s64, cq, sq)
    hq = MLA_HEADS * MLA_QK_PAD
    mq, mk, mv = call(_proj_mla_kernel, 3, s3 + [vec(512), vec(256), full(wq_pad), full(wkv_pad)],
                      a3 + [mla_q_norm.reshape(1, -1), mla_kv_norm.reshape(1, -1), wq_pad, wkv_pad],
                      [(hq, BF16), (hq, BF16), (MLA_HEADS * MLA_V_DIM, BF16)], "proj_mla")
    return u, gq, gk, gv, rq, rk, rv, rg, mq, mk, mv


def _attn_kernel(q_ref, k_ref, v_ref, o_ref, *, groups, dk, dv):
    k = k_ref[0]
    v = v_ref[0]
    v1 = jnp.concatenate([v, jnp.ones_like(v)], -1)
    tq = q_ref.shape[1]
    q = jnp.concatenate([q_ref[0, :, g * dk:(g + 1) * dk] for g in range(groups)], 0)
    s = lax.dot_general(q, k, (((1,), (1,)), ((), ())), preferred_element_type=F32)
    for g in range(groups):
        for half in range(tq // ATTN_SUB):
            r0 = g * tq + half * ATTN_SUB
            sh = s[r0:r0 + ATTN_SUB]
            p = jnp.exp2(sh - jnp.max(sh, -1, keepdims=True)).astype(BF16)
            o = jnp.dot(p, v1, preferred_element_type=F32)
            rows = slice(half * ATTN_SUB, (half + 1) * ATTN_SUB)
            o_ref[0, rows, g * dv:(g + 1) * dv] = (o[:, :dv] / o[:, dv:]).astype(o_ref.dtype)


def attention(q, k, v, *, kv_heads, groups, dk, dv, q_rows, q_off, kv_rows, kv_off, tq):
    B = q.shape[0]
    assert tq % ATTN_SUB == 0 and q_off % tq == 0 and q_rows % tq == 0
    qb0, kb0 = q_off // tq, kv_off // kv_rows
    return pl.pallas_call(
        functools.partial(_attn_kernel, groups=groups, dk=dk, dv=dv),
        grid=(B, kv_heads, q_rows // tq),
        in_specs=[pl.BlockSpec((1, tq, groups * dk), lambda b, h, i: (b, qb0 + i, h)),
                  pl.BlockSpec((1, kv_rows, dk), lambda b, h, i: (b, kb0, h)),
                  pl.BlockSpec((1, kv_rows, dv), lambda b, h, i: (b, kb0, h))],
        out_specs=pl.BlockSpec((1, tq, groups * dv), lambda b, h, i: (b, i, h)),
        out_shape=jax.ShapeDtypeStruct((B, q_rows, kv_heads * groups * dv), BF16),
        compiler_params=_cp(("parallel", "parallel", "arbitrary")),
        name="attention",
    )(q, k, v)


def _ret_kernel(lgf_ref, lgb_ref, q_ref, kt_ref, v_ref, g_ref, gain_ref, o_ref, sb_ref, *, n_out):
    L = RET_CHUNK
    nc = SEQ // L
    pair = pl.program_id(1)
    r_i = lax.broadcasted_iota(jnp.int32, (L, L), 0)
    c_i = lax.broadcasted_iota(jnp.int32, (L, L), 1)
    diff = (r_i - c_i).astype(F32)
    pos_col = lax.broadcasted_iota(jnp.int32, (L, 1), 0).astype(F32)
    pos_row = lax.broadcasted_iota(jnp.int32, (1, L), 1).astype(F32)
    for j in range(2):
        lgf = lgf_ref[pair * 2 + j]
        lgb = lgb_ref[pair * 2 + j]
        dmat = jnp.where(diff >= 0, jnp.exp(lgf * jnp.maximum(diff, 0.0)), jnp.exp(lgb * jnp.maximum(-diff, 0.0)))
        dq_f = jnp.exp(lgf * (pos_col + 1.0))
        dq_b = jnp.exp(lgb * (L - pos_col))
        dk_f = jnp.exp(lgf * (L - 1.0 - pos_row))
        dk_b = jnp.exp(lgb * pos_row)
        dc_f = jnp.exp(lgf * L)
        dc_b = jnp.exp(lgb * L)
        qs = slice(j * RET_QK_DIM, (j + 1) * RET_QK_DIM)
        vs = slice(j * RET_V_DIM, (j + 1) * RET_V_DIM)

        def chunk(c):
            rows = slice(c * L, (c + 1) * L)
            return q_ref[0, rows, qs], kt_ref[0, qs, rows], v_ref[0, rows, vs]

        def readout(o, c, out_row0):
            oc = o - jnp.mean(o, -1, keepdims=True)
            on = oc * lax.rsqrt(jnp.mean(oc * oc, -1, keepdims=True) + NORM_EPS)
            g = g_ref[0, c * L:(c + 1) * L, vs]
            o_ref[0, out_row0:out_row0 + L, vs] = (on * gain_ref[:, vs] * (g * _sigmoid(g))).astype(o_ref.dtype)

        def intra(q, kt, v):
            sc = jnp.dot(q, kt, preferred_element_type=F32) * dmat
            return jnp.dot(sc.astype(BF16), v, preferred_element_type=F32)

        def state_add(kt, dk, v):
            return jnp.dot((kt.astype(F32) * dk).astype(BF16), v, preferred_element_type=F32)

        qc, ktc, vc = chunk(nc)
        if n_out > SEQ:
            readout(intra(qc, ktc, vc), nc, SEQ)
        s_f = state_add(ktc, dk_f, vc)
        s_b = state_add(ktc, dk_b, vc)
        for c in range(nc - 1, -1, -1):
            sb_ref[c] = s_b
            _, kt, v = chunk(c)
            s_b = s_b * dc_b + state_add(kt, dk_b, v)
        for c in range(nc):
            q, kt, v = chunk(c)
            qf = q.astype(F32)
            o = (intra(q, kt, v)
                 + jnp.dot((qf * dq_f).astype(BF16), s_f.astype(BF16), preferred_element_type=F32)
                 + jnp.dot((qf * dq_b).astype(BF16), sb_ref[c].astype(BF16), preferred_element_type=F32))
            readout(o, c, c * L)
            s_f = s_f * dc_f + state_add(kt, dk_f, v)


def retention(rq, rkt, rv, rg, gain, lgf, lgb, n_out):
    B = rq.shape[0]
    smem = pl.BlockSpec(memory_space=pltpu.SMEM)
    return pl.pallas_call(
        functools.partial(_ret_kernel, n_out=n_out),
        grid=(B, RET_HEADS // 2),
        in_specs=[smem, smem,
                  pl.BlockSpec((1, S_ALL, 128), lambda b, p: (b, 0, p)),
                  pl.BlockSpec((1, 128, S_ALL), lambda b, p: (b, p, 0)),
                  pl.BlockSpec((1, S_ALL, 256), lambda b, p: (b, 0, p)),
                  pl.BlockSpec((1, S_ALL, 256), lambda b, p: (b, 0, p)),
                  pl.BlockSpec((1, 256), lambda b, p: (0, p))],
        out_specs=pl.BlockSpec((1, n_out, 256), lambda b, p: (b, 0, p)),
        out_shape=jax.ShapeDtypeStruct((B, n_out, RET_HEADS * RET_V_DIM), BF16),
        scratch_shapes=[pltpu.VMEM((SEQ // RET_CHUNK, RET_QK_DIM, RET_V_DIM), F32)],
        compiler_params=_cp(("parallel", "parallel")),
        name="retention",
    )(lgf, lgb, rq, rkt, rv, rg, gain.reshape(1, -1))


def _s5_kernel(u_ref, bb_ref, lam_ref, cm_ref, y_ref, bu_ref, pw_ref, st_ref):
    M = S5_MODES
    chunk = pl.program_id(2)
    lam_re = lam_ref[0, 0]
    lam_im = lam_ref[0, 1]

    @pl.when(chunk == 0)
    def _():
        st_ref[...] = jnp.zeros_like(st_ref)
        p_re, p_im = lam_re, lam_im
        for j in range(S5_STEPS):
            pw_ref[0, j] = p_re
            pw_ref[1, j] = p_im
            p_re, p_im = p_re * lam_re - p_im * lam_im, p_re * lam_im + p_im * lam_re

    r_i = lax.broadcasted_iota(jnp.int32, (ROW_TILE, ROW_TILE), 0)
    c_i = lax.broadcasted_iota(jnp.int32, (ROW_TILE, ROW_TILE), 1)
    flip = pl.program_id(0) == 1

    def scan_time(r):
        t = (r % S5_SEG) * S5_STEPS + r // S5_SEG
        return jnp.where(flip, ROW_TILE - 1 - t, t)

    to_scan = (c_i == scan_time(r_i)).astype(BF16)
    to_time = (r_i == scan_time(c_i)).astype(BF16)
    u = jnp.dot(to_scan, u_ref[...].astype(BF16), preferred_element_type=F32).astype(BF16)

    slab = BRANCH_W // S5_SLABS
    ms = M // S5_SLABS
    for s in range(S5_SLABS):
        part = jnp.dot(u[:, s * slab:(s + 1) * slab], bb_ref[0, s], preferred_element_type=F32)
        bu_ref[:, s * ms:(s + 1) * ms] = part[:, :ms]
        bu_ref[:, M + s * ms:M + (s + 1) * ms] = part[:, ms:]

    tile = 512
    for t in range(M // tile):
        re_sl = slice(t * tile, (t + 1) * tile)
        im_sl = slice(M + t * tile, M + (t + 1) * tile)
        lr, li = lam_re[:, re_sl], lam_im[:, re_sl]

        def step(j, carry):
            s_re, s_im = carry
            rows = pl.ds(pl.multiple_of(j * S5_SEG, S5_SEG), S5_SEG)
            n_re = lr * s_re - li * s_im + bu_ref[rows, re_sl]
            n_im = lr * s_im + li * s_re + bu_ref[rows, im_sl]
            bu_ref[rows, re_sl] = n_re
            bu_ref[rows, im_sl] = n_im
            return n_re, n_im

        z = jnp.zeros((S5_SEG, tile), F32)
        lax.fori_loop(0, S5_STEPS, step, (z, z))

    last = slice((S5_STEPS - 1) * S5_SEG, S5_STEPS * S5_SEG)
    e_re, e_im = bu_ref[last, 0:M], bu_ref[last, M:2 * M]
    pl_re, pl_im = pw_ref[0, S5_STEPS - 1][0:1], pw_ref[1, S5_STEPS - 1][0:1]
    c_re, c_im = st_ref[0:1, :], st_ref[1:2, :]
    rows_re, rows_im = [], []
    for k in range(S5_SEG):
        rows_re.append(c_re)
        rows_im.append(c_im)
        c_re, c_im = (e_re[k:k + 1] + pl_re * c_re - pl_im * c_im,
                      e_im[k:k + 1] + pl_re * c_im + pl_im * c_re)
    st_ref[0:1, :] = c_re
    st_ref[1:2, :] = c_im
    car_re = jnp.concatenate(rows_re, 0)
    car_im = jnp.concatenate(rows_im, 0)

    def fix(j, _):
        rows = pl.ds(pl.multiple_of(j * S5_SEG, S5_SEG), S5_SEG)
        p_re, p_im = pw_ref[0, j], pw_ref[1, j]
        bu_ref[rows, 0:M] = bu_ref[rows, 0:M] + p_re * car_re - p_im * car_im
        bu_ref[rows, M:2 * M] = bu_ref[rows, M:2 * M] + p_re * car_im + p_im * car_re
        return 0

    lax.fori_loop(0, S5_STEPS, fix, 0)

    ys = []
    for s in range(S5_SLABS):
        hs = jnp.concatenate([bu_ref[:, s * ms:(s + 1) * ms], bu_ref[:, M + s * ms:M + (s + 1) * ms]], -1)
        ys.append(jnp.dot(hs.astype(BF16), cm_ref[s], preferred_element_type=F32))
    y = jnp.concatenate(ys, -1)
    out = None
    for _ in range(2):
        piece = y.astype(BF16)
        y = y - piece.astype(F32)
        term = jnp.dot(to_time, piece, preferred_element_type=F32)
        out = term if out is None else out + term
    y_ref[0] = out


def _s5_block(d, c):
    fwd = (c + LAT_BLOCKS) % BLOCKS_PER_SAMPLE
    bwd = jnp.where(c == 0, LAT_BLOCKS, LAT_BLOCKS - c)
    return jnp.where(d == 0, fwd, bwd)


def s5_scan(u, bb, lam, cmat):
    R = u.shape[0]
    B = R // S_ALL
    W = BRANCH_W
    row_block = lambda d, b, c: b * BLOCKS_PER_SAMPLE + _s5_block(d, c)
    return pl.pallas_call(
        _s5_kernel,
        grid=(2, B, BLOCKS_PER_SAMPLE),
        in_specs=[pl.BlockSpec((ROW_TILE, W), lambda d, b, c: (row_block(d, b, c), 0)),
                  pl.BlockSpec((1,) + bb.shape[1:], lambda d, b, c: (d, 0, 0, 0)),
                  pl.BlockSpec((1, 2, S5_SEG, S5_MODES), lambda d, b, c: (d, 0, 0, 0)),
                  pl.BlockSpec(cmat.shape, lambda d, b, c: (0, 0, 0))],
        out_specs=pl.BlockSpec((1, ROW_TILE, W), lambda d, b, c: (d, row_block(d, b, c), 0)),
        out_shape=jax.ShapeDtypeStruct((2, R, W), F32),
        scratch_shapes=[pltpu.VMEM((ROW_TILE, 2 * S5_MODES), F32),
                        pltpu.VMEM((2, S5_STEPS, S5_SEG, S5_MODES), F32),
                        pltpu.VMEM((8, S5_MODES), F32)],
        compiler_params=_cp(("arbitrary", "arbitrary", "arbitrary")),
        name="s5_scan",
    )(u, bb, lam, cmat)


def _s5_out_kernel(yf_ref, yb_ref, p_ref, d_ref, w_ref, o_ref):
    y = yf_ref[0] + yb_ref[0] + d_ref[...] * p_ref[...]
    z = 0.5 * y * (1.0 + jnp.tanh(math.sqrt(2.0 / math.pi) * (y + 0.044715 * (y * y * y))))
    gate = _sigmoid(jnp.dot(z.astype(BF16), w_ref[...], preferred_element_type=F32))
    o_ref[...] = (z * gate).astype(o_ref.dtype)


def s5_output(y_dirs, u, d, w_glu):
    R = y_dirs.shape[1]
    return pl.pallas_call(
        _s5_out_kernel,
        grid=(R // ROW_TILE,),
        in_specs=[pl.BlockSpec((1, ROW_TILE, 512), lambda i: (0, i, 0)),
                  pl.BlockSpec((1, ROW_TILE, 512), lambda i: (1, i, 0)),
                  pl.BlockSpec((ROW_TILE, 512), lambda i: (i, 0)),
                  pl.BlockSpec((1, 512), lambda i: (0, 0)),
                  pl.BlockSpec((512, 512), lambda i: (0, 0))],
        out_specs=pl.BlockSpec((ROW_TILE, 512), lambda i: (i, 0)),
        out_shape=jax.ShapeDtypeStruct((R, 512), BF16),
        compiler_params=_cp(("parallel",)),
        name="s5_output",
    )(y_dirs, y_dirs, u, d.reshape(1, -1), w_glu)


def _merge_kernel(h_ref, o0_ref, o1_ref, o2_ref, o3_ref, g0_ref, g1_ref, g2_ref, g3_ref,
                  bg_ref, wb_ref, out_ref, wg_ref):
    @pl.when(pl.program_id(1) == 0)
    def _():
        for k, g in enumerate((g0_ref, g1_ref, g2_ref, g3_ref)):
            wg_ref[k] = g[0].astype(BF16)

    h = h_ref[...]
    acc = None
    for k, o in enumerate((o0_ref, o1_ref, o2_ref, o3_ref)):
        gate = _sigmoid(jnp.dot(h, wg_ref[k], preferred_element_type=F32) + bg_ref[k])
        term = gate * jnp.dot(o[...], wb_ref[k], preferred_element_type=F32)
        acc = term if acc is None else acc + term
    out_ref[...] = acc.astype(out_ref.dtype)


def merge_branches(h, outs, w_gate_all, layer, b_gate, w_branch, tn=256, tm=MERGE_ROW_TILE):
    R, D = h.shape
    nb = D // tn
    gate_spec = lambda k: pl.BlockSpec((1, D, tn), lambda n, m: (layer, 0, k * nb + n))
    bg = b_gate.reshape(N_BRANCH, 1, D)
    return pl.pallas_call(
        _merge_kernel,
        grid=(nb, R // tm),
        in_specs=[pl.BlockSpec((tm, D), lambda n, m: (m, 0))]
                 + [pl.BlockSpec((tm, BRANCH_W), lambda n, m: (m, 0))] * N_BRANCH
                 + [gate_spec(k) for k in range(N_BRANCH)]
                 + [pl.BlockSpec((N_BRANCH, 1, tn), lambda n, m: (0, 0, n)),
                    pl.BlockSpec((N_BRANCH, BRANCH_W, tn), lambda n, m: (0, 0, n))],
        out_specs=pl.BlockSpec((tm, tn), lambda n, m: (m, n)),
        out_shape=jax.ShapeDtypeStruct((R, D), BF16),
        scratch_shapes=[pltpu.VMEM((N_BRANCH, D, tn), BF16)],
        compiler_params=_cp(("arbitrary", "arbitrary")),
        name="merge_branches",
    )(h, *outs, w_gate_all, w_gate_all, w_gate_all, w_gate_all, bg, w_branch)


def _out_kernel(m_ref, x_ref, w_ref, g1_ref, lg_ref, lb_ref, sh_ref, sc_ref, rw_ref,
                x1_ref, h2_ref, lo_ref):
    y = jnp.dot(m_ref[...], w_ref[...], preferred_element_type=F32)
    x1 = _standardize(DEEPNORM_ALPHA * x_ref[...] + g1_ref[0] * y) * lg_ref[...] + lb_ref[...]
    x1_ref[...] = x1
    h2 = _standardize(x1) * (1.0 + sc_ref[0]) + sh_ref[0]
    _store_token_tiles(h2_ref, h2)
    rw = rw_ref[...]
    h_hi, rw_hi = h2.astype(BF16), rw.astype(BF16)
    h_lo, rw_lo = (h2 - h_hi.astype(F32)).astype(BF16), (rw - rw_hi.astype(F32)).astype(BF16)
    lo_ref[...] = (jnp.dot(h_hi, rw_hi, preferred_element_type=F32)
                   + jnp.dot(h_lo, rw_hi, preferred_element_type=F32)
                   + jnp.dot(h_hi, rw_lo, preferred_element_type=F32))


def out_proj_norm(merged, x, w_out, modblk, ln_g, ln_b, router_pad):
    R, D = x.shape
    tt = D // LANES
    row = lambda w: pl.BlockSpec((ROW_TILE, w), lambda i: (i, 0))
    mod = lambda part: pl.BlockSpec((1, 1, D), lambda i: (i, 0, part))
    vec = pl.BlockSpec((1, D), lambda i: (0, 0))
    return pl.pallas_call(
        _out_kernel,
        grid=(R // ROW_TILE,),
        in_specs=[row(D), row(D), pl.BlockSpec((D, D), lambda i: (0, 0)), mod(2), vec, vec, mod(3), mod(4),
                  pl.BlockSpec((D, 128), lambda i: (0, 0))],
        out_specs=[row(D), pl.BlockSpec((ROW_TILE * tt, LANES), lambda i: (i, 0)), row(128)],
        out_shape=[jax.ShapeDtypeStruct((R, D), F32), jax.ShapeDtypeStruct((R * tt, LANES), F32),
                   jax.ShapeDtypeStruct((R, 128), F32)],
        compiler_params=_cp(("parallel",)),
        name="out_proj_norm",
    )(merged, x, w_out, modblk, ln_g.reshape(1, -1), ln_b.reshape(1, -1), modblk, modblk, router_pad)


ROUTE_GEOMETRIC_STEPS = 40
ROUTE_BISECT_STEPS = ROUTE_GEOMETRIC_STEPS + 8


def _route_kernel(lt_ref, idx_ref, gate_ref, slot_ref, aff_ref, *, n, cap):
    E = N_EXPERTS
    lt = lt_ref[0]
    ex = jnp.exp(lt - jnp.max(lt, 0, keepdims=True))
    aff = ex / jnp.sum(ex, 0, keepdims=True)

    def count_ge(v):
        return jnp.sum((aff >= v).astype(F32), 1, keepdims=True)

    tiny = jnp.full((E, 1), float(np.finfo(np.float32).tiny), F32)
    normal = count_ge(tiny) >= cap
    lo0 = jnp.where(normal, tiny, 0.0)
    hi0 = jnp.where(normal, 2.0, tiny)

    def bisect(i, lo_hi):
        lo, hi = lo_hi
        geo = jnp.clip(jnp.sqrt(lo) * jnp.sqrt(hi), lo, hi)
        mid = jnp.where(jnp.logical_and(i < ROUTE_GEOMETRIC_STEPS, lo > 0.0), geo, lo + 0.5 * (hi - lo))
        ok = count_ge(mid) >= cap
        return jnp.where(ok, mid, lo), jnp.where(ok, hi, mid)

    lo, _ = lax.fori_loop(0, ROUTE_BISECT_STEPS, bisect, (lo0, hi0))
    thr = jnp.min(jnp.where(aff >= lo, aff, 2.0), 1, keepdims=True)
    gt = aff > thr
    eq = aff == thr
    need = cap - jnp.sum(gt.astype(F32), 1, keepdims=True)

    blk = min(n, 512)
    upper = (lax.broadcasted_iota(jnp.int32, (blk, blk), 0)
             < lax.broadcasted_iota(jnp.int32, (blk, blk), 1)).astype(BF16)

    def excl_cumsum(mask):
        parts, carry = [], jnp.zeros((E, 1), F32)
        for j in range(n // blk):
            m = mask[:, j * blk:(j + 1) * blk].astype(F32)
            parts.append(jnp.dot(m.astype(BF16), upper, preferred_element_type=F32) + carry)
            carry = carry + jnp.sum(m, 1, keepdims=True)
        return parts[0] if len(parts) == 1 else jnp.concatenate(parts, 1)

    sel = gt | (eq & (excl_cumsum(eq) < need))
    slot_ref[...] = jnp.where(sel, excl_cumsum(sel), -1.0)
    idx_ref[0] = jnp.zeros((cap, 128), F32)
    gate_ref[0] = jnp.zeros((cap, 128), F32)

    aff_ref[...] = aff
    rows = min(cap, 64)
    lane_e = lax.broadcasted_iota(jnp.int32, (rows, 128), 1)
    tok = lax.broadcasted_iota(jnp.int32, (1, 128), 1).astype(F32)

    for e in range(E):
        def per_rows(c, _, e=e):
            r0 = pl.multiple_of(c * rows, rows)
            s_col = (lax.broadcasted_iota(jnp.int32, (rows, 1), 0) + r0).astype(F32)
            acc_i = jnp.zeros((rows, 128), F32)
            acc_g = jnp.zeros((rows, 128), F32)
            for j in range(n // 128):
                hit = slot_ref[e:e + 1, j * 128:(j + 1) * 128] == s_col
                acc_i = acc_i + jnp.where(hit, tok + float(j * 128), 0.0)
                acc_g = acc_g + jnp.where(hit, aff_ref[e:e + 1, j * 128:(j + 1) * 128], 0.0)
            icol = jnp.sum(acc_i, 1, keepdims=True)
            gcol = jnp.sum(acc_g, 1, keepdims=True)
            idx_ref[0, pl.ds(r0, rows), :] = jnp.where(lane_e == e, icol, idx_ref[0, pl.ds(r0, rows), :])
            gate_ref[0, pl.ds(r0, rows), :] = jnp.where(lane_e == e, gcol, gate_ref[0, pl.ds(r0, rows), :])
            return 0

        lax.fori_loop(0, cap // rows, per_rows, 0)


def route(logits_t, off, n):
    B = logits_t.shape[0]
    cap = EC_CAPACITY_FACTOR * n // N_EXPERTS
    out = pl.BlockSpec((1, cap, 128), lambda b: (b, 0, 0))
    return pl.pallas_call(
        functools.partial(_route_kernel, n=n, cap=cap),
        grid=(B,),
        in_specs=[pl.BlockSpec((1, N_EXPERTS, n), lambda b: (b, 0, off // n))],
        out_specs=[out, out],
        out_shape=[jax.ShapeDtypeStruct((B, cap, 128), F32)] * 2,
        scratch_shapes=[pltpu.VMEM((N_EXPERTS, n), F32), pltpu.VMEM((N_EXPERTS, n), F32)],
        compiler_params=_cp(("parallel",)),
        name="route",
    )(logits_t)


GATHER_UNROLL = 8


def _gather_kernel(rows_ref, next_ref, h_hbm, o_ref, land0_ref, land1_ref, sems, *, T, tt):
    e = pl.program_id(0)
    lands = (land0_ref, land1_ref)

    def tile_copy(slot, s, r):
        src = h_hbm.at[pl.ds(pl.multiple_of(r * tt, tt), tt), :]
        dst = lands[slot].at[pl.ds(pl.multiple_of(s * tt, tt), tt), :]
        return pltpu.make_async_copy(src, dst, sems.at[slot])

    def issue(slot, ids_ref):
        def body(g, _):
            for i in range(GATHER_UNROLL):
                s = g * GATHER_UNROLL + i
                tile_copy(slot, s, ids_ref[0, 0, s]).start()
            return 0

        lax.fori_loop(0, T // GATHER_UNROLL, body, 0)

    def drain(slot):
        def body(g, _):
            for i in range(GATHER_UNROLL):
                tile_copy(slot, g * GATHER_UNROLL + i, 0).wait()
            return 0

        lax.fori_loop(0, T // GATHER_UNROLL, body, 0)

    @pl.when(e == 0)
    def _():
        issue(0, rows_ref)

    for slot in range(2):
        @pl.when(e % 2 == slot)
        def _(slot=slot):
            @pl.when(e + 1 < pl.num_programs(0))
            def _():
                issue(1 - slot, next_ref)

            drain(slot)
            o_ref[0] = _load_token_tiles(lands[slot], T, tt * LANES, BF16)


def gather_rows(rows, h_tiles, D):
    E, _, T = rows.shape
    tt = D // LANES
    ids = lambda off: pl.BlockSpec((1, 1, T), lambda e: (jnp.minimum(e + off, E - 1), 0, 0),
                                   memory_space=pltpu.SMEM)
    return pl.pallas_call(
        functools.partial(_gather_kernel, T=T, tt=tt),
        grid=(E,),
        in_specs=[ids(0), ids(1), pl.BlockSpec(memory_space=pl.ANY)],
        out_specs=pl.BlockSpec((1, T, D), lambda e: (e, 0, 0)),
        out_shape=jax.ShapeDtypeStruct((E, T, D), BF16),
        scratch_shapes=[pltpu.VMEM((T * tt, LANES), F32), pltpu.VMEM((T * tt, LANES), F32),
                        pltpu.SemaphoreType.DMA((2,))],
        compiler_params=_cp(("arbitrary",)),
        name="gather_rows",
    )(rows, rows, h_tiles)


def _expert_kernel(x_ref, gate_ref, wg_ref, wu_ref, wd_ref, o_ref):
    last = pl.num_programs(1) - 1
    x = x_ref[0]
    a = jnp.dot(x, wg_ref[0, 0].astype(BF16), preferred_element_type=F32)
    u = jnp.dot(x, wu_ref[0, 0].astype(BF16), preferred_element_type=F32)
    hid = (a * _sigmoid(a) * u).astype(BF16)
    part = jnp.dot(hid, wd_ref[0, 0].astype(BF16), preferred_element_type=F32)

    @pl.when(pl.program_id(1) == 0)
    def _():
        o_ref[0] = part

    @pl.when(jnp.logical_and(pl.program_id(1) != 0, pl.program_id(1) != last))
    def _():
        o_ref[0] += part

    @pl.when(pl.program_id(1) == last)
    def _():
        o_ref[0] = (o_ref[0] + part) * gate_ref[0]


def expert_ffn(xs, gate, w_gate, w_up, w_down, layer, tf=256):
    E, T, D = xs.shape
    FF = w_gate.shape[-1]
    return pl.pallas_call(
        _expert_kernel,
        grid=(E, FF // tf),
        in_specs=[pl.BlockSpec((1, T, D), lambda e, f: (e, 0, 0)),
                  pl.BlockSpec((1, T, 1), lambda e, f: (e, 0, 0)),
                  pl.BlockSpec((1, 1, D, tf), lambda e, f: (layer, e, 0, f)),
                  pl.BlockSpec((1, 1, D, tf), lambda e, f: (layer, e, 0, f)),
                  pl.BlockSpec((1, 1, tf, D), lambda e, f: (layer, e, f, 0))],
        out_specs=pl.BlockSpec((1, T, D), lambda e, f: (e, 0, 0)),
        out_shape=jax.ShapeDtypeStruct((E, T, D), F32),
        compiler_params=_cp(("parallel", "arbitrary")),
        name="expert_ffn",
    )(xs, gate, w_gate, w_up, w_down)


COMBINE_UNROLL = 8


def _combine_kernel(*refs, caps, bases, dh):
    n_sets = len(caps)
    idx_refs, y_refs = refs[:n_sets], refs[n_sets:2 * n_sets]
    o_ref, yt_ref = refs[2 * n_sets], refs[2 * n_sets + 1]
    tt = dh // LANES

    @pl.when(pl.program_id(2) == 0)
    def _():
        o_ref[...] = jnp.zeros_like(o_ref)

    for idx_ref, y_ref, cap, base in zip(idx_refs, y_refs, caps, bases):
        for j in range(tt):
            yt_ref[pl.ds(j, cap, stride=tt), :] = y_ref[0, :, j * LANES:(j + 1) * LANES]

        def body(g, _, idx_ref=idx_ref, base=base):
            s0 = g * COMBINE_UNROLL
            toks = [idx_ref[0, 0, 0, s0 + i] + base for i in range(COMBINE_UNROLL)]
            sums = [o_ref[0, toks[i], 0] + yt_ref[pl.ds(pl.multiple_of((s0 + i) * tt, tt), tt), :]
                    for i in range(COMBINE_UNROLL)]
            for i in range(COMBINE_UNROLL):
                o_ref[0, toks[i], 0] = sums[i]
            return 0

        lax.fori_loop(0, cap // COMBINE_UNROLL, body, 0)


def combine(ys, idx_sets, bases, dh=SUBLANES * LANES):
    E, T, D = ys.shape
    B = idx_sets[0].shape[0]
    caps = tuple(int(i.shape[-1]) for i in idx_sets)
    starts = np.concatenate([[0], np.cumsum([B * c for c in caps])[:-1]])
    idx_specs = [pl.BlockSpec((1, 1, 1, c), lambda b, hf, e: (b, e, 0, 0), memory_space=pltpu.SMEM) for c in caps]
    y_specs = [pl.BlockSpec((1, c, dh), lambda b, hf, e, blk0=int(st) // c: (e, blk0 + b, hf))
               for c, st in zip(caps, starts)]
    return pl.pallas_call(
        functools.partial(_combine_kernel, caps=caps, bases=tuple(bases), dh=dh),
        grid=(B, D // dh, E),
        in_specs=idx_specs + y_specs,
        out_specs=pl.BlockSpec((1, S_ALL, 1, SUBLANES, LANES), lambda b, hf, e: (b, 0, hf, 0, 0)),
        out_shape=jax.ShapeDtypeStruct((B, S_ALL, D // dh, SUBLANES, LANES), F32),
        scratch_shapes=[pltpu.VMEM((max(caps) * dh // LANES, LANES), F32)],
        compiler_params=_cp(("parallel", "parallel", "arbitrary")),
        name="combine",
    )(*[i[:, :, None, :] for i in idx_sets], *([ys] * len(caps)))


def _post_kernel(x_ref, y_ref, g2_ref, lg_ref, lb_ref, sh_ref, sc_ref, x2_ref, h_ref):
    y = _load_token_tiles(y_ref, ROW_TILE, x_ref.shape[1])
    x2 = _standardize(DEEPNORM_ALPHA * x_ref[...] + g2_ref[0] * y) * lg_ref[...] + lb_ref[...]
    x2_ref[...] = x2
    h_ref[...] = (_standardize(x2) * (1.0 + sc_ref[0]) + sh_ref[0]).astype(BF16)


def post_moe_norm(x1, y_tiles, modblk, ln_g, ln_b, modblk_next):
    R, D = x1.shape
    row = pl.BlockSpec((ROW_TILE, D), lambda i: (i, 0))
    mod = lambda part: pl.BlockSpec((1, 1, D), lambda i: (i, 0, part))
    vec = pl.BlockSpec((1, D), lambda i: (0, 0))
    y = pl.BlockSpec((ROW_TILE * D // LANES, LANES), lambda i: (i, 0))
    return pl.pallas_call(
        _post_kernel,
        grid=(R // ROW_TILE,),
        in_specs=[row, y, mod(5), vec, vec, mod(0), mod(1)],
        out_specs=[row, row],
        out_shape=[jax.ShapeDtypeStruct((R, D), F32), jax.ShapeDtypeStruct((R, D), BF16)],
        compiler_params=_cp(("parallel",)),
        name="post_moe_norm",
    )(x1, y_tiles, modblk, ln_g.reshape(1, -1), ln_b.reshape(1, -1), modblk_next, modblk_next)


def _final_kernel(x_ref, y_ref, g2_ref, lg_ref, lb_ref, x2_ref):
    y = _load_token_tiles(y_ref, ROW_TILE, x_ref.shape[1])
    x2_ref[...] = _standardize(DEEPNORM_ALPHA * x_ref[...] + g2_ref[0] * y) * lg_ref[...] + lb_ref[...]


def final_norm(x1, y_tiles, modblk, ln_g, ln_b):
    R, D = x1.shape
    n_lat = R // S_ALL * LAT_BLOCKS
    blk = lambda i: i + i // LAT_BLOCKS
    vec = pl.BlockSpec((1, D), lambda i: (0, 0))
    return pl.pallas_call(
        _final_kernel,
        grid=(n_lat,),
        in_specs=[pl.BlockSpec((ROW_TILE, D), lambda i: (blk(i), 0)),
                  pl.BlockSpec((ROW_TILE * D // LANES, LANES), lambda i: (blk(i), 0)),
                  pl.BlockSpec((1, 1, D), lambda i: (blk(i), 0, 5)), vec, vec],
        out_specs=pl.BlockSpec((ROW_TILE, D), lambda i: (i, 0)),
        out_shape=jax.ShapeDtypeStruct((n_lat * ROW_TILE, D), F32),
        compiler_params=_cp(("parallel",)),
        name="final_norm",
    )(x1, y_tiles, modblk, ln_g.reshape(1, -1), ln_b.reshape(1, -1))


def _s5_matrices(a_re, a_im, log_dt, b_re, b_im):
    dt = jnp.exp(log_dt)[:, None]
    mag = jnp.exp(a_re * dt)
    ab_re, ab_im = mag * jnp.cos(a_im * dt), mag * jnp.sin(a_im * dt)
    den = a_re * a_re + a_im * a_im
    num_re, num_im = ab_re - 1.0, ab_im
    coef_re = (num_re * a_re + num_im * a_im) / den
    coef_im = (num_im * a_re - num_re * a_im) / den
    bb_re = coef_re[..., None] * b_re - coef_im[..., None] * b_im
    bb_im = coef_re[..., None] * b_im + coef_im[..., None] * b_re
    gs = S5_GROUPS // S5_SLABS
    eye = jnp.eye(gs, dtype=F32)

    def slabs(t):
        t = t.reshape(S5_SLABS, gs, S5_STATE, S5_GROUP_CH)
        return jnp.einsum('sgpi,gh->sgihp', t, eye).reshape(S5_SLABS, gs * S5_GROUP_CH, gs * S5_STATE)

    lam = jnp.stack([ab_re.reshape(-1), ab_im.reshape(-1)], 0)
    return lam, jnp.concatenate([slabs(bb_re), slabs(bb_im)], -1)


def _s5_readout_matrix(c_re, c_im):
    gs = S5_GROUPS // S5_SLABS
    eye = jnp.eye(gs, dtype=F32)

    def slabs(t):
        t = t.reshape(S5_SLABS, gs, S5_GROUP_CH, S5_STATE)
        return jnp.einsum('sgip,gh->sgphi', t, eye).reshape(S5_SLABS, gs * S5_STATE, gs * S5_GROUP_CH)

    return jnp.concatenate([slabs(c_re), -slabs(c_im)], 1)


def _mla_weights(w_uq, w_ukv):
    qk = MLA_NOPE_DIM + MLA_ROPE_DIM
    wq = w_uq.reshape(MLA_Q_LORA, MLA_HEADS, qk)
    wq = jnp.pad(wq, ((0, 0), (0, 0), (0, MLA_QK_PAD - qk))).reshape(MLA_Q_LORA, MLA_HEADS * MLA_QK_PAD)
    wkv = w_ukv.reshape(MLA_KV_LORA, MLA_HEADS, MLA_NOPE_DIM + MLA_V_DIM)
    wk = jnp.pad(wkv[..., :MLA_NOPE_DIM], ((0, 0), (0, 0), (0, MLA_QK_PAD - MLA_NOPE_DIM)))
    wv = wkv[..., MLA_NOPE_DIM:]
    wkv_pad = jnp.concatenate([wk.reshape(MLA_KV_LORA, -1), wv.reshape(MLA_KV_LORA, -1)], 1)
    return wq.astype(BF16), wkv_pad.astype(BF16)


def kernel(x, c, ctx, c_ctx, ada_w, ada_b, w_in, s5_a_re_f, s5_a_im_f, s5_log_dt_f, s5_a_re_b, s5_a_im_b,
           s5_log_dt_b, s5_b_re, s5_b_im, s5_c_re, s5_c_im, s5_d, s5_w_glu, gqa_q_norm, gqa_k_norm,
           ret_decay_f, ret_decay_b, ret_norm, mla_q_norm, mla_kv_norm, mla_w_uq, mla_w_ukv,
           w_branch, w_gate, b_gate, w_out, ln1_g, ln1_b, router_w, moe_w_gate, moe_w_up, moe_w_down,
           ln2_g, ln2_b):
    B, N, D = x.shape
    R = B * S_ALL
    assert (B, N, D) == (BATCH, SEQ, D_MODEL) and RET_CHUNK == CTX_LEN == ROW_TILE

    cc = jnp.zeros((16, D), F32).at[:B].set(c).at[B].set(c_ctx)
    mod = ada_modulation(cc, ada_w, ada_b)
    sel = np.concatenate([np.r_[np.full(LAT_BLOCKS, b), B] for b in range(B)]).astype(np.int32)
    modblks = [mod[l][sel].reshape(R // ROW_TILE, 1, 6 * D) for l in range(DEPTH)]

    X = jnp.concatenate([x, ctx], 1).reshape(R, D)
    h = modulate_rows(X, modblks[0], 0, 1)

    for l in range(DEPTH):
        need_ctx = l < DEPTH - 1
        modblk = modblks[l]
        w_in_p = jnp.pad(w_in[l], ((0, 0), (0, IN_PAD - IN_TOTAL))).astype(BF16)
        wq_pad, wkv_pad = _mla_weights(mla_w_uq[l], mla_w_ukv[l])
        u, gq, gk, gv, rq, rk, rv, rg, mq, mk, mv = in_proj_branches(
            h, w_in_p, gqa_q_norm[l], gqa_k_norm[l], mla_q_norm[l], mla_kv_norm[l], wq_pad, wkv_pad)
        to3 = lambda t: t.reshape(B, S_ALL, t.shape[-1])

        lam_f, bb_f = _s5_matrices(s5_a_re_f[l], s5_a_im_f[l], s5_log_dt_f[l], s5_b_re[l], s5_b_im[l])
        lam_b, bb_b = _s5_matrices(s5_a_re_b[l], s5_a_im_b[l], s5_log_dt_b[l], s5_b_re[l], s5_b_im[l])
        lam = jnp.broadcast_to(jnp.stack([lam_f, lam_b], 0)[:, :, None, :], (2, 2, S5_SEG, S5_MODES))
        bb = jnp.stack([bb_f, bb_b], 0).astype(BF16)
        cmat = _s5_readout_matrix(s5_c_re[l], s5_c_im[l]).astype(BF16)
        o_s5 = s5_output(s5_scan(u, bb, lam, cmat), u, s5_d[l], s5_w_glu[l].astype(BF16))

        gq3, gk3, gv3 = to3(gq), to3(gk), to3(gv)
        att = functools.partial(attention, kv_heads=GQA_KV_HEADS, groups=GQA_HEADS // GQA_KV_HEADS,
                                dk=GQA_HEAD_DIM, dv=GQA_HEAD_DIM)
        o_lat = att(gq3, gk3, gv3, q_rows=SEQ, q_off=0, kv_rows=S_ALL, kv_off=0, tq=512)
        no_ctx = jnp.zeros((B, CTX_LEN, BRANCH_W), BF16)
        o_ctx = (att(gq3, gk3, gv3, q_rows=CTX_LEN, q_off=SEQ, kv_rows=CTX_LEN, kv_off=SEQ, tq=256)
                 if need_ctx else no_ctx)
        o_gqa = jnp.concatenate([o_lat, o_ctx], 1).reshape(R, BRANCH_W)

        lgf = -jnp.exp(ret_decay_f[l])
        lgb = -jnp.exp(ret_decay_b[l])
        rkt = jnp.swapaxes(to3(rk), 1, 2)
        o_ret = retention(to3(rq), rkt, to3(rv), to3(rg), ret_norm[l], lgf, lgb, S_ALL).reshape(R, BRANCH_W)

        matt = functools.partial(attention, kv_heads=MLA_HEADS, groups=1, dk=MLA_QK_PAD, dv=MLA_V_DIM)
        m_lat = matt(to3(mq), to3(mk), to3(mv), q_rows=SEQ, q_off=0, kv_rows=S_ALL, kv_off=0, tq=1024)
        m_ctx = (matt(to3(mq), to3(mk), to3(mv), q_rows=CTX_LEN, q_off=SEQ, kv_rows=CTX_LEN, kv_off=SEQ, tq=256)
                 if need_ctx else no_ctx)
        o_mla = jnp.concatenate([m_lat, m_ctx], 1).reshape(R, BRANCH_W)

        merged = merge_branches(h, (o_s5, o_gqa, o_ret, o_mla), w_gate, l, b_gate[l], w_branch[l].astype(BF16))
        router_pad = jnp.pad(router_w[l], ((0, 0), (0, 128 - N_EXPERTS)))
        x1, h2, logits = out_proj_norm(merged, X, w_out[l].astype(BF16), modblk, ln1_g[l], ln1_b[l], router_pad)

        logits_t = jnp.swapaxes(logits.reshape(B, S_ALL, 128)[:, :, :N_EXPERTS], 1, 2)
        sets = [(0, SEQ)] + ([(SEQ, CTX_LEN)] if need_ctx else [])
        idx_sets, row_parts, gate_parts = [], [], []
        sample_row0 = (jnp.arange(B, dtype=jnp.int32) * S_ALL)[:, None, None]
        for off, n in sets:
            idx_f, gate_f = route(logits_t, off, n)
            idx = jnp.swapaxes(idx_f[:, :, :N_EXPERTS], 1, 2).astype(jnp.int32)
            gate = jnp.swapaxes(gate_f[:, :, :N_EXPERTS], 1, 2)
            idx_sets.append(idx)
            row_parts.append(jnp.swapaxes(idx + sample_row0 + off, 0, 1).reshape(N_EXPERTS, -1))
            gate_parts.append(jnp.swapaxes(gate, 0, 1).reshape(N_EXPERTS, -1))
        rows = jnp.concatenate(row_parts, 1)[:, None, :]
        gates = jnp.concatenate(gate_parts, 1)[:, :, None]
        xs = gather_rows(rows, h2, D)
        ys = expert_ffn(xs, gates, moe_w_gate, moe_w_up, moe_w_down, l)
        moe = combine(ys, idx_sets, [off for off, _ in sets]).reshape(R * D // LANES, LANES)
        if l == DEPTH - 1:
            return final_norm(x1, moe, modblk, ln2_g[l], ln2_b[l]).reshape(B, SEQ, D)
        X, h = post_moe_norm(x1, moe, modblk, ln2_g[l], ln2_b[l], modblks[l + 1])
```

```python
import functools
import math

import numpy as np
import jax
import jax.numpy as jnp
from jax import lax
from jax.experimental import pallas as pl
from jax.experimental.pallas import tpu as pltpu

F32 = jnp.float32
BF16 = jnp.bfloat16

D_MODEL = 2048
BATCH = 2
SEQ = 4096
DEPTH = 2
GRID_W = 64
CTX_LEN = 256
S_ALL = SEQ + CTX_LEN
N_BRANCH = 4
BRANCH_W = D_MODEL // 4
S5_GROUP_CH = 16
S5_GROUPS = BRANCH_W // S5_GROUP_CH
S5_STATE = 64
S5_MODES = S5_GROUPS * S5_STATE
GQA_HEAD_DIM = 128
GQA_HEADS = 4
GQA_KV_HEADS = 2
RET_HEADS = 4
RET_V_DIM = 128
RET_QK_DIM = 64
MLA_HEADS = 4
MLA_Q_LORA = 512
MLA_KV_LORA = 256
MLA_NOPE_DIM = 128
MLA_ROPE_DIM = 64
MLA_V_DIM = 128
MLA_QK_PAD = 256
N_EXPERTS = 16
EXPERT_FF = D_MODEL // 2
EC_CAPACITY_FACTOR = 2
ROPE_BASE = 10000.0
NORM_EPS = 1e-6
LOG2E = math.log2(math.e)
DEEPNORM_ALPHA = (2 * DEPTH) ** 0.25
IN_WIDTHS = (512, 512, 256, 256, 256, 256, 512, 512, 512, 256, 64)
IN_TOTAL = sum(IN_WIDTHS)
IN_PAD = 4096

LANES = 128
SUBLANES = 8
ROW_TILE = 256
MM_ROW_TILE = 512
MERGE_ROW_TILE = 1088
ATTN_SUB = 256
BLOCKS_PER_SAMPLE = S_ALL // ROW_TILE
LAT_BLOCKS = SEQ // ROW_TILE
S5_SLABS = 4
S5_SEG = 8
S5_STEPS = ROW_TILE // S5_SEG
RET_CHUNK = 256
VMEM_LIMIT = 56 * 1024 * 1024


def _cp(sem, vmem=VMEM_LIMIT):
    return pltpu.CompilerParams(dimension_semantics=sem, vmem_limit_bytes=vmem)


def _standardize(x):
    xc = x - jnp.mean(x, -1, keepdims=True)
    return xc * lax.rsqrt(jnp.mean(xc * xc, -1, keepdims=True) + NORM_EPS)


def _sigmoid(x):
    return 1.0 / (1.0 + jnp.exp(-x))


def _store_token_tiles(ref, x, row0=0):
    rows, w = x.shape
    tt = w // LANES
    for j in range(tt):
        ref[pl.ds(row0 * tt + j, rows, stride=tt), :] = x[:, j * LANES:(j + 1) * LANES]


def _load_token_tiles(ref, rows, w, dtype=F32):
    tt = w // LANES
    return jnp.concatenate([ref[pl.ds(j, rows, stride=tt), :].astype(dtype) for j in range(tt)], -1)


def _ada_kernel(c_ref, w_ref, b_ref, o_ref):
    c = c_ref[...]
    cs = (c * _sigmoid(c)).astype(BF16)
    o_ref[0] = jnp.dot(cs, w_ref[0].astype(BF16), preferred_element_type=F32) + b_ref[0]


def ada_modulation(cc, ada_w, ada_b, tn=1024):
    L, D, N = ada_w.shape
    return pl.pallas_call(
        _ada_kernel,
        grid=(L, N // tn),
        in_specs=[pl.BlockSpec((16, D), lambda l, n: (0, 0)),
                  pl.BlockSpec((1, D, tn), lambda l, n: (l, 0, n)),
                  pl.BlockSpec((1, 1, tn), lambda l, n: (l, 0, n))],
        out_specs=pl.BlockSpec((1, 16, tn), lambda l, n: (l, 0, n)),
        out_shape=jax.ShapeDtypeStruct((L, 16, N), F32),
        compiler_params=_cp(("arbitrary", "arbitrary")),
        name="ada_modulation",
    )(cc, ada_w, ada_b.reshape(L, 1, N))


def _modulate_kernel(x_ref, sh_ref, sc_ref, o_ref):
    o_ref[...] = (_standardize(x_ref[...]) * (1.0 + sc_ref[0]) + sh_ref[0]).astype(o_ref.dtype)


def modulate_rows(x, modblk, shift_part, scale_part):
    R, D = x.shape
    return pl.pallas_call(
        _modulate_kernel,
        grid=(R // ROW_TILE,),
        in_specs=[pl.BlockSpec((ROW_TILE, D), lambda i: (i, 0)),
                  pl.BlockSpec((1, 1, D), lambda i: (i, 0, shift_part)),
                  pl.BlockSpec((1, 1, D), lambda i: (i, 0, scale_part))],
        out_specs=pl.BlockSpec((ROW_TILE, D), lambda i: (i, 0)),
        out_shape=jax.ShapeDtypeStruct((R, D), BF16),
        compiler_params=_cp(("parallel",)),
        name="modulate_rows",
    )(x, modblk, modblk)


def _rope_tables(head_dim, width):
    half = head_dim // 2
    n = half // 2
    inv = ROPE_BASE ** (-np.arange(n, dtype=np.float64) / n)
    t = np.arange(SEQ)
    pos = np.stack([t // GRID_W, t % GRID_W], 0).astype(np.float64)
    lane = np.arange(head_dim)
    which = lane // half
    m = lane % half
    ang = (pos[which, :].T.astype(np.float32) * inv[m % n].astype(np.float32)[None, :]).astype(np.float64)
    cos = np.cos(ang)
    sin = np.where(m < n, -np.sin(ang), np.sin(ang))
    cos = np.concatenate([cos, np.ones((CTX_LEN, head_dim))], 0)
    sin = np.concatenate([sin, np.zeros((CTX_LEN, head_dim))], 0)
    reps = width // head_dim
    return (np.tile(cos, (1, reps)).astype(np.float32), np.tile(sin, (1, reps)).astype(np.float32))


def _mla_q_tables():
    cos64, sin64 = _rope_tables(MLA_ROPE_DIM, MLA_ROPE_DIM)
    ones = np.ones((S_ALL, MLA_NOPE_DIM), np.float32)
    zeros = np.zeros((S_ALL, MLA_NOPE_DIM), np.float32)
    pad1 = np.ones((S_ALL, MLA_QK_PAD - MLA_NOPE_DIM - MLA_ROPE_DIM), np.float32)
    cos = np.concatenate([ones, cos64, pad1], 1)
    sin = np.concatenate([zeros, sin64, 0 * pad1], 1)
    return cos, sin


def _rope(x, cos, sin, quarter):
    w = x.shape[-1]
    lane = lax.broadcasted_iota(jnp.int32, x.shape, 1)
    first = (lane % (2 * quarter)) < quarter
    partner = jnp.where(first, pltpu.roll(x, w - quarter, 1), pltpu.roll(x, quarter, 1))
    return x * cos + partner * sin


def _rms_heads(x, gain, head_dim):
    outs = []
    for h in range(x.shape[-1] // head_dim):
        xh = x[:, h * head_dim:(h + 1) * head_dim]
        outs.append(xh * lax.rsqrt(jnp.mean(xh * xh, -1, keepdims=True) + NORM_EPS) * gain)
    return outs[0] if len(outs) == 1 else jnp.concatenate(outs, -1)


PROJ_TILE = 1024


def _tab(ref_a, ref_b):
    return jnp.concatenate([ref_a[...], ref_b[...]], 0)


def _proj_s5_gq_kernel(h_ref, w_ref, ca, cb, sa, sb, gqn_ref, u_ref, gq_ref):
    p = jnp.dot(h_ref[...], w_ref[...], preferred_element_type=F32)
    u_ref[...] = p[:, :512]
    c128, s128 = _tab(ca, cb), _tab(sa, sb)
    q = _rms_heads(p[:, 512:], gqn_ref[...], GQA_HEAD_DIM)
    q = _rope(q, jnp.concatenate([c128] * 4, -1), jnp.concatenate([s128] * 4, -1), GQA_HEAD_DIM // 4)
    gq_ref[...] = (q * (GQA_HEAD_DIM ** -0.5 * LOG2E)).astype(BF16)


def _proj_kv_ret_kernel(h_ref, w_ref, ca, cb, sa, sb, c6a, c6b, s6a, s6b, gkn_ref,
                        gk_ref, gv_ref, rq_ref, rk_ref):
    p = jnp.dot(h_ref[...], w_ref[...], preferred_element_type=F32)
    tile2 = lambda t: jnp.concatenate([t, t], -1)
    c128, s128, c64, s64 = _tab(ca, cb), _tab(sa, sb), _tab(c6a, c6b), _tab(s6a, s6b)
    k = _rms_heads(p[:, :256], gkn_ref[...], GQA_HEAD_DIM)
    gk_ref[...] = _rope(k, tile2(c128), tile2(s128), GQA_HEAD_DIM // 4).astype(BF16)
    gv_ref[...] = p[:, 256:512].astype(BF16)
    rq_ref[...] = _rope(p[:, 512:768], tile2(c64), tile2(s64), RET_QK_DIM // 4).astype(BF16)
    rk = _rope(p[:, 768:], tile2(c64), tile2(s64), RET_QK_DIM // 4)
    rk_ref[...] = (rk * (RET_QK_DIM ** -0.5)).astype(BF16)


def _proj_ret_vg_kernel(h_ref, w_ref, rv_ref, rg_ref):
    p = jnp.dot(h_ref[...], w_ref[...], preferred_element_type=F32)
    rv_ref[...] = p[:, :512].astype(BF16)
    rg_ref[...] = p[:, 512:]


def _proj_mla_kernel(h_ref, w_ref, c6a, c6b, s6a, s6b, cqa, cqb, sqa, sqb, mqn_ref, mkvn_ref, wq_ref, wkv_ref,
                     q_ref, k_ref, v_ref):
    p = jnp.dot(h_ref[...], w_ref[...], preferred_element_type=F32)
    tm = p.shape[0]
    cqn = _rms_heads(p[:, :512], mqn_ref[...], MLA_Q_LORA).astype(BF16)
    ckvn = _rms_heads(p[:, 512:768], mkvn_ref[...], MLA_KV_LORA).astype(BF16)
    lane = lax.broadcasted_iota(jnp.int32, (tm, 128), 1)
    kr = jnp.where(lane < MLA_ROPE_DIM, _rope(p[:, 768:896], _tab(c6a, c6b), _tab(s6a, s6b), MLA_ROPE_DIM // 4), 0.0)
    scale = (MLA_NOPE_DIM + MLA_ROPE_DIM) ** -0.5 * LOG2E
    q = jnp.dot(cqn, wq_ref[...], preferred_element_type=F32)
    kv = jnp.dot(ckvn, wkv_ref[...], preferred_element_type=F32)
    cos, sin = _tab(cqa, cqb), _tab(sqa, sqb)
    krp = jnp.concatenate([jnp.zeros((tm, MLA_NOPE_DIM), F32), kr], -1)
    for h in range(MLA_HEADS):
        sl = slice(h * MLA_QK_PAD, (h + 1) * MLA_QK_PAD)
        q_ref[:, sl] = (_rope(q[:, sl], cos, sin, MLA_ROPE_DIM // 4) * scale).astype(BF16)
        k_ref[:, sl] = (kv[:, sl] + krp).astype(BF16)
    v_ref[...] = kv[:, MLA_HEADS * MLA_QK_PAD:].astype(BF16)


def in_proj_branches(h, w_in_p, gqa_q_norm, gqa_k_norm, mla_q_norm, mla_kv_norm, wq_pad, wkv_pad,
                     tm=MM_ROW_TILE):
    R, D = h.shape
    half = tm // 2
    assert half == ROW_TILE
    c128, s128 = (jnp.asarray(t) for t in _rope_tables(GQA_HEAD_DIM, 128))
    c64, s64 = (jnp.asarray(t) for t in _rope_tables(RET_QK_DIM, 128))
    cq, sq = (jnp.asarray(t) for t in _mla_q_tables())
    x_spec = pl.BlockSpec((tm, D), lambda i: (i, 0))
    w_spec = lambda n: pl.BlockSpec((D, PROJ_TILE), lambda i: (0, n))
    row = lambda w: pl.BlockSpec((tm, w), lambda i: (i, 0))
    vec = lambda w: pl.BlockSpec((1, w), lambda i: (0, 0))
    full = lambda a: pl.BlockSpec(a.shape, lambda i: (0, 0))

    def tabs(t):
        w = t.shape[1]
        return ([pl.BlockSpec((half, w), lambda i: ((2 * i) % BLOCKS_PER_SAMPLE, 0)),
                 pl.BlockSpec((half, w), lambda i: ((2 * i + 1) % BLOCKS_PER_SAMPLE, 0))], [t, t])

    def call(kernel, n, extra_specs, extra_args, outs, name):
        return pl.pallas_call(
            kernel, grid=(R // tm,),
            in_specs=[x_spec, w_spec(n)] + extra_specs,
            out_specs=[row(w) for w, _ in outs],
            out_shape=[jax.ShapeDtypeStruct((R, w), dt) for w, dt in outs],
            compiler_params=_cp(("parallel",)), name=name,
        )(h, w_in_p, *extra_args)

    def gather_tabs(*ts):
        specs, args = [], []
        for t in ts:
            s, a = tabs(t)
            specs += s
            args += a
        return specs, args

    s0, a0 = gather_tabs(c128, s128)
    u, gq = call(_proj_s5_gq_kernel, 0, s0 + [vec(128)], a0 + [gqa_q_norm.reshape(1, -1)],
                 [(512, F32), (512, BF16)], "proj_s5_gq")
    s1, a1 = gather_tabs(c128, s128, c64, s64)
    gk, gv, rq, rk = call(_proj_kv_ret_kernel, 1, s1 + [vec(128)], a1 + [gqa_k_norm.reshape(1, -1)],
                          [(256, BF16)] * 4, "proj_kv_ret")
    rv, rg = call(_proj_ret_vg_kernel, 2, [], [], [(512, BF16), (512, F32)], "proj_ret_vg")
    s3, a3 = gather_tabs(c64, s64, cq, sq)
    hq = MLA_HEADS * MLA_QK_PAD
    mq, mk, mv = call(_proj_mla_kernel, 3, s3 + [vec(512), vec(256), full(wq_pad), full(wkv_pad)],
                      a3 + [mla_q_norm.reshape(1, -1), mla_kv_norm.reshape(1, -1), wq_pad, wkv_pad],
                      [(hq, BF16), (hq, BF16), (MLA_HEADS * MLA_V_DIM, BF16)], "proj_mla")
    return u, gq, gk, gv, rq, rk, rv, rg, mq, mk, mv


def _attn_kernel(q_ref, k_ref, v_ref, o_ref, *, groups, dk, dv):
    k = k_ref[0]
    v = v_ref[0]
    v1 = jnp.concatenate([v, jnp.ones_like(v)], -1)
    tq = q_ref.shape[1]
    q = jnp.concatenate([q_ref[0, :, g * dk:(g + 1) * dk] for g in range(groups)], 0)
    s = lax.dot_general(q, k, (((1,), (1,)), ((), ())), preferred_element_type=F32)
    for g in range(groups):
        for half in range(tq // ATTN_SUB):
            r0 = g * tq + half * ATTN_SUB
            sh = s[r0:r0 + ATTN_SUB]
            p = jnp.exp2(sh - jnp.max(sh, -1, keepdims=True)).astype(BF16)
            o = jnp.dot(p, v1, preferred_element_type=F32)
            rows = slice(half * ATTN_SUB, (half + 1) * ATTN_SUB)
            o_ref[0, rows, g * dv:(g + 1) * dv] = (o[:, :dv] / o[:, dv:]).astype(o_ref.dtype)


def attention(q, k, v, *, kv_heads, groups, dk, dv, q_rows, q_off, kv_rows, kv_off, tq):
    B = q.shape[0]
    assert tq % ATTN_SUB == 0 and q_off % tq == 0 and q_rows % tq == 0
    qb0, kb0 = q_off // tq, kv_off // kv_rows
    return pl.pallas_call(
        functools.partial(_attn_kernel, groups=groups, dk=dk, dv=dv),
        grid=(B, kv_heads, q_rows // tq),
        in_specs=[pl.BlockSpec((1, tq, groups * dk), lambda b, h, i: (b, qb0 + i, h)),
                  pl.BlockSpec((1, kv_rows, dk), lambda b, h, i: (b, kb0, h)),
                  pl.BlockSpec((1, kv_rows, dv), lambda b, h, i: (b, kb0, h))],
        out_specs=pl.BlockSpec((1, tq, groups * dv), lambda b, h, i: (b, i, h)),
        out_shape=jax.ShapeDtypeStruct((B, q_rows, kv_heads * groups * dv), BF16),
        compiler_params=_cp(("parallel", "parallel", "arbitrary")),
        name="attention",
    )(q, k, v)


def _ret_kernel(lgf_ref, lgb_ref, q_ref, kt_ref, v_ref, g_ref, gain_ref, o_ref, sb_ref, *, n_out):
    L = RET_CHUNK
    nc = SEQ // L
    pair = pl.program_id(1)
    r_i = lax.broadcasted_iota(jnp.int32, (L, L), 0)
    c_i = lax.broadcasted_iota(jnp.int32, (L, L), 1)
    diff = (r_i - c_i).astype(F32)
    pos_col = lax.broadcasted_iota(jnp.int32, (L, 1), 0).astype(F32)
    pos_row = lax.broadcasted_iota(jnp.int32, (1, L), 1).astype(F32)
    for j in range(2):
        lgf = lgf_ref[pair * 2 + j]
        lgb = lgb_ref[pair * 2 + j]
        dmat = jnp.where(diff >= 0, jnp.exp(lgf * jnp.maximum(diff, 0.0)), jnp.exp(lgb * jnp.maximum(-diff, 0.0)))
        dq_f = jnp.exp(lgf * (pos_col + 1.0))
        dq_b = jnp.exp(lgb * (L - pos_col))
        dk_f = jnp.exp(lgf * (L - 1.0 - pos_row))
        dk_b = jnp.exp(lgb * pos_row)
        dc_f = jnp.exp(lgf * L)
        dc_b = jnp.exp(lgb * L)
        qs = slice(j * RET_QK_DIM, (j + 1) * RET_QK_DIM)
        vs = slice(j * RET_V_DIM, (j + 1) * RET_V_DIM)

        def chunk(c):
            rows = slice(c * L, (c + 1) * L)
            return q_ref[0, rows, qs], kt_ref[0, qs, rows], v_ref[0, rows, vs]

        def readout(o, c, out_row0):
            oc = o - jnp.mean(o, -1, keepdims=True)
            on = oc * lax.rsqrt(jnp.mean(oc * oc, -1, keepdims=True) + NORM_EPS)
            g = g_ref[0, c * L:(c + 1) * L, vs]
            o_ref[0, out_row0:out_row0 + L, vs] = (on * gain_ref[:, vs] * (g * _sigmoid(g))).astype(o_ref.dtype)

        def intra(q, kt, v):
            sc = jnp.dot(q, kt, preferred_element_type=F32) * dmat
            return jnp.dot(sc.astype(BF16), v, preferred_element_type=F32)

        def state_add(kt, dk, v):
            return jnp.dot((kt.astype(F32) * dk).astype(BF16), v, preferred_element_type=F32)

        qc, ktc, vc = chunk(nc)
        if n_out > SEQ:
            readout(intra(qc, ktc, vc), nc, SEQ)
        s_f = state_add(ktc, dk_f, vc)
        s_b = state_add(ktc, dk_b, vc)
        for c in range(nc - 1, -1, -1):
            sb_ref[c] = s_b
            _, kt, v = chunk(c)
            s_b = s_b * dc_b + state_add(kt, dk_b, v)
        for c in range(nc):
            q, kt, v = chunk(c)
            qf = q.astype(F32)
            o = (intra(q, kt, v)
                 + jnp.dot((qf * dq_f).astype(BF16), s_f.astype(BF16), preferred_element_type=F32)
                 + jnp.dot((qf * dq_b).astype(BF16), sb_ref[c].astype(BF16), preferred_element_type=F32))
            readout(o, c, c * L)
            s_f = s_f * dc_f + state_add(kt, dk_f, v)


def retention(rq, rkt, rv, rg, gain, lgf, lgb, n_out):
    B = rq.shape[0]
    smem = pl.BlockSpec(memory_space=pltpu.SMEM)
    return pl.pallas_call(
        functools.partial(_ret_kernel, n_out=n_out),
        grid=(B, RET_HEADS // 2),
        in_specs=[smem, smem,
                  pl.BlockSpec((1, S_ALL, 128), lambda b, p: (b, 0, p)),
                  pl.BlockSpec((1, 128, S_ALL), lambda b, p: (b, p, 0)),
                  pl.BlockSpec((1, S_ALL, 256), lambda b, p: (b, 0, p)),
                  pl.BlockSpec((1, S_ALL, 256), lambda b, p: (b, 0, p)),
                  pl.BlockSpec((1, 256), lambda b, p: (0, p))],
        out_specs=pl.BlockSpec((1, n_out, 256), lambda b, p: (b, 0, p)),
        out_shape=jax.ShapeDtypeStruct((B, n_out, RET_HEADS * RET_V_DIM), BF16),
        scratch_shapes=[pltpu.VMEM((SEQ // RET_CHUNK, RET_QK_DIM, RET_V_DIM), F32)],
        compiler_params=_cp(("parallel", "parallel")),
        name="retention",
    )(lgf, lgb, rq, rkt, rv, rg, gain.reshape(1, -1))


def _s5_kernel(u_ref, bb_ref, lam_ref, cm_ref, y_ref, bu_ref, pw_ref, st_ref):
    M = S5_MODES
    chunk = pl.program_id(2)
    lam_re = lam_ref[0, 0]
    lam_im = lam_ref[0, 1]

    @pl.when(chunk == 0)
    def _():
        st_ref[...] = jnp.zeros_like(st_ref)
        p_re, p_im = lam_re, lam_im
        for j in range(S5_STEPS):
            pw_ref[0, j] = p_re
            pw_ref[1, j] = p_im
            p_re, p_im = p_re * lam_re - p_im * lam_im, p_re * lam_im + p_im * lam_re

    r_i = lax.broadcasted_iota(jnp.int32, (ROW_TILE, ROW_TILE), 0)
    c_i = lax.broadcasted_iota(jnp.int32, (ROW_TILE, ROW_TILE), 1)
    flip = pl.program_id(0) == 1

    def scan_time(r):
        t = (r % S5_SEG) * S5_STEPS + r // S5_SEG
        return jnp.where(flip, ROW_TILE - 1 - t, t)

    to_scan = (c_i == scan_time(r_i)).astype(BF16)
    to_time = (r_i == scan_time(c_i)).astype(BF16)
    u = jnp.dot(to_scan, u_ref[...].astype(BF16), preferred_element_type=F32).astype(BF16)

    slab = BRANCH_W // S5_SLABS
    ms = M // S5_SLABS
    for s in range(S5_SLABS):
        part = jnp.dot(u[:, s * slab:(s + 1) * slab], bb_ref[0, s], preferred_element_type=F32)
        bu_ref[:, s * ms:(s + 1) * ms] = part[:, :ms]
        bu_ref[:, M + s * ms:M + (s + 1) * ms] = part[:, ms:]

    tile = 512
    for t in range(M // tile):
        re_sl = slice(t * tile, (t + 1) * tile)
        im_sl = slice(M + t * tile, M + (t + 1) * tile)
        lr, li = lam_re[:, re_sl], lam_im[:, re_sl]

        def step(j, carry):
            s_re, s_im = carry
            rows = pl.ds(pl.multiple_of(j * S5_SEG, S5_SEG), S5_SEG)
            n_re = lr * s_re - li * s_im + bu_ref[rows, re_sl]
            n_im = lr * s_im + li * s_re + bu_ref[rows, im_sl]
            bu_ref[rows, re_sl] = n_re
            bu_ref[rows, im_sl] = n_im
            return n_re, n_im

        z = jnp.zeros((S5_SEG, tile), F32)
        lax.fori_loop(0, S5_STEPS, step, (z, z))

    last = slice((S5_STEPS - 1) * S5_SEG, S5_STEPS * S5_SEG)
    e_re, e_im = bu_ref[last, 0:M], bu_ref[last, M:2 * M]
    pl_re, pl_im = pw_ref[0, S5_STEPS - 1][0:1], pw_ref[1, S5_STEPS - 1][0:1]
    c_re, c_im = st_ref[0:1, :], st_ref[1:2, :]
    rows_re, rows_im = [], []
    for k in range(S5_SEG):
        rows_re.append(c_re)
        rows_im.append(c_im)
        c_re, c_im = (e_re[k:k + 1] + pl_re * c_re - pl_im * c_im,
                      e_im[k:k + 1] + pl_re * c_im + pl_im * c_re)
    st_ref[0:1, :] = c_re
    st_ref[1:2, :] = c_im
    car_re = jnp.concatenate(rows_re, 0)
    car_im = jnp.concatenate(rows_im, 0)

    def fix(j, _):
        rows = pl.ds(pl.multiple_of(j * S5_SEG, S5_SEG), S5_SEG)
        p_re, p_im = pw_ref[0, j], pw_ref[1, j]
        bu_ref[rows, 0:M] = bu_ref[rows, 0:M] + p_re * car_re - p_im * car_im
        bu_ref[rows, M:2 * M] = bu_ref[rows, M:2 * M] + p_re * car_im + p_im * car_re
        return 0

    lax.fori_loop(0, S5_STEPS, fix, 0)

    ys = []
    for s in range(S5_SLABS):
        hs = jnp.concatenate([bu_ref[:, s * ms:(s + 1) * ms], bu_ref[:, M + s * ms:M + (s + 1) * ms]], -1)
        ys.append(jnp.dot(hs.astype(BF16), cm_ref[s], preferred_element_type=F32))
    y = jnp.concatenate(ys, -1)
    out = None
    for _ in range(2):
        piece = y.astype(BF16)
        y = y - piece.astype(F32)
        term = jnp.dot(to_time, piece, preferred_element_type=F32)
        out = term if out is None else out + term
    y_ref[0] = out


def _s5_block(d, c):
    fwd = (c + LAT_BLOCKS) % BLOCKS_PER_SAMPLE
    bwd = jnp.where(c == 0, LAT_BLOCKS, LAT_BLOCKS - c)
    return jnp.where(d == 0, fwd, bwd)


def s5_scan(u, bb, lam, cmat):
    R = u.shape[0]
    B = R // S_ALL
    W = BRANCH_W
    row_block = lambda d, b, c: b * BLOCKS_PER_SAMPLE + _s5_block(d, c)
    return pl.pallas_call(
        _s5_kernel,
        grid=(2, B, BLOCKS_PER_SAMPLE),
        in_specs=[pl.BlockSpec((ROW_TILE, W), lambda d, b, c: (row_block(d, b, c), 0)),
                  pl.BlockSpec((1,) + bb.shape[1:], lambda d, b, c: (d, 0, 0, 0)),
                  pl.BlockSpec((1, 2, S5_SEG, S5_MODES), lambda d, b, c: (d, 0, 0, 0)),
                  pl.BlockSpec(cmat.shape, lambda d, b, c: (0, 0, 0))],
        out_specs=pl.BlockSpec((1, ROW_TILE, W), lambda d, b, c: (d, row_block(d, b, c), 0)),
        out_shape=jax.ShapeDtypeStruct((2, R, W), F32),
        scratch_shapes=[pltpu.VMEM((ROW_TILE, 2 * S5_MODES), F32),
                        pltpu.VMEM((2, S5_STEPS, S5_SEG, S5_MODES), F32),
                        pltpu.VMEM((8, S5_MODES), F32)],
        compiler_params=_cp(("arbitrary", "arbitrary", "arbitrary")),
        name="s5_scan",
    )(u, bb, lam, cmat)


def _s5_out_kernel(yf_ref, yb_ref, p_ref, d_ref, w_ref, o_ref):
    y = yf_ref[0] + yb_ref[0] + d_ref[...] * p_ref[...]
    z = 0.5 * y * (1.0 + jnp.tanh(math.sqrt(2.0 / math.pi) * (y + 0.044715 * (y * y * y))))
    gate = _sigmoid(jnp.dot(z.astype(BF16), w_ref[...], preferred_element_type=F32))
    o_ref[...] = (z * gate).astype(o_ref.dtype)


def s5_output(y_dirs, u, d, w_glu):
    R = y_dirs.shape[1]
    return pl.pallas_call(
        _s5_out_kernel,
        grid=(R // ROW_TILE,),
        in_specs=[pl.BlockSpec((1, ROW_TILE, 512), lambda i: (0, i, 0)),
                  pl.BlockSpec((1, ROW_TILE, 512), lambda i: (1, i, 0)),
                  pl.BlockSpec((ROW_TILE, 512), lambda i: (i, 0)),
                  pl.BlockSpec((1, 512), lambda i: (0, 0)),
                  pl.BlockSpec((512, 512), lambda i: (0, 0))],
        out_specs=pl.BlockSpec((ROW_TILE, 512), lambda i: (i, 0)),
        out_shape=jax.ShapeDtypeStruct((R, 512), BF16),
        compiler_params=_cp(("parallel",)),
        name="s5_output",
    )(y_dirs, y_dirs, u, d.reshape(1, -1), w_glu)


def _merge_kernel(h_ref, o0_ref, o1_ref, o2_ref, o3_ref, g0_ref, g1_ref, g2_ref, g3_ref,
                  bg_ref, wb_ref, out_ref, wg_ref):
    @pl.when(pl.program_id(1) == 0)
    def _():
        for k, g in enumerate((g0_ref, g1_ref, g2_ref, g3_ref)):
            wg_ref[k] = g[0].astype(BF16)

    h = h_ref[...]
    acc = None
    for k, o in enumerate((o0_ref, o1_ref, o2_ref, o3_ref)):
        gate = _sigmoid(jnp.dot(h, wg_ref[k], preferred_element_type=F32) + bg_ref[k])
        term = gate * jnp.dot(o[...], wb_ref[k], preferred_element_type=F32)
        acc = term if acc is None else acc + term
    out_ref[...] = acc.astype(out_ref.dtype)


def merge_branches(h, outs, w_gate_all, layer, b_gate, w_branch, tn=256, tm=MERGE_ROW_TILE):
    R, D = h.shape
    nb = D // tn
    gate_spec = lambda k: pl.BlockSpec((1, D, tn), lambda n, m: (layer, 0, k * nb + n))
    bg = b_gate.reshape(N_BRANCH, 1, D)
    return pl.pallas_call(
        _merge_kernel,
        grid=(nb, R // tm),
        in_specs=[pl.BlockSpec((tm, D), lambda n, m: (m, 0))]
                 + [pl.BlockSpec((tm, BRANCH_W), lambda n, m: (m, 0))] * N_BRANCH
                 + [gate_spec(k) for k in range(N_BRANCH)]
                 + [pl.BlockSpec((N_BRANCH, 1, tn), lambda n, m: (0, 0, n)),
                    pl.BlockSpec((N_BRANCH, BRANCH_W, tn), lambda n, m: (0, 0, n))],
        out_specs=pl.BlockSpec((tm, tn), lambda n, m: (m, n)),
        out_shape=jax.ShapeDtypeStruct((R, D), BF16),
        scratch_shapes=[pltpu.VMEM((N_BRANCH, D, tn), BF16)],
        compiler_params=_cp(("arbitrary", "arbitrary")),
        name="merge_branches",
    )(h, *outs, w_gate_all, w_gate_all, w_gate_all, w_gate_all, bg, w_branch)


def _out_kernel(m_ref, x_ref, w_ref, g1_ref, lg_ref, lb_ref, sh_ref, sc_ref, rw_ref,
                x1_ref, h2_ref, lo_ref):
    y = jnp.dot(m_ref[...], w_ref[...], preferred_element_type=F32)
    rw = rw_ref[...]
    rw_hi = rw.astype(BF16)
    rw_lo = (rw - rw_hi.astype(F32)).astype(BF16)
    tt = y.shape[1] // LANES
    for j in range(MM_ROW_TILE // ROW_TILE):
        rows = slice(j * ROW_TILE, (j + 1) * ROW_TILE)
        x1 = (_standardize(DEEPNORM_ALPHA * x_ref[rows, :] + g1_ref[0, j:j + 1] * y[rows]) * lg_ref[...]
              + lb_ref[...])
        x1_ref[rows, :] = x1
        h2 = _standardize(x1) * (1.0 + sc_ref[0, j:j + 1]) + sh_ref[0, j:j + 1]
        for f in range(tt):
            h2_ref[pl.ds(j * ROW_TILE * tt + f, ROW_TILE, stride=tt), :] = h2[:, f * LANES:(f + 1) * LANES]
        h_hi = h2.astype(BF16)
        h_lo = (h2 - h_hi.astype(F32)).astype(BF16)
        lo_ref[rows, :] = (jnp.dot(h_hi, rw_hi, preferred_element_type=F32)
                           + jnp.dot(h_lo, rw_hi, preferred_element_type=F32)
                           + jnp.dot(h_hi, rw_lo, preferred_element_type=F32))


def out_proj_norm(merged, x, w_out, modblk, ln_g, ln_b, router_pad):
    R, D = x.shape
    tt = D // LANES
    tm = MM_ROW_TILE
    per = tm // ROW_TILE
    mod2 = modblk.reshape(R // tm, per, modblk.shape[-1])
    row = lambda w: pl.BlockSpec((tm, w), lambda i: (i, 0))
    mod = lambda part: pl.BlockSpec((1, per, D), lambda i: (i, 0, part))
    vec = pl.BlockSpec((1, D), lambda i: (0, 0))
    return pl.pallas_call(
        _out_kernel,
        grid=(R // tm,),
        in_specs=[row(D), row(D), pl.BlockSpec((D, D), lambda i: (0, 0)), mod(2), vec, vec, mod(3), mod(4),
                  pl.BlockSpec((D, 128), lambda i: (0, 0))],
        out_specs=[row(D), pl.BlockSpec((tm * tt, LANES), lambda i: (i, 0)), row(128)],
        out_shape=[jax.ShapeDtypeStruct((R, D), F32), jax.ShapeDtypeStruct((R * tt, LANES), F32),
                   jax.ShapeDtypeStruct((R, 128), F32)],
        compiler_params=_cp(("parallel",)),
        name="out_proj_norm",
    )(merged, x, w_out, mod2, ln_g.reshape(1, -1), ln_b.reshape(1, -1), mod2, mod2, router_pad)


ROUTE_GEOMETRIC_STEPS = 40
ROUTE_BISECT_STEPS = ROUTE_GEOMETRIC_STEPS + 8


def _route_kernel(lt_ref, idx_ref, gate_ref, slot_ref, aff_ref, *, n, cap):
    E = N_EXPERTS
    lt = lt_ref[0]
    ex = jnp.exp(lt - jnp.max(lt, 0, keepdims=True))
    aff = ex / jnp.sum(ex, 0, keepdims=True)

    def count_ge(v):
        return jnp.sum((aff >= v).astype(F32), 1, keepdims=True)

    tiny = jnp.full((E, 1), float(np.finfo(np.float32).tiny), F32)
    normal = count_ge(tiny) >= cap
    lo0 = jnp.where(normal, tiny, 0.0)
    hi0 = jnp.where(normal, 2.0, tiny)

    def bisect(i, lo_hi):
        lo, hi = lo_hi
        geo = jnp.clip(jnp.sqrt(lo) * jnp.sqrt(hi), lo, hi)
        mid = jnp.where(jnp.logical_and(i < ROUTE_GEOMETRIC_STEPS, lo > 0.0), geo, lo + 0.5 * (hi - lo))
        ok = count_ge(mid) >= cap
        return jnp.where(ok, mid, lo), jnp.where(ok, hi, mid)

    lo, _ = lax.fori_loop(0, ROUTE_BISECT_STEPS, bisect, (lo0, hi0))
    thr = jnp.min(jnp.where(aff >= lo, aff, 2.0), 1, keepdims=True)
    gt = aff > thr
    eq = aff == thr
    need = cap - jnp.sum(gt.astype(F32), 1, keepdims=True)

    blk = min(n, 512)
    upper = (lax.broadcasted_iota(jnp.int32, (blk, blk), 0)
             < lax.broadcasted_iota(jnp.int32, (blk, blk), 1)).astype(BF16)

    def excl_cumsum(mask):
        parts, carry = [], jnp.zeros((E, 1), F32)
        for j in range(n // blk):
            m = mask[:, j * blk:(j + 1) * blk].astype(F32)
            parts.append(jnp.dot(m.astype(BF16), upper, preferred_element_type=F32) + carry)
            carry = carry + jnp.sum(m, 1, keepdims=True)
        return parts[0] if len(parts) == 1 else jnp.concatenate(parts, 1)

    sel = gt | (eq & (excl_cumsum(eq) < need))
    slot_ref[...] = jnp.where(sel, excl_cumsum(sel), -1.0)
    idx_ref[0] = jnp.zeros((cap, 128), F32)
    gate_ref[0] = jnp.zeros((cap, 128), F32)

    aff_ref[...] = aff
    rows = min(cap, 64)
    lane_e = lax.broadcasted_iota(jnp.int32, (rows, 128), 1)
    tok = lax.broadcasted_iota(jnp.int32, (1, 128), 1).astype(F32)

    for e in range(E):
        def per_rows(c, _, e=e):
            r0 = pl.multiple_of(c * rows, rows)
            s_col = (lax.broadcasted_iota(jnp.int32, (rows, 1), 0) + r0).astype(F32)
            acc_i = jnp.zeros((rows, 128), F32)
            acc_g = jnp.zeros((rows, 128), F32)
            for j in range(n // 128):
                hit = slot_ref[e:e + 1, j * 128:(j + 1) * 128] == s_col
                acc_i = acc_i + jnp.where(hit, tok + float(j * 128), 0.0)
                acc_g = acc_g + jnp.where(hit, aff_ref[e:e + 1, j * 128:(j + 1) * 128], 0.0)
            icol = jnp.sum(acc_i, 1, keepdims=True)
            gcol = jnp.sum(acc_g, 1, keepdims=True)
            idx_ref[0, pl.ds(r0, rows), :] = jnp.where(lane_e == e, icol, idx_ref[0, pl.ds(r0, rows), :])
            gate_ref[0, pl.ds(r0, rows), :] = jnp.where(lane_e == e, gcol, gate_ref[0, pl.ds(r0, rows), :])
            return 0

        lax.fori_loop(0, cap // rows, per_rows, 0)


def route(logits_t, off, n):
    B = logits_t.shape[0]
    cap = EC_CAPACITY_FACTOR * n // N_EXPERTS
    out = pl.BlockSpec((1, cap, 128), lambda b: (b, 0, 0))
    return pl.pallas_call(
        functools.partial(_route_kernel, n=n, cap=cap),
        grid=(B,),
        in_specs=[pl.BlockSpec((1, N_EXPERTS, n), lambda b: (b, 0, off // n))],
        out_specs=[out, out],
        out_shape=[jax.ShapeDtypeStruct((B, cap, 128), F32)] * 2,
        scratch_shapes=[pltpu.VMEM((N_EXPERTS, n), F32), pltpu.VMEM((N_EXPERTS, n), F32)],
        compiler_params=_cp(("parallel",)),
        name="route",
    )(logits_t)


GATHER_UNROLL = 8


def _gather_kernel(rows_ref, next_ref, h_hbm, o_ref, land0_ref, land1_ref, sems, *, T, tt):
    e = pl.program_id(0)
    lands = (land0_ref, land1_ref)

    def tile_copy(slot, s, r):
        src = h_hbm.at[pl.ds(pl.multiple_of(r * tt, tt), tt), :]
        dst = lands[slot].at[pl.ds(pl.multiple_of(s * tt, tt), tt), :]
        return pltpu.make_async_copy(src, dst, sems.at[slot])

    def issue(slot, ids_ref):
        def body(g, _):
            for i in range(GATHER_UNROLL):
                s = g * GATHER_UNROLL + i
                tile_copy(slot, s, ids_ref[0, 0, s]).start()
            return 0

        lax.fori_loop(0, T // GATHER_UNROLL, body, 0)

    def drain(slot):
        def body(g, _):
            for i in range(GATHER_UNROLL):
                tile_copy(slot, g * GATHER_UNROLL + i, 0).wait()
            return 0

        lax.fori_loop(0, T // GATHER_UNROLL, body, 0)

    @pl.when(e == 0)
    def _():
        issue(0, rows_ref)

    for slot in range(2):
        @pl.when(e % 2 == slot)
        def _(slot=slot):
            @pl.when(e + 1 < pl.num_programs(0))
            def _():
                issue(1 - slot, next_ref)

            drain(slot)
            o_ref[0] = _load_token_tiles(lands[slot], T, tt * LANES, BF16)


def gather_rows(rows, h_tiles, D):
    E, _, T = rows.shape
    tt = D // LANES
    ids = lambda off: pl.BlockSpec((1, 1, T), lambda e: (jnp.minimum(e + off, E - 1), 0, 0),
                                   memory_space=pltpu.SMEM)
    return pl.pallas_call(
        functools.partial(_gather_kernel, T=T, tt=tt),
        grid=(E,),
        in_specs=[ids(0), ids(1), pl.BlockSpec(memory_space=pl.ANY)],
        out_specs=pl.BlockSpec((1, T, D), lambda e: (e, 0, 0)),
        out_shape=jax.ShapeDtypeStruct((E, T, D), BF16),
        scratch_shapes=[pltpu.VMEM((T * tt, LANES), F32), pltpu.VMEM((T * tt, LANES), F32),
                        pltpu.SemaphoreType.DMA((2,))],
        compiler_params=_cp(("arbitrary",)),
        name="gather_rows",
    )(rows, rows, h_tiles)


def _expert_kernel(x_ref, gate_ref, wg_ref, wu_ref, wd_ref, o_ref):
    last = pl.num_programs(1) - 1
    x = x_ref[0]
    a = jnp.dot(x, wg_ref[0, 0].astype(BF16), preferred_element_type=F32)
    u = jnp.dot(x, wu_ref[0, 0].astype(BF16), preferred_element_type=F32)
    hid = (a * _sigmoid(a) * u).astype(BF16)
    part = jnp.dot(hid, wd_ref[0, 0].astype(BF16), preferred_element_type=F32)

    @pl.when(pl.program_id(1) == 0)
    def _():
        o_ref[0] = part

    @pl.when(jnp.logical_and(pl.program_id(1) != 0, pl.program_id(1) != last))
    def _():
        o_ref[0] += part

    @pl.when(pl.program_id(1) == last)
    def _():
        o_ref[0] = (o_ref[0] + part) * gate_ref[0]


def expert_ffn(xs, gate, w_gate, w_up, w_down, layer, tf=256):
    E, T, D = xs.shape
    FF = w_gate.shape[-1]
    return pl.pallas_call(
        _expert_kernel,
        grid=(E, FF // tf),
        in_specs=[pl.BlockSpec((1, T, D), lambda e, f: (e, 0, 0)),
                  pl.BlockSpec((1, T, 1), lambda e, f: (e, 0, 0)),
                  pl.BlockSpec((1, 1, D, tf), lambda e, f: (layer, e, 0, f)),
                  pl.BlockSpec((1, 1, D, tf), lambda e, f: (layer, e, 0, f)),
                  pl.BlockSpec((1, 1, tf, D), lambda e, f: (layer, e, f, 0))],
        out_specs=pl.BlockSpec((1, T, D), lambda e, f: (e, 0, 0)),
        out_shape=jax.ShapeDtypeStruct((E, T, D), F32),
        compiler_params=_cp(("parallel", "arbitrary")),
        name="expert_ffn",
    )(xs, gate, w_gate, w_up, w_down)


COMBINE_UNROLL = 8


def _combine_kernel(*refs, caps, bases, dh):
    n_sets = len(caps)
    idx_refs, y_refs = refs[:n_sets], refs[n_sets:2 * n_sets]
    o_ref, yt_ref = refs[2 * n_sets], refs[2 * n_sets + 1]
    tt = dh // LANES

    @pl.when(pl.program_id(2) == 0)
    def _():
        o_ref[...] = jnp.zeros_like(o_ref)

    for idx_ref, y_ref, cap, base in zip(idx_refs, y_refs, caps, bases):
        for j in range(tt):
            yt_ref[pl.ds(j, cap, stride=tt), :] = y_ref[0, :, j * LANES:(j + 1) * LANES]

        def body(g, _, idx_ref=idx_ref, base=base):
            s0 = g * COMBINE_UNROLL
            toks = [idx_ref[0, 0, 0, s0 + i] + base for i in range(COMBINE_UNROLL)]
            sums = [o_ref[0, toks[i], 0] + yt_ref[pl.ds(pl.multiple_of((s0 + i) * tt, tt), tt), :]
                    for i in range(COMBINE_UNROLL)]
            for i in range(COMBINE_UNROLL):
                o_ref[0, toks[i], 0] = sums[i]
            return 0

        lax.fori_loop(0, cap // COMBINE_UNROLL, body, 0)


def combine(ys, idx_sets, bases, dh=SUBLANES * LANES):
    E, T, D = ys.shape
    B = idx_sets[0].shape[0]
    caps = tuple(int(i.shape[-1]) for i in idx_sets)
    starts = np.concatenate([[0], np.cumsum([B * c for c in caps])[:-1]])
    idx_specs = [pl.BlockSpec((1, 1, 1, c), lambda b, hf, e: (b, e, 0, 0), memory_space=pltpu.SMEM) for c in caps]
    y_specs = [pl.BlockSpec((1, c, dh), lambda b, hf, e, blk0=int(st) // c: (e, blk0 + b, hf))
               for c, st in zip(caps, starts)]
    return pl.pallas_call(
        functools.partial(_combine_kernel, caps=caps, bases=tuple(bases), dh=dh),
        grid=(B, D // dh, E),
        in_specs=idx_specs + y_specs,
        out_specs=pl.BlockSpec((1, S_ALL, 1, SUBLANES, LANES), lambda b, hf, e: (b, 0, hf, 0, 0)),
        out_shape=jax.ShapeDtypeStruct((B, S_ALL, D // dh, SUBLANES, LANES), F32),
        scratch_shapes=[pltpu.VMEM((max(caps) * dh // LANES, LANES), F32)],
        compiler_params=_cp(("parallel", "parallel", "arbitrary")),
        name="combine",
    )(*[i[:, :, None, :] for i in idx_sets], *([ys] * len(caps)))


def _post_kernel(x_ref, y_ref, g2_ref, lg_ref, lb_ref, sh_ref, sc_ref, x2_ref, h_ref):
    y = _load_token_tiles(y_ref, ROW_TILE, x_ref.shape[1])
    x2 = _standardize(DEEPNORM_ALPHA * x_ref[...] + g2_ref[0] * y) * lg_ref[...] + lb_ref[...]
    x2_ref[...] = x2
    h_ref[...] = (_standardize(x2) * (1.0 + sc_ref[0]) + sh_ref[0]).astype(BF16)


def post_moe_norm(x1, y_tiles, modblk, ln_g, ln_b, modblk_next):
    R, D = x1.shape
    row = pl.BlockSpec((ROW_TILE, D), lambda i: (i, 0))
    mod = lambda part: pl.BlockSpec((1, 1, D), lambda i: (i, 0, part))
    vec = pl.BlockSpec((1, D), lambda i: (0, 0))
    y = pl.BlockSpec((ROW_TILE * D // LANES, LANES), lambda i: (i, 0))
    return pl.pallas_call(
        _post_kernel,
        grid=(R // ROW_TILE,),
        in_specs=[row, y, mod(5), vec, vec, mod(0), mod(1)],
        out_specs=[row, row],
        out_shape=[jax.ShapeDtypeStruct((R, D), F32), jax.ShapeDtypeStruct((R, D), BF16)],
        compiler_params=_cp(("parallel",)),
        name="post_moe_norm",
    )(x1, y_tiles, modblk, ln_g.reshape(1, -1), ln_b.reshape(1, -1), modblk_next, modblk_next)


def _final_kernel(x_ref, y_ref, g2_ref, lg_ref, lb_ref, x2_ref):
    y = _load_token_tiles(y_ref, ROW_TILE, x_ref.shape[1])
    x2_ref[...] = _standardize(DEEPNORM_ALPHA * x_ref[...] + g2_ref[0] * y) * lg_ref[...] + lb_ref[...]


def final_norm(x1, y_tiles, modblk, ln_g, ln_b):
    R, D = x1.shape
    n_lat = R // S_ALL * LAT_BLOCKS
    blk = lambda i: i + i // LAT_BLOCKS
    vec = pl.BlockSpec((1, D), lambda i: (0, 0))
    return pl.pallas_call(
        _final_kernel,
        grid=(n_lat,),
        in_specs=[pl.BlockSpec((ROW_TILE, D), lambda i: (blk(i), 0)),
                  pl.BlockSpec((ROW_TILE * D // LANES, LANES), lambda i: (blk(i), 0)),
                  pl.BlockSpec((1, 1, D), lambda i: (blk(i), 0, 5)), vec, vec],
        out_specs=pl.BlockSpec((ROW_TILE, D), lambda i: (i, 0)),
        out_shape=jax.ShapeDtypeStruct((n_lat * ROW_TILE, D), F32),
        compiler_params=_cp(("parallel",)),
        name="final_norm",
    )(x1, y_tiles, modblk, ln_g.reshape(1, -1), ln_b.reshape(1, -1))


def _s5_matrices(a_re, a_im, log_dt, b_re, b_im):
    dt = jnp.exp(log_dt)[:, None]
    mag = jnp.exp(a_re * dt)
    ab_re, ab_im = mag * jnp.cos(a_im * dt), mag * jnp.sin(a_im * dt)
    den = a_re * a_re + a_im * a_im
    num_re, num_im = ab_re - 1.0, ab_im
    coef_re = (num_re * a_re + num_im * a_im) / den
    coef_im = (num_im * a_re - num_re * a_im) / den
    bb_re = coef_re[..., None] * b_re - coef_im[..., None] * b_im
    bb_im = coef_re[..., None] * b_im + coef_im[..., None] * b_re
    gs = S5_GROUPS // S5_SLABS
    eye = jnp.eye(gs, dtype=F32)

    def slabs(t):
        t = t.reshape(S5_SLABS, gs, S5_STATE, S5_GROUP_CH)
        return jnp.einsum('sgpi,gh->sgihp', t, eye).reshape(S5_SLABS, gs * S5_GROUP_CH, gs * S5_STATE)

    lam = jnp.stack([ab_re.reshape(-1), ab_im.reshape(-1)], 0)
    return lam, jnp.concatenate([slabs(bb_re), slabs(bb_im)], -1)


def _s5_readout_matrix(c_re, c_im):
    gs = S5_GROUPS // S5_SLABS
    eye = jnp.eye(gs, dtype=F32)

    def slabs(t):
        t = t.reshape(S5_SLABS, gs, S5_GROUP_CH, S5_STATE)
        return jnp.einsum('sgip,gh->sgphi', t, eye).reshape(S5_SLABS, gs * S5_STATE, gs * S5_GROUP_CH)

    return jnp.concatenate([slabs(c_re), -slabs(c_im)], 1)


def _mla_weights(w_uq, w_ukv):
    qk = MLA_NOPE_DIM + MLA_ROPE_DIM
    wq = w_uq.reshape(MLA_Q_LORA, MLA_HEADS, qk)
    wq = jnp.pad(wq, ((0, 0), (0, 0), (0, MLA_QK_PAD - qk))).reshape(MLA_Q_LORA, MLA_HEADS * MLA_QK_PAD)
    wkv = w_ukv.reshape(MLA_KV_LORA, MLA_HEADS, MLA_NOPE_DIM + MLA_V_DIM)
    wk = jnp.pad(wkv[..., :MLA_NOPE_DIM], ((0, 0), (0, 0), (0, MLA_QK_PAD - MLA_NOPE_DIM)))
    wv = wkv[..., MLA_NOPE_DIM:]
    wkv_pad = jnp.concatenate([wk.reshape(MLA_KV_LORA, -1), wv.reshape(MLA_KV_LORA, -1)], 1)
    return wq.astype(BF16), wkv_pad.astype(BF16)


def kernel(x, c, ctx, c_ctx, ada_w, ada_b, w_in, s5_a_re_f, s5_a_im_f, s5_log_dt_f, s5_a_re_b, s5_a_im_b,
           s5_log_dt_b, s5_b_re, s5_b_im, s5_c_re, s5_c_im, s5_d, s5_w_glu, gqa_q_norm, gqa_k_norm,
           ret_decay_f, ret_decay_b, ret_norm, mla_q_norm, mla_kv_norm, mla_w_uq, mla_w_ukv,
           w_branch, w_gate, b_gate, w_out, ln1_g, ln1_b, router_w, moe_w_gate, moe_w_up, moe_w_down,
           ln2_g, ln2_b):
    B, N, D = x.shape
    R = B * S_ALL
    assert (B, N, D) == (BATCH, SEQ, D_MODEL) and RET_CHUNK == CTX_LEN == ROW_TILE

    cc = jnp.zeros((16, D), F32).at[:B].set(c).at[B].set(c_ctx)
    mod = ada_modulation(cc, ada_w, ada_b)
    sel = np.concatenate([np.r_[np.full(LAT_BLOCKS, b), B] for b in range(B)]).astype(np.int32)
    modblks = [mod[l][sel].reshape(R // ROW_TILE, 1, 6 * D) for l in range(DEPTH)]

    X = jnp.concatenate([x, ctx], 1).reshape(R, D)
    h = modulate_rows(X, modblks[0], 0, 1)

    for l in range(DEPTH):
        need_ctx = l < DEPTH - 1
        modblk = modblks[l]
        w_in_p = jnp.pad(w_in[l], ((0, 0), (0, IN_PAD - IN_TOTAL))).astype(BF16)
        wq_pad, wkv_pad = _mla_weights(mla_w_uq[l], mla_w_ukv[l])
        u, gq, gk, gv, rq, rk, rv, rg, mq, mk, mv = in_proj_branches(
            h, w_in_p, gqa_q_norm[l], gqa_k_norm[l], mla_q_norm[l], mla_kv_norm[l], wq_pad, wkv_pad)
        to3 = lambda t: t.reshape(B, S_ALL, t.shape[-1])

        lam_f, bb_f = _s5_matrices(s5_a_re_f[l], s5_a_im_f[l], s5_log_dt_f[l], s5_b_re[l], s5_b_im[l])
        lam_b, bb_b = _s5_matrices(s5_a_re_b[l], s5_a_im_b[l], s5_log_dt_b[l], s5_b_re[l], s5_b_im[l])
        lam = jnp.broadcast_to(jnp.stack([lam_f, lam_b], 0)[:, :, None, :], (2, 2, S5_SEG, S5_MODES))
        bb = jnp.stack([bb_f, bb_b], 0).astype(BF16)
        cmat = _s5_readout_matrix(s5_c_re[l], s5_c_im[l]).astype(BF16)
        o_s5 = s5_output(s5_scan(u, bb, lam, cmat), u, s5_d[l], s5_w_glu[l].astype(BF16))

        gq3, gk3, gv3 = to3(gq), to3(gk), to3(gv)
        att = functools.partial(attention, kv_heads=GQA_KV_HEADS, groups=GQA_HEADS // GQA_KV_HEADS,
                                dk=GQA_HEAD_DIM, dv=GQA_HEAD_DIM)
        o_lat = att(gq3, gk3, gv3, q_rows=SEQ, q_off=0, kv_rows=S_ALL, kv_off=0, tq=512)
        no_ctx = jnp.zeros((B, CTX_LEN, BRANCH_W), BF16)
        o_ctx = (att(gq3, gk3, gv3, q_rows=CTX_LEN, q_off=SEQ, kv_rows=CTX_LEN, kv_off=SEQ, tq=256)
                 if need_ctx else no_ctx)
        o_gqa = jnp.concatenate([o_lat, o_ctx], 1).reshape(R, BRANCH_W)

        lgf = -jnp.exp(ret_decay_f[l])
        lgb = -jnp.exp(ret_decay_b[l])
        rkt = jnp.swapaxes(to3(rk), 1, 2)
        o_ret = retention(to3(rq), rkt, to3(rv), to3(rg), ret_norm[l], lgf, lgb, S_ALL).reshape(R, BRANCH_W)

        matt = functools.partial(attention, kv_heads=MLA_HEADS, groups=1, dk=MLA_QK_PAD, dv=MLA_V_DIM)
        m_lat = matt(to3(mq), to3(mk), to3(mv), q_rows=SEQ, q_off=0, kv_rows=S_ALL, kv_off=0, tq=1024)
        m_ctx = (matt(to3(mq), to3(mk), to3(mv), q_rows=CTX_LEN, q_off=SEQ, kv_rows=CTX_LEN, kv_off=SEQ, tq=256)
                 if need_ctx else no_ctx)
        o_mla = jnp.concatenate([m_lat, m_ctx], 1).reshape(R, BRANCH_W)

        merged = merge_branches(h, (o_s5, o_gqa, o_ret, o_mla), w_gate, l, b_gate[l], w_branch[l].astype(BF16))
        router_pad = jnp.pad(router_w[l], ((0, 0), (0, 128 - N_EXPERTS)))
        x1, h2, logits = out_proj_norm(merged, X, w_out[l].astype(BF16), modblk, ln1_g[l], ln1_b[l], router_pad)

        logits_t = jnp.swapaxes(logits.reshape(B, S_ALL, 128)[:, :, :N_EXPERTS], 1, 2)
        sets = [(0, SEQ)] + ([(SEQ, CTX_LEN)] if need_ctx else [])
        idx_sets, row_parts, gate_parts = [], [], []
        sample_row0 = (jnp.arange(B, dtype=jnp.int32) * S_ALL)[:, None, None]
        for off, n in sets:
            idx_f, gate_f = route(logits_t, off, n)
            idx = jnp.swapaxes(idx_f[:, :, :N_EXPERTS], 1, 2).astype(jnp.int32)
            gate = jnp.swapaxes(gate_f[:, :, :N_EXPERTS], 1, 2)
            idx_sets.append(idx)
            row_parts.append(jnp.swapaxes(idx + sample_row0 + off, 0, 1).reshape(N_EXPERTS, -1))
            gate_parts.append(jnp.swapaxes(gate, 0, 1).reshape(N_EXPERTS, -1))
        rows = jnp.concatenate(row_parts, 1)[:, None, :]
        gates = jnp.concatenate(gate_parts, 1)[:, :, None]
        xs = gather_rows(rows, h2, D)
        ys = expert_ffn(xs, gates, moe_w_gate, moe_w_up, moe_w_down, l)
        moe = combine(ys, idx_sets, [off for off, _ in sets]).reshape(R * D // LANES, LANES)
        if l == DEPTH - 1:
            return final_norm(x1, moe, modblk, ln2_g[l], ln2_b[l]).reshape(B, SEQ, D)
        X, h = post_moe_norm(x1, moe, modblk, ln2_g[l], ln2_b[l], modblks[l + 1])
```
